```python
import math
import jax, jax.numpy as jnp
from jax import lax
import numpy as np


D_MODEL = 1024
BATCH = 16
SEQ = 2048
DEPTH = 1
DEC_BATCH = 32
DEC_SEQ = 32
PAST_LEN = 2048

CHUNK = 64
D_MIX = 2 * D_MODEL
W_POOL = D_MIX // 2
W_MLSTM = D_MIX - W_POOL
POOL_WINDOWS = (2, 4, 8, 16)
N_POOL_GROUPS = 4
POOL_GW = W_POOL // N_POOL_GROUPS
POOL_BUF = 15
N_HEADS = 4
HEAD_DIM = W_MLSTM // N_HEADS
EPS = 1e-6
IN_SECTIONS = (W_POOL, W_POOL, W_MLSTM, W_MLSTM, W_MLSTM, W_MLSTM, W_MLSTM, N_HEADS, N_HEADS)
D_IN = 2 * W_POOL + 5 * W_MLSTM + 2 * N_HEADS

kernel_name = 'hybrid_pool_mlstm_streaming_step'


def _split_points():
    pts, acc = [], 0
    for s in IN_SECTIONS[:-1]:
        acc += s
        pts.append(acc)
    return pts


def rmsnorm(x, g):
    xf = x.astype(jnp.float32)
    r = lax.rsqrt(jnp.mean(xf * xf, axis=-1, keepdims=True) + EPS)
    return (xf * r).astype(x.dtype) * g


def head_layernorm(h):
    mu = jnp.mean(h, axis=-1, keepdims=True)
    hc = h - mu
    out = hc * lax.rsqrt(jnp.mean(hc * hc, axis=-1, keepdims=True) + EPS)
    B, H, T, Dh = h.shape
    return out.transpose(0, 2, 1, 3).reshape(B, T, H * Dh)


def pool_mixer(xp, buf, start, w_pool, pool_scale):
    B, T, W = xp.shape
    ext = jnp.concatenate([buf, xp], axis=1).astype(jnp.float32)
    cs = jnp.cumsum(ext, axis=1)
    cs = jnp.concatenate([jnp.zeros((B, 1, W), jnp.float32), cs], axis=1)
    top = cs[:, POOL_BUF + 1:POOL_BUF + 1 + T]
    pos = jnp.arange(T) + start
    means = []
    for g, w in enumerate(POOL_WINDOWS):
        sl = slice(g * POOL_GW, (g + 1) * POOL_GW)
        win = top[..., sl] - cs[:, POOL_BUF + 1 - w:POOL_BUF + 1 - w + T, sl]
        cnt = jnp.minimum(pos + 1, w).astype(jnp.float32)
        means.append(win / cnt[None, :, None])
    pooled = jnp.concatenate(means, axis=-1) - xp.astype(jnp.float32)
    pooled = pooled.astype(xp.dtype).reshape(B, T, N_POOL_GROUPS, POOL_GW)
    mixed = jnp.einsum('btgc,gcd->btgd', pooled, w_pool).reshape(B, T, W)
    return mixed * pool_scale


def mlstm_block(carry, blk):
    C0, n0, m0 = carry
    q, k, v, ig, lf = blk
    L = q.shape[2]
    F = jnp.cumsum(lf, axis=-1)
    a = ig - F
    m = F + jnp.maximum(m0[..., None], lax.cummax(a, axis=a.ndim - 1))
    causal = jnp.tril(jnp.ones((L, L), dtype=bool))
    logD = F[..., :, None] + a[..., None, :] - m[..., :, None]
    D = jnp.exp(jnp.where(causal, logD, -jnp.inf))
    decay0 = jnp.exp(m0[..., None] + F - m)
    S = jnp.einsum('bhtd,bhsd->bhts', q, k) * D
    num = jnp.einsum('bhts,bhsd->bhtd', S, v) + decay0[..., None] * jnp.einsum('bhtk,bhkv->bhtv', q, C0)
    nq = jnp.sum(S, axis=-1) + decay0 * jnp.einsum('bhtk,bhk->bht', q, n0)
    h = num / jnp.maximum(jnp.abs(nq), jnp.exp(-m))[..., None]
    mL = m[..., -1]
    wL = jnp.exp(a + F[..., -1:] - mL[..., None])
    dL = jnp.exp(m0 + F[..., -1] - mL)
    C = dL[..., None, None] * C0 + jnp.einsum('bhs,bhsk,bhsv->bhkv', wL, k, v)
    n = dL[..., None] * n0 + jnp.einsum('bhs,bhsk->bhk', wL, k)
    return (C, n, mL), h


def mlstm_sequence(q, k, v, ig, lf, C0, n0, m0):
    B, H, T, Dh = q.shape
    L = min(T, CHUNK)
    NB = T // L

    def blocks(t):
        return jnp.moveaxis(t.reshape(t.shape[:2] + (NB, L) + t.shape[3:]), 2, 0)

    carry0 = (C0.astype(jnp.float32), n0.astype(jnp.float32), m0.astype(jnp.float32))
    carry, hs = lax.scan(mlstm_block, carry0, (blocks(q), blocks(k), blocks(v), blocks(ig), blocks(lf)))
    h = jnp.moveaxis(hs, 0, 2).reshape(B, H, T, Dh)
    return h, carry


def mixer_layer(x, c, pool_buf, C0, n0, m0, start, w_ada, b_ada, g_norm, w_in, b_i, b_f,
                w_pool, pool_scale, g_head, w_out):
    B, T, _ = x.shape
    mod = jnp.einsum('bd,de->be', jax.nn.silu(c), w_ada) + b_ada
    shift, scale, gate = jnp.split(mod[:, None, :], 3, axis=-1)
    h = rmsnorm(x, g_norm) * (1 + scale) + shift
    u = jnp.einsum('btd,de->bte', h, w_in)
    xp, zp, q, k, v, o, zm, ig, fg = jnp.split(u, _split_points(), axis=-1)
    y_pool = pool_mixer(xp, pool_buf, start, w_pool, pool_scale) * jax.nn.silu(zp)
    new_buf = jnp.concatenate([pool_buf, xp], axis=1)[:, -POOL_BUF:]
    def heads(t):
        return t.reshape(B, T, N_HEADS, HEAD_DIM).transpose(0, 2, 1, 3).astype(jnp.float32)
    qh = heads(q)
    kh = heads(k) * (HEAD_DIM ** -0.5)
    vh = heads(v)
    igh = (ig + b_i).astype(jnp.float32).transpose(0, 2, 1)
    lfh = jax.nn.log_sigmoid((fg + b_f).astype(jnp.float32)).transpose(0, 2, 1)
    hm, (C1, n1, m1) = mlstm_sequence(qh, kh, vh, igh, lfh, C0, n0, m0)
    hm = head_layernorm(hm).astype(x.dtype) * g_head
    y_m = hm * jax.nn.sigmoid(o) * jax.nn.silu(zm)
    y = jnp.einsum('bte,ed->btd', jnp.concatenate([y_pool, y_m], axis=-1), w_out)
    x = x + gate * y
    return x, new_buf, C1.astype(x.dtype), n1.astype(x.dtype), m1.astype(x.dtype)


def setup_inputs(seed: int = 0) -> dict:
    key = jax.random.key(seed)
    ks = jax.random.split(key, 24)
    nrm = jax.random.normal
    f32 = jnp.float32
    return {
        'x_prompt': nrm(ks[0], (BATCH, SEQ, D_MODEL), f32),
        'x_sample': nrm(ks[1], (DEC_BATCH, DEC_SEQ, D_MODEL), f32),
        'c_prompt': nrm(ks[2], (BATCH, D_MODEL), f32),
        'c_sample': nrm(ks[3], (DEC_BATCH, D_MODEL), f32),
        'state_pool': nrm(ks[4], (DEPTH, DEC_BATCH, POOL_BUF, W_POOL), f32),
        'state_C': 0.1 * nrm(ks[5], (DEPTH, DEC_BATCH, N_HEADS, HEAD_DIM, HEAD_DIM), f32),
        'state_n': 0.1 * nrm(ks[6], (DEPTH, DEC_BATCH, N_HEADS, HEAD_DIM), f32),
        'state_m': nrm(ks[7], (DEPTH, DEC_BATCH, N_HEADS), f32),
        'w_ada': 0.5 * nrm(ks[8], (DEPTH, D_MODEL, 3 * D_MODEL), f32) * D_MODEL ** -0.5,
        'b_ada': 0.01 * nrm(ks[9], (DEPTH, 3 * D_MODEL), f32),
        'g_norm': 1.0 + 0.02 * nrm(ks[10], (DEPTH, D_MODEL), f32),
        'w_in': nrm(ks[11], (DEPTH, D_MODEL, D_IN), f32) * D_MODEL ** -0.5,
        'b_i': 0.1 * nrm(ks[12], (DEPTH, N_HEADS), f32),
        'b_f': jnp.linspace(3.0, 6.0, N_HEADS, dtype=f32)[None, :] + 0.1 * nrm(ks[13], (DEPTH, N_HEADS), f32),
        'w_pool': nrm(ks[14], (DEPTH, N_POOL_GROUPS, POOL_GW, POOL_GW), f32) * POOL_GW ** -0.5,
        'pool_scale': 1.0 + 0.02 * nrm(ks[15], (DEPTH, W_POOL), f32),
        'g_head': 1.0 + 0.02 * nrm(ks[16], (DEPTH, W_MLSTM), f32),
        'w_out': nrm(ks[17], (DEPTH, D_MIX, D_MODEL), f32) * D_MIX ** -0.5,
        'g_final': 1.0 + 0.02 * nrm(ks[18], (D_MODEL,), f32),
    }


def reference(x_prompt, x_sample, c_prompt, c_sample, state_pool, state_C, state_n, state_m,
              w_ada, b_ada, g_norm, w_in, b_i, b_f, w_pool, pool_scale, g_head, w_out, g_final):
    xpr, xsm = x_prompt, x_sample
    dt = x_prompt.dtype
    pp, pc, pn, pm, sp, sc, sn, sm = [], [], [], [], [], [], [], []
    for l in range(DEPTH):
        lw = (w_ada[l], b_ada[l], g_norm[l], w_in[l], b_i[l], b_f[l], w_pool[l], pool_scale[l], g_head[l], w_out[l])
        zb = jnp.zeros((BATCH, POOL_BUF, W_POOL), dt)
        zC = jnp.zeros((BATCH, N_HEADS, HEAD_DIM, HEAD_DIM), dt)
        zn = jnp.zeros((BATCH, N_HEADS, HEAD_DIM), dt)
        zm = jnp.zeros((BATCH, N_HEADS), dt)
        xpr, b1, C1, n1, m1 = mixer_layer(xpr, c_prompt, zb, zC, zn, zm, 0, *lw)
        xsm, b2, C2, n2, m2 = mixer_layer(xsm, c_sample, state_pool[l], state_C[l], state_n[l], state_m[l],
                                          PAST_LEN, *lw)
        pp.append(b1); pc.append(C1); pn.append(n1); pm.append(m1)
        sp.append(b2); sc.append(C2); sn.append(n2); sm.append(m2)
    y_prompt = rmsnorm(xpr, g_final)
    y_sample = rmsnorm(xsm, g_final)
    return (y_prompt, y_sample, jnp.stack(pp), jnp.stack(pc), jnp.stack(pn), jnp.stack(pm),
            jnp.stack(sp), jnp.stack(sc), jnp.stack(sn), jnp.stack(sm))
```

```python
import functools

import jax
import jax.numpy as jnp
from jax import lax
from jax.experimental import pallas as pl
from jax.experimental.pallas import tpu as pltpu

F32 = jnp.float32
BF16 = jnp.bfloat16

D_MODEL = 1024
W_POOL = 1024
W_MLSTM = 1024
POOL_WINDOWS = (2, 4, 8, 16)
N_POOL_GROUPS = 4
POOL_GW = W_POOL // N_POOL_GROUPS
POOL_BUF = 15
N_HEADS = 4
HEAD_DIM = W_MLSTM // N_HEADS
EPS = 1e-6
N_MAIN_SECTIONS = 7
LANES = 128
PAST_LEN = 2048
HIST = 16
VMEM_LIMIT_BYTES = 56 * 1024 * 1024


def _silu(z):
    return z * jax.nn.sigmoid(z)


def _log_sigmoid(z):
    return jnp.minimum(z, 0.0) - jnp.log1p(jnp.exp(-jnp.abs(z)))


def _mod_kernel(c_ref, w_ref, b_ref, o_ref):
    c = c_ref[...]
    o_ref[...] = jnp.dot(_silu(c), w_ref[...], preferred_element_type=F32) + b_ref[...]


def _adaln_mod(c, w_ada, b_ada):
    nb = c.shape[0]
    n_out = w_ada.shape[1]
    blk = D_MODEL
    return pl.pallas_call(
        _mod_kernel,
        grid=(n_out // blk,),
        in_specs=[
            pl.BlockSpec((nb, D_MODEL), lambda j: (0, 0)),
            pl.BlockSpec((D_MODEL, blk), lambda j: (0, j)),
            pl.BlockSpec((1, blk), lambda j: (0, j)),
        ],
        out_specs=pl.BlockSpec((nb, blk), lambda j: (0, j)),
        out_shape=jax.ShapeDtypeStruct((nb, n_out), F32),
        name="adaln_mod",
    )(c, w_ada, b_ada.reshape(1, n_out))


def _seg_scan(x, tl, seg, combine, fill):
    k = 1
    while k < seg:
        shifted = pltpu.roll(x, k, 0)
        x = combine(x, jnp.where(tl >= k, shifted, fill))
        k *= 2
    return x


def _rows_of(per_batch_rows, TB):
    parts = [jnp.broadcast_to(r, (TB, LANES)) for r in per_batch_rows]
    return parts[0] if len(parts) == 1 else jnp.concatenate(parts, axis=0)


def _layer_kernel(*refs, BB, TB, start, has_state):
    R = BB * TB
    D, W, DH, GW = D_MODEL, W_POOL, HEAD_DIM, POOL_GW
    it = iter(refs)
    x_ref = next(it)
    mod_ref = next(it)
    if has_state:
        pool0_ref, C0_ref, n0_ref, m0_ref = next(it), next(it), next(it), next(it)
    gnorm_ref, win_ref, wg_ref, gbias_ref = next(it), next(it), next(it), next(it)
    wpool_ref, pscale_ref, ghead_ref, wout_ref, gfinal_ref = (next(it), next(it), next(it),
                                                             next(it), next(it))
    y_ref, pool_out_ref, C_ref, n_ref, m_ref = next(it), next(it), next(it), next(it), next(it)
    h_s, ext_s, pooled_s, mixed_s, q_s, k_s, v_s, ycat_s = it

    t = pl.program_id(1)

    @pl.when(t == 0)
    def _init():
        if has_state:
            ext_s[:, 0:1, :] = jnp.zeros((BB, 1, W), F32)
            ext_s[:, 1:HIST, :] = pool0_ref[...]
            C_ref[...] = C0_ref[...]
            n_ref[...] = n0_ref[...]
            m_ref[...] = m0_ref[...]
        else:
            ext_s[:, 0:HIST, :] = jnp.zeros((BB, HIST, W), F32)
            C_ref[...] = jnp.zeros_like(C_ref)
            n_ref[...] = jnp.zeros_like(n_ref)
            m_ref[...] = jnp.zeros_like(m_ref)

    for bb in range(BB):
        x = x_ref[bb]
        r = lax.rsqrt(jnp.mean(x * x, axis=-1, keepdims=True) + EPS)
        shift = mod_ref[bb, :, 0:D]
        scale = mod_ref[bb, :, D:2 * D]
        h = (x * r) * gnorm_ref[...] * (1.0 + scale) + shift
        h_s[bb * TB:(bb + 1) * TB, :] = h.astype(BF16)

    def proj(sec):
        return jnp.dot(h_s[...], win_ref[:, sec * W:(sec + 1) * W], preferred_element_type=F32)

    xp = proj(0)
    for bb in range(BB):
        ext_s[bb, HIST:HIST + TB, :] = xp[bb * TB:(bb + 1) * TB, :]
    pos = start + t * TB + lax.broadcasted_iota(jnp.int32, (TB, 1), 0)
    for g, w in enumerate(POOL_WINDOWS):
        cols = slice(g * GW, (g + 1) * GW)
        inv_cnt = 1.0 / jnp.minimum(pos + 1, w).astype(F32)
        for bb in range(BB):
            cur = ext_s[bb, HIST:HIST + TB, cols]
            win = cur
            for j in range(1, w):
                win = win + ext_s[bb, HIST - j:HIST - j + TB, cols]
            pooled_s[bb * TB:(bb + 1) * TB, cols] = (win * inv_cnt - cur).astype(BF16)
    for g in range(N_POOL_GROUPS):
        cols = slice(g * GW, (g + 1) * GW)
        mixed = jnp.dot(pooled_s[:, cols], wpool_ref[g], preferred_element_type=F32)
        mixed_s[:, cols] = mixed * pscale_ref[:, cols]
    ycat_s[:, 0:W] = (mixed_s[...] * _silu(proj(1))).astype(BF16)

    @pl.when(t == pl.num_programs(1) - 1)
    def _emit_pool():
        pool_out_ref[...] = ext_s[:, TB + 1:TB + HIST, :]

    for bb in range(BB):
        ext_s[bb, 0:HIST, :] = ext_s[bb, TB:TB + HIST, :]

    q_s[...] = proj(2).astype(BF16)
    k_s[...] = (proj(3) * (DH ** -0.5)).astype(BF16)
    v_s[...] = proj(4).astype(BF16)
    gates = jnp.dot(h_s[...], wg_ref[...], preferred_element_type=F32) + gbias_ref[...]

    tl = lax.broadcasted_iota(jnp.int32, (R, LANES), 0) & (TB - 1)
    ig = gates
    lf = _log_sigmoid(pltpu.roll(gates, LANES - N_HEADS, 1))
    Fc = _seg_scan(lf, tl, TB, jnp.add, 0.0)
    a = ig - Fc
    cmax = _seg_scan(a, tl, TB, jnp.maximum, -jnp.inf)
    m0_rows = [m_ref[bb] for bb in range(BB)]
    m0 = _rows_of(m0_rows, TB)
    m = Fc + jnp.maximum(m0, cmax)
    Fl_rows = [Fc[(bb + 1) * TB - 1:(bb + 1) * TB, :] for bb in range(BB)]
    mL_rows = [m[(bb + 1) * TB - 1:(bb + 1) * TB, :] for bb in range(BB)]
    Fl = _rows_of(Fl_rows, TB)
    mL = _rows_of(mL_rows, TB)
    decay0 = jnp.exp(m0 + Fc - m)
    wL = jnp.exp(a + Fl - mL)
    emm = jnp.exp(-m)
    Fm = Fc - m
    aT = a.T
    causal = (lax.broadcasted_iota(jnp.int32, (TB, TB), 0)
              >= lax.broadcasted_iota(jnp.int32, (TB, TB), 1))

    for bb in range(BB):
        rows = slice(bb * TB, (bb + 1) * TB)
        dL = jnp.exp(m0_rows[bb] + Fl_rows[bb] - mL_rows[bb])
        for hd in range(N_HEADS):
            cols = slice(hd * DH, (hd + 1) * DH)
            q = q_s[rows, cols]
            k = k_s[rows, cols]
            v = v_s[rows, cols]
            s = lax.dot_general(q, k, (((1,), (1,)), ((), ())), preferred_element_type=F32)
            logD = Fm[rows, hd:hd + 1] + aT[hd:hd + 1, rows]
            S = s * jnp.exp(jnp.where(causal, logD, -jnp.inf))
            C0 = C_ref[bb, hd]
            n0 = n_ref[bb, hd:hd + 1, :]
            d0 = decay0[rows, hd:hd + 1]
            qn = jnp.sum(q.astype(F32) * n0, axis=-1, keepdims=True)
            nq = jnp.sum(S, axis=-1, keepdims=True) + d0 * qn
            num = (jnp.dot(S.astype(BF16), v, preferred_element_type=F32)
                   + d0 * jnp.dot(q, C0.astype(BF16), preferred_element_type=F32))
            den = jnp.maximum(jnp.abs(nq), emm[rows, hd:hd + 1])
            hh = num * (1.0 / den)
            hc = hh - jnp.mean(hh, axis=-1, keepdims=True)
            hn = hc * lax.rsqrt(jnp.mean(hc * hc, axis=-1, keepdims=True) + EPS)
            mixed_s[rows, cols] = hn * ghead_ref[:, cols]
            kw = k.astype(F32) * wL[rows, hd:hd + 1]
            dl = dL[:, hd:hd + 1]
            C_ref[bb, hd] = dl * C0 + lax.dot_general(
                kw.astype(BF16), v, (((0,), (0,)), ((), ())), preferred_element_type=F32)
            n_ref[bb, hd:hd + 1, :] = dl * n0 + jnp.sum(kw, axis=0, keepdims=True)
        m_ref[bb] = mL_rows[bb]

    mixed_s[...] = mixed_s[...] * jax.nn.sigmoid(proj(5))
    ycat_s[:, W:2 * W] = (mixed_s[...] * _silu(proj(6))).astype(BF16)

    y = jnp.dot(ycat_s[...], wout_ref[...], preferred_element_type=F32)
    for bb in range(BB):
        gate = mod_ref[bb, :, 2 * D:3 * D]
        xn = x_ref[bb] + gate * y[bb * TB:(bb + 1) * TB, :]
        r = lax.rsqrt(jnp.mean(xn * xn, axis=-1, keepdims=True) + EPS)
        y_ref[bb] = (xn * r) * gfinal_ref[...]


def _const_spec(shape):
    nd = len(shape)
    return pl.BlockSpec(shape, lambda b, t: (0,) * nd, pipeline_mode=pl.Buffered(1))


def _run_layer(x, mod, state, weights, *, BB, TB, start):
    B, T, D = x.shape
    W, H, DH = W_POOL, N_HEADS, HEAD_DIM
    R = BB * TB
    has_state = state is not None
    grid = (B // BB, T // TB)

    in_specs = [
        pl.BlockSpec((BB, TB, D), lambda b, t: (b, t, 0)),
        pl.BlockSpec((BB, 1, 3 * D), lambda b, t: (b, 0, 0)),
    ]
    args = [x, mod.reshape(B, 1, 3 * D)]
    if has_state:
        pool0, C0, n0, m0 = state
        in_specs += [
            pl.BlockSpec((BB, POOL_BUF, W), lambda b, t: (b, 0, 0)),
            pl.BlockSpec((BB, H, DH, DH), lambda b, t: (b, 0, 0, 0)),
            pl.BlockSpec((BB, H, DH), lambda b, t: (b, 0, 0)),
            pl.BlockSpec((BB, 1, LANES), lambda b, t: (b, 0, 0)),
        ]
        m0p = jnp.pad(m0, ((0, 0), (0, LANES - H))).reshape(B, 1, LANES)
        args += [pool0, C0, n0, m0p]
    in_specs += [_const_spec(w.shape) for w in weights]
    args += list(weights)

    out_shape = (
        jax.ShapeDtypeStruct((B, T, D), F32),
        jax.ShapeDtypeStruct((B, POOL_BUF, W), F32),
        jax.ShapeDtypeStruct((B, H, DH, DH), F32),
        jax.ShapeDtypeStruct((B, H, DH), F32),
        jax.ShapeDtypeStruct((B, 1, LANES), F32),
    )
    out_specs = (
        pl.BlockSpec((BB, TB, D), lambda b, t: (b, t, 0)),
        pl.BlockSpec((BB, POOL_BUF, W), lambda b, t: (b, 0, 0)),
        pl.BlockSpec((BB, H, DH, DH), lambda b, t: (b, 0, 0, 0)),
        pl.BlockSpec((BB, H, DH), lambda b, t: (b, 0, 0)),
        pl.BlockSpec((BB, 1, LANES), lambda b, t: (b, 0, 0)),
    )
    scratch = [
        pltpu.VMEM((R, D), BF16),
        pltpu.VMEM((BB, HIST + TB, W), F32),
        pltpu.VMEM((R, W), BF16),
        pltpu.VMEM((R, W), F32),
        pltpu.VMEM((R, W), BF16),
        pltpu.VMEM((R, W), BF16),
        pltpu.VMEM((R, W), BF16),
        pltpu.VMEM((R, 2 * W), BF16),
    ]
    y, pool, C, n, m = pl.pallas_call(
        functools.partial(_layer_kernel, BB=BB, TB=TB, start=start, has_state=has_state),
        grid=grid,
        in_specs=in_specs,
        out_specs=out_specs,
        out_shape=out_shape,
        scratch_shapes=scratch,
        compiler_params=pltpu.CompilerParams(
            dimension_semantics=("parallel", "arbitrary"),
            vmem_limit_bytes=VMEM_LIMIT_BYTES),
        name="layer_state" if has_state else "layer_fresh",
    )(*args)
    return y, pool, C, n, m[:, 0, :H]


def kernel(x_prompt, x_sample, c_prompt, c_sample, state_pool, state_C, state_n, state_m,
           w_ada, b_ada, g_norm, w_in, b_i, b_f, w_pool, pool_scale, g_head, w_out, g_final):
    depth = w_ada.shape[0]
    assert depth == 1, "single-layer trunk"
    l = 0
    nbp = x_prompt.shape[0]
    n_main = N_MAIN_SECTIONS * W_POOL

    mod = _adaln_mod(jnp.concatenate([c_prompt, c_sample], axis=0), w_ada[l], b_ada[l])

    w_gate = jnp.pad(w_in[l][:, n_main:], ((0, 0), (0, LANES - 2 * N_HEADS))).astype(BF16)
    gate_bias = jnp.pad(jnp.concatenate([b_i[l], b_f[l]]), (0, LANES - 2 * N_HEADS)).reshape(1, LANES)
    weights = (
        g_norm[l].reshape(1, D_MODEL),
        w_in[l][:, :n_main].astype(BF16),
        w_gate,
        gate_bias,
        w_pool[l].astype(BF16),
        pool_scale[l].reshape(1, W_POOL),
        g_head[l].reshape(1, W_MLSTM),
        w_out[l].astype(BF16),
        g_final.reshape(1, D_MODEL),
    )

    yp, pp, pc, pn, pm = _run_layer(x_prompt, mod[:nbp], None, weights, BB=1, TB=256, start=0)
    ys, sp, sc, sn, sm = _run_layer(
        x_sample, mod[nbp:], (state_pool[l], state_C[l], state_n[l], state_m[l]), weights,
        BB=4, TB=x_sample.shape[1], start=PAST_LEN)
    return (yp, ys, pp[None], pc[None], pn[None], pm[None], sp[None], sc[None], sn[None], sm[None])
```

```python
import functools

import jax
import jax.numpy as jnp
from jax import lax
from jax.experimental import pallas as pl
from jax.experimental.pallas import tpu as pltpu

F32 = jnp.float32
BF16 = jnp.bfloat16

D_MODEL = 1024
W_POOL = 1024
W_MLSTM = 1024
POOL_WINDOWS = (2, 4, 8, 16)
N_POOL_GROUPS = 4
POOL_GW = W_POOL // N_POOL_GROUPS
POOL_BUF = 15
N_HEADS = 4
HEAD_DIM = W_MLSTM // N_HEADS
EPS = 1e-6
N_MAIN_SECTIONS = 7
LANES = 128
PAST_LEN = 2048
HIST = 16
VMEM_LIMIT_BYTES = 56 * 1024 * 1024


def _silu(z):
    return z * jax.nn.sigmoid(z)


def _log_sigmoid(z):
    return jnp.minimum(z, 0.0) - jnp.log1p(jnp.exp(-jnp.abs(z)))


def _mod_kernel(c_ref, w_ref, b_ref, o_ref):
    c = c_ref[...]
    o_ref[...] = jnp.dot(_silu(c), w_ref[...], preferred_element_type=F32) + b_ref[...]


def _adaln_mod(c, w_ada, b_ada):
    nb = c.shape[0]
    n_out = w_ada.shape[1]
    blk = D_MODEL
    return pl.pallas_call(
        _mod_kernel,
        grid=(n_out // blk,),
        in_specs=[
            pl.BlockSpec((nb, D_MODEL), lambda j: (0, 0)),
            pl.BlockSpec((D_MODEL, blk), lambda j: (0, j)),
            pl.BlockSpec((1, blk), lambda j: (0, j)),
        ],
        out_specs=pl.BlockSpec((nb, blk), lambda j: (0, j)),
        out_shape=jax.ShapeDtypeStruct((nb, n_out), F32),
        name="adaln_mod",
    )(c, w_ada, b_ada.reshape(1, n_out))


def _seg_scan(x, tl, seg, combine, fill):
    k = 1
    while k < seg:
        shifted = pltpu.roll(x, k, 0)
        x = combine(x, jnp.where(tl >= k, shifted, fill))
        k *= 2
    return x


def _rows_of(per_batch_rows, TB):
    parts = [jnp.broadcast_to(r, (TB, LANES)) for r in per_batch_rows]
    return parts[0] if len(parts) == 1 else jnp.concatenate(parts, axis=0)


def _layer_kernel(*refs, BB, TB, start, has_state):
    R = BB * TB
    D, W, DH, GW = D_MODEL, W_POOL, HEAD_DIM, POOL_GW
    it = iter(refs)
    x_ref = next(it)
    mod_ref = next(it)
    if has_state:
        pool0_ref, C0_ref, n0_ref, m0_ref = next(it), next(it), next(it), next(it)
    gnorm_ref, win_ref, wg_ref, gbias_ref = next(it), next(it), next(it), next(it)
    wpool_ref, pscale_ref, ghead_ref, wout_ref, gfinal_ref = (next(it), next(it), next(it),
                                                             next(it), next(it))
    y_ref, pool_out_ref, C_ref, n_ref, m_ref = next(it), next(it), next(it), next(it), next(it)
    h_s, ext_s, pooled_s, zp_s, hm_s, q_s, k_s, v_s, ycat_s = it

    t = pl.program_id(1)

    @pl.when(t == 0)
    def _init():
        if has_state:
            ext_s[:, 0:1, :] = jnp.zeros((BB, 1, W), F32)
            ext_s[:, 1:HIST, :] = pool0_ref[...]
            C_ref[...] = C0_ref[...]
            n_ref[...] = n0_ref[...]
            m_ref[...] = m0_ref[...]
        else:
            ext_s[:, 0:HIST, :] = jnp.zeros((BB, HIST, W), F32)
            C_ref[...] = jnp.zeros_like(C_ref)
            n_ref[...] = jnp.zeros_like(n_ref)
            m_ref[...] = jnp.zeros_like(m_ref)

    for bb in range(BB):
        x = x_ref[bb]
        r = lax.rsqrt(jnp.mean(x * x, axis=-1, keepdims=True) + EPS)
        shift = mod_ref[bb, :, 0:D]
        scale = mod_ref[bb, :, D:2 * D]
        h = (x * r) * gnorm_ref[...] * (1.0 + scale) + shift
        h_s[bb * TB:(bb + 1) * TB, :] = h.astype(BF16)

    def proj(sec):
        return jnp.dot(h_s[...], win_ref[:, sec * W:(sec + 1) * W], preferred_element_type=F32)

    gates = jnp.dot(h_s[...], wg_ref[...], preferred_element_type=F32) + gbias_ref[...]

    tl = lax.broadcasted_iota(jnp.int32, (R, LANES), 0) & (TB - 1)
    ig = gates
    lf = _log_sigmoid(pltpu.roll(gates, LANES - N_HEADS, 1))
    Fc = _seg_scan(lf, tl, TB, jnp.add, 0.0)
    a = ig - Fc
    cmax = _seg_scan(a, tl, TB, jnp.maximum, -jnp.inf)
    m0_rows = [m_ref[bb] for bb in range(BB)]
    m0 = _rows_of(m0_rows, TB)
    m = Fc + jnp.maximum(m0, cmax)
    Fl_rows = [Fc[(bb + 1) * TB - 1:(bb + 1) * TB, :] for bb in range(BB)]
    mL_rows = [m[(bb + 1) * TB - 1:(bb + 1) * TB, :] for bb in range(BB)]
    Fl = _rows_of(Fl_rows, TB)
    mL = _rows_of(mL_rows, TB)
    decay0 = jnp.exp(m0 + Fc - m)
    wL = jnp.exp(a + Fl - mL)
    emm = jnp.exp(-m)
    Fm = Fc - m
    aT = a.T

    xp = proj(0)
    for bb in range(BB):
        ext_s[bb, HIST:HIST + TB, :] = xp[bb * TB:(bb + 1) * TB, :]
    zp_s[...] = _silu(proj(1))
    q_s[...] = proj(2).astype(BF16)
    k_s[...] = (proj(3) * (DH ** -0.5)).astype(BF16)
    v_s[...] = proj(4).astype(BF16)

    pos_head = start + t * TB + lax.broadcasted_iota(jnp.int32, (HIST, 1), 0)
    for g, w in enumerate(POOL_WINDOWS):
        cols = slice(g * GW, (g + 1) * GW)
        inv_head = 1.0 / jnp.minimum(pos_head + 1, w).astype(F32)
        for bb in range(BB):
            ext = ext_s[bb, :, cols]
            win = ext
            s = 1
            while s < w:
                win = win + pltpu.roll(win, s, 0)
                s *= 2
            r0 = bb * TB
            head = win[HIST:2 * HIST, :] * inv_head - ext[HIST:2 * HIST, :]
            tail = win[2 * HIST:, :] * (1.0 / w) - ext[2 * HIST:, :]
            pooled_s[r0:r0 + HIST, cols] = head.astype(BF16)
            pooled_s[r0 + HIST:r0 + TB, cols] = tail.astype(BF16)
    for g in range(N_POOL_GROUPS):
        cols = slice(g * GW, (g + 1) * GW)
        mixed = jnp.dot(pooled_s[:, cols], wpool_ref[g], preferred_element_type=F32)
        ycat_s[:, cols] = (mixed * pscale_ref[:, cols] * zp_s[:, cols]).astype(BF16)

    @pl.when(t == pl.num_programs(1) - 1)
    def _emit_pool():
        pool_out_ref[...] = ext_s[:, TB + 1:TB + HIST, :]

    for bb in range(BB):
        ext_s[bb, 0:HIST, :] = ext_s[bb, TB:TB + HIST, :]

    causal = (lax.broadcasted_iota(jnp.int32, (TB, TB), 0)
              >= lax.broadcasted_iota(jnp.int32, (TB, TB), 1))

    for bb in range(BB):
        rows = slice(bb * TB, (bb + 1) * TB)
        dL = jnp.exp(m0_rows[bb] + Fl_rows[bb] - mL_rows[bb])
        for hd in range(N_HEADS):
            cols = slice(hd * DH, (hd + 1) * DH)
            q = q_s[rows, cols]
            k = k_s[rows, cols]
            v = v_s[rows, cols]
            s = lax.dot_general(q, k, (((1,), (1,)), ((), ())), preferred_element_type=F32)
            logD = Fm[rows, hd:hd + 1] + aT[hd:hd + 1, rows]
            S = s * jnp.exp(jnp.where(causal, logD, -jnp.inf))
            C0 = C_ref[bb, hd]
            n0 = n_ref[bb, hd:hd + 1, :]
            d0 = decay0[rows, hd:hd + 1]
            qn = jnp.sum(q.astype(F32) * n0, axis=-1, keepdims=True)
            nq = jnp.sum(S, axis=-1, keepdims=True) + d0 * qn
            num = (jnp.dot(S.astype(BF16), v, preferred_element_type=F32)
                   + d0 * jnp.dot(q, C0.astype(BF16), preferred_element_type=F32))
            den = jnp.maximum(jnp.abs(nq), emm[rows, hd:hd + 1])
            hh = num * (1.0 / den)
            hc = hh - jnp.mean(hh, axis=-1, keepdims=True)
            hn = hc * lax.rsqrt(jnp.mean(hc * hc, axis=-1, keepdims=True) + EPS)
            hm_s[rows, cols] = hn * ghead_ref[:, cols]
            kw = k.astype(F32) * wL[rows, hd:hd + 1]
            dl = dL[:, hd:hd + 1]
            C_ref[bb, hd] = dl * C0 + lax.dot_general(
                kw.astype(BF16), v, (((0,), (0,)), ((), ())), preferred_element_type=F32)
            n_ref[bb, hd:hd + 1, :] = dl * n0 + jnp.sum(kw, axis=0, keepdims=True)
        m_ref[bb] = mL_rows[bb]

    hm_s[...] = hm_s[...] * jax.nn.sigmoid(proj(5))
    ycat_s[:, W:2 * W] = (hm_s[...] * _silu(proj(6))).astype(BF16)

    y = jnp.dot(ycat_s[...], wout_ref[...], preferred_element_type=F32)
    for bb in range(BB):
        gate = mod_ref[bb, :, 2 * D:3 * D]
        xn = x_ref[bb] + gate * y[bb * TB:(bb + 1) * TB, :]
        r = lax.rsqrt(jnp.mean(xn * xn, axis=-1, keepdims=True) + EPS)
        y_ref[bb] = (xn * r) * gfinal_ref[...]


def _const_spec(shape):
    nd = len(shape)
    return pl.BlockSpec(shape, lambda b, t: (0,) * nd, pipeline_mode=pl.Buffered(1))


def _run_layer(x, mod, state, weights, *, BB, TB, start):
    B, T, D = x.shape
    W, H, DH = W_POOL, N_HEADS, HEAD_DIM
    R = BB * TB
    has_state = state is not None
    grid = (B // BB, T // TB)

    in_specs = [
        pl.BlockSpec((BB, TB, D), lambda b, t: (b, t, 0)),
        pl.BlockSpec((BB, 1, 3 * D), lambda b, t: (b, 0, 0)),
    ]
    args = [x, mod.reshape(B, 1, 3 * D)]
    if has_state:
        pool0, C0, n0, m0 = state
        in_specs += [
            pl.BlockSpec((BB, POOL_BUF, W), lambda b, t: (b, 0, 0)),
            pl.BlockSpec((BB, H, DH, DH), lambda b, t: (b, 0, 0, 0)),
            pl.BlockSpec((BB, H, DH), lambda b, t: (b, 0, 0)),
            pl.BlockSpec((BB, 1, LANES), lambda b, t: (b, 0, 0)),
        ]
        m0p = jnp.pad(m0, ((0, 0), (0, LANES - H))).reshape(B, 1, LANES)
        args += [pool0, C0, n0, m0p]
    in_specs += [_const_spec(w.shape) for w in weights]
    args += list(weights)

    out_shape = (
        jax.ShapeDtypeStruct((B, T, D), F32),
        jax.ShapeDtypeStruct((B, POOL_BUF, W), F32),
        jax.ShapeDtypeStruct((B, H, DH, DH), F32),
        jax.ShapeDtypeStruct((B, H, DH), F32),
        jax.ShapeDtypeStruct((B, 1, LANES), F32),
    )
    out_specs = (
        pl.BlockSpec((BB, TB, D), lambda b, t: (b, t, 0)),
        pl.BlockSpec((BB, POOL_BUF, W), lambda b, t: (b, 0, 0)),
        pl.BlockSpec((BB, H, DH, DH), lambda b, t: (b, 0, 0, 0)),
        pl.BlockSpec((BB, H, DH), lambda b, t: (b, 0, 0)),
        pl.BlockSpec((BB, 1, LANES), lambda b, t: (b, 0, 0)),
    )
    scratch = [
        pltpu.VMEM((R, D), BF16),
        pltpu.VMEM((BB, HIST + TB, W), F32),
        pltpu.VMEM((R, W), BF16),
        pltpu.VMEM((R, W), F32),
        pltpu.VMEM((R, W), F32),
        pltpu.VMEM((R, W), BF16),
        pltpu.VMEM((R, W), BF16),
        pltpu.VMEM((R, W), BF16),
        pltpu.VMEM((R, 2 * W), BF16),
    ]
    y, pool, C, n, m = pl.pallas_call(
        functools.partial(_layer_kernel, BB=BB, TB=TB, start=start, has_state=has_state),
        grid=grid,
        in_specs=in_specs,
        out_specs=out_specs,
        out_shape=out_shape,
        scratch_shapes=scratch,
        compiler_params=pltpu.CompilerParams(
            dimension_semantics=("parallel", "arbitrary"),
            vmem_limit_bytes=VMEM_LIMIT_BYTES),
        name="layer_state" if has_state else "layer_fresh",
    )(*args)
    return y, pool, C, n, m[:, 0, :H]


def kernel(x_prompt, x_sample, c_prompt, c_sample, state_pool, state_C, state_n, state_m,
           w_ada, b_ada, g_norm, w_in, b_i, b_f, w_pool, pool_scale, g_head, w_out, g_final):
    depth = w_ada.shape[0]
    assert depth == 1, "single-layer trunk"
    l = 0
    nbp = x_prompt.shape[0]
    n_main = N_MAIN_SECTIONS * W_POOL

    mod = _adaln_mod(jnp.concatenate([c_prompt, c_sample], axis=0), w_ada[l], b_ada[l])

    w_gate = jnp.pad(w_in[l][:, n_main:], ((0, 0), (0, LANES - 2 * N_HEADS))).astype(BF16)
    gate_bias = jnp.pad(jnp.concatenate([b_i[l], b_f[l]]), (0, LANES - 2 * N_HEADS)).reshape(1, LANES)
    weights = (
        g_norm[l].reshape(1, D_MODEL),
        w_in[l][:, :n_main].astype(BF16),
        w_gate,
        gate_bias,
        w_pool[l].astype(BF16),
        pool_scale[l].reshape(1, W_POOL),
        g_head[l].reshape(1, W_MLSTM),
        w_out[l].astype(BF16),
        g_final.reshape(1, D_MODEL),
    )

    yp, pp, pc, pn, pm = _run_layer(x_prompt, mod[:nbp], None, weights, BB=1, TB=256, start=0)
    ys, sp, sc, sn, sm = _run_layer(
        x_sample, mod[nbp:], (state_pool[l], state_C[l], state_n[l], state_m[l]), weights,
        BB=4, TB=x_sample.shape[1], start=PAST_LEN)
    return (yp, ys, pp[None], pc[None], pn[None], pm[None], sp[None], sc[None], sn[None], sm[None])
```

```python
import functools

import jax
import jax.numpy as jnp
from jax import lax
from jax.experimental import pallas as pl
from jax.experimental.pallas import tpu as pltpu

F32 = jnp.float32
BF16 = jnp.bfloat16

D_MODEL = 1024
W_POOL = 1024
W_MLSTM = 1024
POOL_WINDOWS = (2, 4, 8, 16)
N_POOL_GROUPS = 4
POOL_GW = W_POOL // N_POOL_GROUPS
POOL_BUF = 15
N_HEADS = 4
HEAD_DIM = W_MLSTM // N_HEADS
EPS = 1e-6
N_MAIN_SECTIONS = 7
LANES = 128
PAST_LEN = 2048
HIST = 16
VMEM_LIMIT_BYTES = 56 * 1024 * 1024


def _sigmoid(z):
    return 0.5 * jnp.tanh(0.5 * z) + 0.5


def _silu(z):
    hz = 0.5 * z
    return hz * jnp.tanh(hz) + hz


def _log_sigmoid(z):
    return jnp.minimum(z, 0.0) - jnp.log1p(jnp.exp(-jnp.abs(z)))


def _mod_kernel(c_ref, w_ref, b_ref, o_ref):
    c = c_ref[...]
    o_ref[...] = jnp.dot(_silu(c), w_ref[...], preferred_element_type=F32) + b_ref[...]


def _adaln_mod(c, w_ada, b_ada):
    nb = c.shape[0]
    n_out = w_ada.shape[1]
    blk = D_MODEL
    return pl.pallas_call(
        _mod_kernel,
        grid=(n_out // blk,),
        in_specs=[
            pl.BlockSpec((nb, D_MODEL), lambda j: (0, 0)),
            pl.BlockSpec((D_MODEL, blk), lambda j: (0, j)),
            pl.BlockSpec((1, blk), lambda j: (0, j)),
        ],
        out_specs=pl.BlockSpec((nb, blk), lambda j: (0, j)),
        out_shape=jax.ShapeDtypeStruct((nb, n_out), F32),
        name="adaln_mod",
    )(c, w_ada, b_ada.reshape(1, n_out))


def _seg_scan(x, tl, seg, combine, fill):
    k = 1
    while k < seg:
        shifted = pltpu.roll(x, k, 0)
        x = combine(x, jnp.where(tl >= k, shifted, fill))
        k *= 2
    return x


def _rows_of(per_batch_rows, TB):
    parts = [jnp.broadcast_to(r, (TB, LANES)) for r in per_batch_rows]
    return parts[0] if len(parts) == 1 else jnp.concatenate(parts, axis=0)


def _layer_kernel(*refs, BB, TB, NT, start, has_state, pipelined):
    R = BB * TB
    D, W, DH, GW = D_MODEL, W_POOL, HEAD_DIM, POOL_GW
    it = iter(refs)
    xf_ref = next(it)
    xb_ref = next(it) if pipelined else xf_ref
    mod_ref = next(it)
    if has_state:
        pool0_ref, C0_ref, n0_ref, m0_ref = next(it), next(it), next(it), next(it)
    gnorm_ref, win_ref, wg_ref, gbias_ref = next(it), next(it), next(it), next(it)
    wpool_ref, pscale_ref, ghead_ref, wout_ref, gfinal_ref = (next(it), next(it), next(it),
                                                             next(it), next(it))
    y_ref, pool_out_ref, C_ref, n_ref, m_ref = next(it), next(it), next(it), next(it), next(it)
    (h_s, pooled_s, zp_s, q_s, k_s, v_s, gcol_s, arow_s, dl_s, ext_s, hm_s, ycat_s) = it

    t = pl.program_id(1)

    def init():
        if has_state:
            ext_s[:, 0:1, :] = jnp.zeros((BB, 1, W), F32)
            ext_s[:, 1:HIST, :] = pool0_ref[...]
            C_ref[...] = C0_ref[...]
            n_ref[...] = n0_ref[...]
            m_ref[...] = m0_ref[...]
        else:
            ext_s[:, 0:HIST, :] = jnp.zeros((BB, HIST, W), F32)
            C_ref[...] = jnp.zeros_like(C_ref)
            n_ref[...] = jnp.zeros_like(n_ref)
            m_ref[...] = jnp.zeros_like(m_ref)

    def proj(p, sec):
        return jnp.dot(h_s[p], win_ref[:, sec * W:(sec + 1) * W], preferred_element_type=F32)

    def front(p, tb):
        for bb in range(BB):
            x = xf_ref[bb]
            r = lax.rsqrt(jnp.mean(x * x, axis=-1, keepdims=True) + EPS)
            shift = mod_ref[bb, :, 0:D]
            scale = mod_ref[bb, :, D:2 * D]
            h = (x * r) * gnorm_ref[...] * (1.0 + scale) + shift
            h_s[p, bb * TB:(bb + 1) * TB, :] = h.astype(BF16)

        gates = jnp.dot(h_s[p], wg_ref[...], preferred_element_type=F32) + gbias_ref[...]
        tl = lax.broadcasted_iota(jnp.int32, (R, LANES), 0) & (TB - 1)
        ig = gates
        lf = _log_sigmoid(pltpu.roll(gates, LANES - N_HEADS, 1))
        Fc = _seg_scan(lf, tl, TB, jnp.add, 0.0)
        a = ig - Fc
        cmax = _seg_scan(a, tl, TB, jnp.maximum, -jnp.inf)
        m0_rows = [m_ref[bb] for bb in range(BB)]
        m0 = _rows_of(m0_rows, TB)
        m = Fc + jnp.maximum(m0, cmax)
        Fl_rows = [Fc[(bb + 1) * TB - 1:(bb + 1) * TB, :] for bb in range(BB)]
        mL_rows = [m[(bb + 1) * TB - 1:(bb + 1) * TB, :] for bb in range(BB)]
        Fl = _rows_of(Fl_rows, TB)
        mL = _rows_of(mL_rows, TB)
        gcol_s[p, 0] = Fc - m
        gcol_s[p, 1] = jnp.exp(m0 + Fc - m)
        gcol_s[p, 2] = jnp.exp(a + Fl - mL)
        gcol_s[p, 3] = jnp.exp(-m)
        arow_s[p] = a.T
        for bb in range(BB):
            dl_s[p, bb] = jnp.exp(m0_rows[bb] + Fl_rows[bb] - mL_rows[bb])
            m_ref[bb] = mL_rows[bb]
        yield

        xp = proj(p, 0)
        for bb in range(BB):
            ext_s[bb, HIST:HIST + TB, :] = xp[bb * TB:(bb + 1) * TB, :]
        zp_s[p] = _silu(proj(p, 1))
        q_s[p] = proj(p, 2).astype(BF16)
        k_s[p] = (proj(p, 3) * (DH ** -0.5)).astype(BF16)
        v_s[p] = proj(p, 4).astype(BF16)
        yield

        pos_head = start + tb * TB + lax.broadcasted_iota(jnp.int32, (HIST, 1), 0)
        for g, w in enumerate(POOL_WINDOWS):
            cols = slice(g * GW, (g + 1) * GW)
            inv_head = 1.0 / jnp.minimum(pos_head + 1, w).astype(F32)
            for bb in range(BB):
                ext = ext_s[bb, :, cols]
                win = ext
                s = 1
                while s < w:
                    win = win + pltpu.roll(win, s, 0)
                    s *= 2
                r0 = bb * TB
                head = win[HIST:2 * HIST, :] * inv_head - ext[HIST:2 * HIST, :]
                tail = win[2 * HIST:, :] * (1.0 / w) - ext[2 * HIST:, :]
                pooled_s[p, r0:r0 + HIST, cols] = head.astype(BF16)
                pooled_s[p, r0 + HIST:r0 + TB, cols] = tail.astype(BF16)
        pool_out_ref[...] = ext_s[:, TB + 1:TB + HIST, :]
        for bb in range(BB):
            ext_s[bb, 0:HIST, :] = ext_s[bb, TB:TB + HIST, :]

    def back(p):
        for g in range(N_POOL_GROUPS):
            cols = slice(g * GW, (g + 1) * GW)
            mixed = jnp.dot(pooled_s[p, :, cols], wpool_ref[g], preferred_element_type=F32)
            ycat_s[:, cols] = (mixed * pscale_ref[:, cols] * zp_s[p, :, cols]).astype(BF16)

        Fm, decay0, wL, emm = gcol_s[p, 0], gcol_s[p, 1], gcol_s[p, 2], gcol_s[p, 3]
        aT = arow_s[p]
        causal = (lax.broadcasted_iota(jnp.int32, (TB, TB), 0)
                  >= lax.broadcasted_iota(jnp.int32, (TB, TB), 1))
        for bb in range(BB):
            rows = slice(bb * TB, (bb + 1) * TB)
            dL = dl_s[p, bb]
            for hd in range(N_HEADS):
                cols = slice(hd * DH, (hd + 1) * DH)
                q = q_s[p, rows, cols]
                k = k_s[p, rows, cols]
                v = v_s[p, rows, cols]
                s = lax.dot_general(q, k, (((1,), (1,)), ((), ())), preferred_element_type=F32)
                logD = Fm[rows, hd:hd + 1] + aT[hd:hd + 1, rows]
                S = s * jnp.exp(jnp.where(causal, logD, -jnp.inf))
                C0 = C_ref[bb, hd]
                n0 = n_ref[bb, hd:hd + 1, :]
                d0 = decay0[rows, hd:hd + 1]
                qn = jnp.sum(q.astype(F32) * n0, axis=-1, keepdims=True)
                nq = jnp.sum(S, axis=-1, keepdims=True) + d0 * qn
                num = (jnp.dot(S.astype(BF16), v, preferred_element_type=F32)
                       + d0 * jnp.dot(q, C0.astype(BF16), preferred_element_type=F32))
                den = jnp.maximum(jnp.abs(nq), emm[rows, hd:hd + 1])
                hh = num * (1.0 / den)
                hc = hh - jnp.mean(hh, axis=-1, keepdims=True)
                hn = hc * lax.rsqrt(jnp.mean(hc * hc, axis=-1, keepdims=True) + EPS)
                hm_s[rows, cols] = hn * ghead_ref[:, cols]
                kw = k.astype(F32) * wL[rows, hd:hd + 1]
                dl = dL[:, hd:hd + 1]
                C_ref[bb, hd] = dl * C0 + lax.dot_general(
                    kw.astype(BF16), v, (((0,), (0,)), ((), ())), preferred_element_type=F32)
                n_ref[bb, hd:hd + 1, :] = dl * n0 + jnp.sum(kw, axis=0, keepdims=True)
        yield

        hm_s[...] = hm_s[...] * _sigmoid(proj(p, 5))
        ycat_s[:, W:2 * W] = (hm_s[...] * _silu(proj(p, 6))).astype(BF16)
        yield

        y = jnp.dot(ycat_s[...], wout_ref[...], preferred_element_type=F32)
        for bb in range(BB):
            gate = mod_ref[bb, :, 2 * D:3 * D]
            xn = xb_ref[bb] + gate * y[bb * TB:(bb + 1) * TB, :]
            r = lax.rsqrt(jnp.mean(xn * xn, axis=-1, keepdims=True) + EPS)
            y_ref[bb] = (xn * r) * gfinal_ref[...]

    def run(*stages):
        stages = list(stages)
        while stages:
            for st in list(stages):
                if next(st, "done") == "done":
                    stages.remove(st)

    if not pipelined:
        pl.when(t == 0)(init)
        run(front(0, t))
        run(back(0))
        return

    @pl.when(t == 0)
    def _first():
        init()
        run(front(0, t))

    for par in (0, 1):
        @pl.when((t > 0) & (t < NT) & (t % 2 == par))
        def _steady(par=par):
            run(back(1 - par))
            run(front(par, t))

    @pl.when(t == NT)
    def _last():
        run(back((NT - 1) % 2))


def _const_spec(shape):
    nd = len(shape)
    return pl.BlockSpec(shape, lambda b, t: (0,) * nd, pipeline_mode=pl.Buffered(1))


def _run_layer(x, mod, state, weights, *, BB, TB, start, pipelined):
    B, T, D = x.shape
    W, H, DH = W_POOL, N_HEADS, HEAD_DIM
    R = BB * TB
    NT = T // TB
    has_state = state is not None
    nsets = 2 if pipelined else 1

    if pipelined:
        grid = (B // BB, NT + 1)
        front_map = lambda b, t: (b, jnp.minimum(t, NT - 1), 0)
        back_map = lambda b, t: (b, jnp.maximum(t - 1, 0), 0)
        in_specs = [pl.BlockSpec((BB, TB, D), front_map), pl.BlockSpec((BB, TB, D), back_map)]
        args = [x, x]
    else:
        grid = (B // BB, NT)
        back_map = lambda b, t: (b, t, 0)
        in_specs = [pl.BlockSpec((BB, TB, D), back_map)]
        args = [x]
    in_specs.append(pl.BlockSpec((BB, 1, 3 * D), lambda b, t: (b, 0, 0)))
    args.append(mod.reshape(B, 1, 3 * D))
    if has_state:
        pool0, C0, n0, m0 = state
        in_specs += [
            pl.BlockSpec((BB, POOL_BUF, W), lambda b, t: (b, 0, 0)),
            pl.BlockSpec((BB, H, DH, DH), lambda b, t: (b, 0, 0, 0)),
            pl.BlockSpec((BB, H, DH), lambda b, t: (b, 0, 0)),
            pl.BlockSpec((BB, 1, LANES), lambda b, t: (b, 0, 0)),
        ]
        m0p = jnp.pad(m0, ((0, 0), (0, LANES - H))).reshape(B, 1, LANES)
        args += [pool0, C0, n0, m0p]
    in_specs += [_const_spec(w.shape) for w in weights]
    args += list(weights)

    out_shape = (
        jax.ShapeDtypeStruct((B, T, D), F32),
        jax.ShapeDtypeStruct((B, POOL_BUF, W), F32),
        jax.ShapeDtypeStruct((B, H, DH, DH), F32),
        jax.ShapeDtypeStruct((B, H, DH), F32),
        jax.ShapeDtypeStruct((B, 1, LANES), F32),
    )
    out_specs = (
        pl.BlockSpec((BB, TB, D), back_map),
        pl.BlockSpec((BB, POOL_BUF, W), lambda b, t: (b, 0, 0)),
        pl.BlockSpec((BB, H, DH, DH), lambda b, t: (b, 0, 0, 0)),
        pl.BlockSpec((BB, H, DH), lambda b, t: (b, 0, 0)),
        pl.BlockSpec((BB, 1, LANES), lambda b, t: (b, 0, 0)),
    )
    scratch = [
        pltpu.VMEM((nsets, R, D), BF16),
        pltpu.VMEM((nsets, R, W), BF16),
        pltpu.VMEM((nsets, R, W), F32),
        pltpu.VMEM((nsets, R, W), BF16),
        pltpu.VMEM((nsets, R, W), BF16),
        pltpu.VMEM((nsets, R, W), BF16),
        pltpu.VMEM((nsets, 4, R, LANES), F32),
        pltpu.VMEM((nsets, LANES, R), F32),
        pltpu.VMEM((nsets, BB, 1, LANES), F32),
        pltpu.VMEM((BB, HIST + TB, W), F32),
        pltpu.VMEM((R, W), F32),
        pltpu.VMEM((R, 2 * W), BF16),
    ]
    y, pool, C, n, m = pl.pallas_call(
        functools.partial(_layer_kernel, BB=BB, TB=TB, NT=NT, start=start,
                          has_state=has_state, pipelined=pipelined),
        grid=grid,
        in_specs=in_specs,
        out_specs=out_specs,
        out_shape=out_shape,
        scratch_shapes=scratch,
        compiler_params=pltpu.CompilerParams(
            dimension_semantics=("parallel", "arbitrary"),
            vmem_limit_bytes=VMEM_LIMIT_BYTES),
        name="layer_state" if has_state else "layer_fresh",
    )(*args)
    return y, pool, C, n, m[:, 0, :H]


def kernel(x_prompt, x_sample, c_prompt, c_sample, state_pool, state_C, state_n, state_m,
           w_ada, b_ada, g_norm, w_in, b_i, b_f, w_pool, pool_scale, g_head, w_out, g_final):
    depth = w_ada.shape[0]
    assert depth == 1, "single-layer trunk"
    l = 0
    nbp = x_prompt.shape[0]
    n_main = N_MAIN_SECTIONS * W_POOL

    mod = _adaln_mod(jnp.concatenate([c_prompt, c_sample], axis=0), w_ada[l], b_ada[l])

    w_gate = jnp.pad(w_in[l][:, n_main:], ((0, 0), (0, LANES - 2 * N_HEADS))).astype(BF16)
    gate_bias = jnp.pad(jnp.concatenate([b_i[l], b_f[l]]), (0, LANES - 2 * N_HEADS)).reshape(1, LANES)
    weights = (
        g_norm[l].reshape(1, D_MODEL),
        w_in[l][:, :n_main].astype(BF16),
        w_gate,
        gate_bias,
        w_pool[l].astype(BF16),
        pool_scale[l].reshape(1, W_POOL),
        g_head[l].reshape(1, W_MLSTM),
        w_out[l].astype(BF16),
        g_final.reshape(1, D_MODEL),
    )

    yp, pp, pc, pn, pm = _run_layer(x_prompt, mod[:nbp], None, weights, BB=1, TB=256, start=0,
                                    pipelined=True)
    ys, sp, sc, sn, sm = _run_layer(
        x_sample, mod[nbp:], (state_pool[l], state_C[l], state_n[l], state_m[l]), weights,
        BB=4, TB=x_sample.shape[1], start=PAST_LEN, pipelined=False)
    return (yp, ys, pp[None], pc[None], pn[None], pm[None], sp[None], sc[None], sn[None], sm[None])
```

```python
import functools

import jax
import jax.numpy as jnp
from jax import lax
from jax.experimental import pallas as pl
from jax.experimental.pallas import tpu as pltpu

F32 = jnp.float32
BF16 = jnp.bfloat16

D_MODEL = 1024
W_POOL = 1024
W_MLSTM = 1024
POOL_WINDOWS = (2, 4, 8, 16)
N_POOL_GROUPS = 4
POOL_GW = W_POOL // N_POOL_GROUPS
POOL_BUF = 15
N_HEADS = 4
HEAD_DIM = W_MLSTM // N_HEADS
EPS = 1e-6
N_MAIN_SECTIONS = 7
LANES = 128
PAST_LEN = 2048
HIST = 16
VMEM_LIMIT_BYTES = 56 * 1024 * 1024


def _sigmoid(z):
    return 0.5 * jnp.tanh(0.5 * z) + 0.5


def _silu(z):
    hz = 0.5 * z
    return hz * jnp.tanh(hz) + hz


def _log_sigmoid(z):
    return jnp.minimum(z, 0.0) - jnp.log1p(jnp.exp(-jnp.abs(z)))


def _mod_kernel(c_ref, w_ref, b_ref, o_ref):
    c = c_ref[...]
    o_ref[...] = jnp.dot(_silu(c), w_ref[...], preferred_element_type=F32) + b_ref[...]


def _adaln_mod(c, w_ada, b_ada):
    nb = c.shape[0]
    n_out = w_ada.shape[1]
    blk = D_MODEL
    return pl.pallas_call(
        _mod_kernel,
        grid=(n_out // blk,),
        in_specs=[
            pl.BlockSpec((nb, D_MODEL), lambda j: (0, 0)),
            pl.BlockSpec((D_MODEL, blk), lambda j: (0, j)),
            pl.BlockSpec((1, blk), lambda j: (0, j)),
        ],
        out_specs=pl.BlockSpec((nb, blk), lambda j: (0, j)),
        out_shape=jax.ShapeDtypeStruct((nb, n_out), F32),
        name="adaln_mod",
    )(c, w_ada, b_ada.reshape(1, n_out))


def _seg_scan(x, tl, seg, combine, fill):
    k = 1
    while k < seg:
        shifted = pltpu.roll(x, k, 0)
        x = combine(x, jnp.where(tl >= k, shifted, fill))
        k *= 2
    return x


def _rows_of(per_batch_rows, TB):
    parts = [jnp.broadcast_to(r, (TB, LANES)) for r in per_batch_rows]
    return parts[0] if len(parts) == 1 else jnp.concatenate(parts, axis=0)


def _layer_kernel(*refs, BB, TB, NT, start, has_state, pipelined):
    R = BB * TB
    D, W, DH, GW = D_MODEL, W_POOL, HEAD_DIM, POOL_GW
    it = iter(refs)
    xf_ref = next(it)
    mod_ref = next(it)
    if has_state:
        pool0_ref, C0_ref, n0_ref, m0_ref = next(it), next(it), next(it), next(it)
    gnorm_ref, win_ref, wg_ref, gbias_ref = next(it), next(it), next(it), next(it)
    wpool_ref, pscale_ref, ghead_ref, wout_ref, gfinal_ref = (next(it), next(it), next(it),
                                                             next(it), next(it))
    y_ref, pool_out_ref, C_ref, n_ref, m_ref = next(it), next(it), next(it), next(it), next(it)
    (h_s, xres_s, pooled_s, zp_s, q_s, k_s, v_s, gcol_s, arow_s, dl_s, ext_s, hm_s, ycat_s) = it

    t = pl.program_id(1)

    def init():
        if has_state:
            ext_s[:, 0:1, :] = jnp.zeros((BB, 1, W), F32)
            ext_s[:, 1:HIST, :] = pool0_ref[...]
            C_ref[...] = C0_ref[...]
            n_ref[...] = n0_ref[...]
            m_ref[...] = m0_ref[...]
        else:
            ext_s[:, 0:HIST, :] = jnp.zeros((BB, HIST, W), F32)
            C_ref[...] = jnp.zeros_like(C_ref)
            n_ref[...] = jnp.zeros_like(n_ref)
            m_ref[...] = jnp.zeros_like(m_ref)

    def proj(p, sec):
        return jnp.dot(h_s[p], win_ref[:, sec * W:(sec + 1) * W], preferred_element_type=F32)

    def front(p, tb):
        for bb in range(BB):
            x = xf_ref[bb]
            xres_s[p, bb] = x
            r = lax.rsqrt(jnp.mean(x * x, axis=-1, keepdims=True) + EPS)
            shift = mod_ref[bb, :, 0:D]
            scale = mod_ref[bb, :, D:2 * D]
            h = (x * r) * gnorm_ref[...] * (1.0 + scale) + shift
            h_s[p, bb * TB:(bb + 1) * TB, :] = h.astype(BF16)

        gates = jnp.dot(h_s[p], wg_ref[...], preferred_element_type=F32) + gbias_ref[...]
        tl = lax.broadcasted_iota(jnp.int32, (R, LANES), 0) & (TB - 1)
        ig = gates
        lf = _log_sigmoid(pltpu.roll(gates, LANES - N_HEADS, 1))
        Fc = _seg_scan(lf, tl, TB, jnp.add, 0.0)
        a = ig - Fc
        cmax = _seg_scan(a, tl, TB, jnp.maximum, -jnp.inf)
        m0_rows = [m_ref[bb] for bb in range(BB)]
        m0 = _rows_of(m0_rows, TB)
        m = Fc + jnp.maximum(m0, cmax)
        Fl_rows = [Fc[(bb + 1) * TB - 1:(bb + 1) * TB, :] for bb in range(BB)]
        mL_rows = [m[(bb + 1) * TB - 1:(bb + 1) * TB, :] for bb in range(BB)]
        Fl = _rows_of(Fl_rows, TB)
        mL = _rows_of(mL_rows, TB)
        gcol_s[p, 0] = Fc - m
        gcol_s[p, 1] = jnp.exp(m0 + Fc - m)
        gcol_s[p, 2] = jnp.exp(a + Fl - mL)
        gcol_s[p, 3] = jnp.exp(-m)
        arow_s[p] = a.T
        for bb in range(BB):
            dl_s[p, bb] = jnp.exp(m0_rows[bb] + Fl_rows[bb] - mL_rows[bb])
            m_ref[bb] = mL_rows[bb]
        yield

        xp = proj(p, 0)
        for bb in range(BB):
            ext_s[bb, HIST:HIST + TB, :] = xp[bb * TB:(bb + 1) * TB, :]
        zp_s[p] = _silu(proj(p, 1))
        q_s[p] = proj(p, 2).astype(BF16)
        k_s[p] = (proj(p, 3) * (DH ** -0.5)).astype(BF16)
        v_s[p] = proj(p, 4).astype(BF16)
        yield

        pos_head = start + tb * TB + lax.broadcasted_iota(jnp.int32, (HIST, 1), 0)
        for g, w in enumerate(POOL_WINDOWS):
            cols = slice(g * GW, (g + 1) * GW)
            inv_head = 1.0 / jnp.minimum(pos_head + 1, w).astype(F32)
            for bb in range(BB):
                ext = ext_s[bb, :, cols]
                win = ext
                s = 1
                while s < w:
                    win = win + pltpu.roll(win, s, 0)
                    s *= 2
                r0 = bb * TB
                head = win[HIST:2 * HIST, :] * inv_head - ext[HIST:2 * HIST, :]
                tail = win[2 * HIST:, :] * (1.0 / w) - ext[2 * HIST:, :]
                pooled_s[p, r0:r0 + HIST, cols] = head.astype(BF16)
                pooled_s[p, r0 + HIST:r0 + TB, cols] = tail.astype(BF16)
        pool_out_ref[...] = ext_s[:, TB + 1:TB + HIST, :]
        for bb in range(BB):
            ext_s[bb, 0:HIST, :] = ext_s[bb, TB:TB + HIST, :]

    def back(p):
        for g in range(N_POOL_GROUPS):
            cols = slice(g * GW, (g + 1) * GW)
            mixed = jnp.dot(pooled_s[p, :, cols], wpool_ref[g], preferred_element_type=F32)
            ycat_s[:, cols] = (mixed * pscale_ref[:, cols] * zp_s[p, :, cols]).astype(BF16)

        Fm, decay0, wL, emm = gcol_s[p, 0], gcol_s[p, 1], gcol_s[p, 2], gcol_s[p, 3]
        aT = arow_s[p]
        causal = (lax.broadcasted_iota(jnp.int32, (TB, TB), 0)
                  >= lax.broadcasted_iota(jnp.int32, (TB, TB), 1))
        for bb in range(BB):
            rows = slice(bb * TB, (bb + 1) * TB)
            dL = dl_s[p, bb]
            for hd in range(N_HEADS):
                cols = slice(hd * DH, (hd + 1) * DH)
                q = q_s[p, rows, cols]
                k = k_s[p, rows, cols]
                v = v_s[p, rows, cols]
                s = lax.dot_general(q, k, (((1,), (1,)), ((), ())), preferred_element_type=F32)
                logD = Fm[rows, hd:hd + 1] + aT[hd:hd + 1, rows]
                S = s * jnp.exp(jnp.where(causal, logD, -jnp.inf))
                C0 = C_ref[bb, hd]
                n0 = n_ref[bb, hd:hd + 1, :]
                d0 = decay0[rows, hd:hd + 1]
                qn = jnp.sum(q.astype(F32) * n0, axis=-1, keepdims=True)
                nq = jnp.sum(S, axis=-1, keepdims=True) + d0 * qn
                num = (jnp.dot(S.astype(BF16), v, preferred_element_type=F32)
                       + d0 * jnp.dot(q, C0.astype(BF16), preferred_element_type=F32))
                den = jnp.maximum(jnp.abs(nq), emm[rows, hd:hd + 1])
                hh = num * (1.0 / den)
                hc = hh - jnp.mean(hh, axis=-1, keepdims=True)
                hn = hc * lax.rsqrt(jnp.mean(hc * hc, axis=-1, keepdims=True) + EPS)
                hm_s[rows, cols] = hn * ghead_ref[:, cols]
                kw = k.astype(F32) * wL[rows, hd:hd + 1]
                dl = dL[:, hd:hd + 1]
                C_ref[bb, hd] = dl * C0 + lax.dot_general(
                    kw.astype(BF16), v, (((0,), (0,)), ((), ())), preferred_element_type=F32)
                n_ref[bb, hd:hd + 1, :] = dl * n0 + jnp.sum(kw, axis=0, keepdims=True)
        yield

        hm_s[...] = hm_s[...] * _sigmoid(proj(p, 5))
        ycat_s[:, W:2 * W] = (hm_s[...] * _silu(proj(p, 6))).astype(BF16)
        yield

        y = jnp.dot(ycat_s[...], wout_ref[...], preferred_element_type=F32)
        for bb in range(BB):
            gate = mod_ref[bb, :, 2 * D:3 * D]
            xn = xres_s[p, bb] + gate * y[bb * TB:(bb + 1) * TB, :]
            r = lax.rsqrt(jnp.mean(xn * xn, axis=-1, keepdims=True) + EPS)
            y_ref[bb] = (xn * r) * gfinal_ref[...]

    def run(*stages):
        stages = list(stages)
        while stages:
            for st in list(stages):
                if next(st, "done") == "done":
                    stages.remove(st)

    if not pipelined:
        pl.when(t == 0)(init)
        run(front(0, t))
        run(back(0))
        return

    @pl.when(t == 0)
    def _first():
        init()
        run(front(0, t))

    for par in (0, 1):
        @pl.when((t > 0) & (t < NT) & (t % 2 == par))
        def _steady(par=par):
            run(back(1 - par))
            run(front(par, t))

    @pl.when(t == NT)
    def _last():
        run(back((NT - 1) % 2))


def _const_spec(shape):
    nd = len(shape)
    return pl.BlockSpec(shape, lambda b, t: (0,) * nd, pipeline_mode=pl.Buffered(1))


def _run_layer(x, mod, state, weights, *, BB, TB, start, pipelined):
    B, T, D = x.shape
    W, H, DH = W_POOL, N_HEADS, HEAD_DIM
    R = BB * TB
    NT = T // TB
    has_state = state is not None
    nsets = 2 if pipelined else 1

    if pipelined:
        grid = (B // BB, NT + 1)
        front_map = lambda b, t: (b, jnp.minimum(t, NT - 1), 0)
        back_map = lambda b, t: (b, jnp.maximum(t - 1, 0), 0)
        in_specs = [pl.BlockSpec((BB, TB, D), front_map)]
        args = [x]
    else:
        grid = (B // BB, NT)
        back_map = lambda b, t: (b, t, 0)
        in_specs = [pl.BlockSpec((BB, TB, D), back_map)]
        args = [x]
    in_specs.append(pl.BlockSpec((BB, 1, 3 * D), lambda b, t: (b, 0, 0)))
    args.append(mod.reshape(B, 1, 3 * D))
    if has_state:
        pool0, C0, n0, m0 = state
        in_specs += [
            pl.BlockSpec((BB, POOL_BUF, W), lambda b, t: (b, 0, 0)),
            pl.BlockSpec((BB, H, DH, DH), lambda b, t: (b, 0, 0, 0)),
            pl.BlockSpec((BB, H, DH), lambda b, t: (b, 0, 0)),
            pl.BlockSpec((BB, 1, LANES), lambda b, t: (b, 0, 0)),
        ]
        m0p = jnp.pad(m0, ((0, 0), (0, LANES - H))).reshape(B, 1, LANES)
        args += [pool0, C0, n0, m0p]
    in_specs += [_const_spec(w.shape) for w in weights]
    args += list(weights)

    out_shape = (
        jax.ShapeDtypeStruct((B, T, D), F32),
        jax.ShapeDtypeStruct((B, POOL_BUF, W), F32),
        jax.ShapeDtypeStruct((B, H, DH, DH), F32),
        jax.ShapeDtypeStruct((B, H, DH), F32),
        jax.ShapeDtypeStruct((B, 1, LANES), F32),
    )
    out_specs = (
        pl.BlockSpec((BB, TB, D), back_map),
        pl.BlockSpec((BB, POOL_BUF, W), lambda b, t: (b, 0, 0)),
        pl.BlockSpec((BB, H, DH, DH), lambda b, t: (b, 0, 0, 0)),
        pl.BlockSpec((BB, H, DH), lambda b, t: (b, 0, 0)),
        pl.BlockSpec((BB, 1, LANES), lambda b, t: (b, 0, 0)),
    )
    scratch = [
        pltpu.VMEM((nsets, R, D), BF16),
        pltpu.VMEM((nsets, BB, TB, D), F32),
        pltpu.VMEM((nsets, R, W), BF16),
        pltpu.VMEM((nsets, R, W), F32),
        pltpu.VMEM((nsets, R, W), BF16),
        pltpu.VMEM((nsets, R, W), BF16),
        pltpu.VMEM((nsets, R, W), BF16),
        pltpu.VMEM((nsets, 4, R, LANES), F32),
        pltpu.VMEM((nsets, LANES, R), F32),
        pltpu.VMEM((nsets, BB, 1, LANES), F32),
        pltpu.VMEM((BB, HIST + TB, W), F32),
        pltpu.VMEM((R, W), F32),
        pltpu.VMEM((R, 2 * W), BF16),
    ]
    y, pool, C, n, m = pl.pallas_call(
        functools.partial(_layer_kernel, BB=BB, TB=TB, NT=NT, start=start,
                          has_state=has_state, pipelined=pipelined),
        grid=grid,
        in_specs=in_specs,
        out_specs=out_specs,
        out_shape=out_shape,
        scratch_shapes=scratch,
        compiler_params=pltpu.CompilerParams(
            dimension_semantics=("parallel", "arbitrary"),
            vmem_limit_bytes=VMEM_LIMIT_BYTES),
        name="layer_state" if has_state else "layer_fresh",
    )(*args)
    return y, pool, C, n, m[:, 0, :H]


def kernel(x_prompt, x_sample, c_prompt, c_sample, state_pool, state_C, state_n, state_m,
           w_ada, b_ada, g_norm, w_in, b_i, b_f, w_pool, pool_scale, g_head, w_out, g_final):
    depth = w_ada.shape[0]
    assert depth == 1, "single-layer trunk"
    l = 0
    nbp = x_prompt.shape[0]
    n_main = N_MAIN_SECTIONS * W_POOL

    mod = _adaln_mod(jnp.concatenate([c_prompt, c_sample], axis=0), w_ada[l], b_ada[l])

    w_gate = jnp.pad(w_in[l][:, n_main:], ((0, 0), (0, LANES - 2 * N_HEADS))).astype(BF16)
    gate_bias = jnp.pad(jnp.concatenate([b_i[l], b_f[l]]), (0, LANES - 2 * N_HEADS)).reshape(1, LANES)
    weights = (
        g_norm[l].reshape(1, D_MODEL),
        w_in[l][:, :n_main].astype(BF16),
        w_gate,
        gate_bias,
        w_pool[l].astype(BF16),
        pool_scale[l].reshape(1, W_POOL),
        g_head[l].reshape(1, W_MLSTM),
        w_out[l].astype(BF16),
        g_final.reshape(1, D_MODEL),
    )

    yp, pp, pc, pn, pm = _run_layer(x_prompt, mod[:nbp], None, weights, BB=1, TB=256, start=0,
                                    pipelined=True)
    ys, sp, sc, sn, sm = _run_layer(
        x_sample, mod[nbp:], (state_pool[l], state_C[l], state_n[l], state_m[l]), weights,
        BB=4, TB=x_sample.shape[1], start=PAST_LEN, pipelined=False)
    return (yp, ys, pp[None], pc[None], pn[None], pm[None], sp[None], sc[None], sn[None], sm[None])
```

```python
import functools

import jax
import jax.numpy as jnp
from jax import lax
from jax.experimental import pallas as pl
from jax.experimental.pallas import tpu as pltpu

F32 = jnp.float32
BF16 = jnp.bfloat16

D_MODEL = 1024
W_POOL = 1024
W_MLSTM = 1024
POOL_WINDOWS = (2, 4, 8, 16)
N_POOL_GROUPS = 4
POOL_GW = W_POOL // N_POOL_GROUPS
POOL_BUF = 15
N_HEADS = 4
HEAD_DIM = W_MLSTM // N_HEADS
EPS = 1e-6
N_MAIN_SECTIONS = 7
LANES = 128
PAST_LEN = 2048
HIST = 16
VMEM_LIMIT_BYTES = 56 * 1024 * 1024


def _sigmoid(z):
    return 0.5 * jnp.tanh(0.5 * z) + 0.5


def _silu(z):
    hz = 0.5 * z
    return hz * jnp.tanh(hz) + hz


def _log_sigmoid(z):
    return jnp.minimum(z, 0.0) - jnp.log1p(jnp.exp(-jnp.abs(z)))


def _mod_kernel(c_ref, w_ref, b_ref, o_ref):
    c = c_ref[...]
    mod = jnp.dot(_silu(c), w_ref[...], preferred_element_type=F32) + b_ref[...]
    for b in range(mod.shape[0]):
        o_ref[b] = mod[b:b + 1, :]


def _adaln_mod(c, w_ada, b_ada):
    nb = c.shape[0]
    n_out = w_ada.shape[1]
    blk = D_MODEL
    return pl.pallas_call(
        _mod_kernel,
        grid=(n_out // blk,),
        in_specs=[
            pl.BlockSpec((nb, D_MODEL), lambda j: (0, 0)),
            pl.BlockSpec((D_MODEL, blk), lambda j: (0, j)),
            pl.BlockSpec((1, blk), lambda j: (0, j)),
        ],
        out_specs=pl.BlockSpec((nb, 1, blk), lambda j: (0, 0, j)),
        out_shape=jax.ShapeDtypeStruct((nb, 1, n_out), F32),
        name="adaln_mod",
    )(c, w_ada, b_ada.reshape(1, n_out))


def _cast_kernel(i_ref, o_ref):
    o_ref[...] = i_ref[...].astype(o_ref.dtype)


def _cast_columns_bf16(w, n_cols, blk):
    k = w.shape[0]
    return pl.pallas_call(
        _cast_kernel,
        grid=(n_cols // blk,),
        in_specs=[pl.BlockSpec((k, blk), lambda j: (0, j))],
        out_specs=pl.BlockSpec((k, blk), lambda j: (0, j)),
        out_shape=jax.ShapeDtypeStruct((k, n_cols), BF16),
        name="cast_bf16",
    )(w)


def _seg_scan(x, tl, seg, combine, fill):
    k = 1
    while k < seg:
        shifted = pltpu.roll(x, k, 0)
        x = combine(x, jnp.where(tl >= k, shifted, fill))
        k *= 2
    return x


def _rows_of(per_batch_rows, TB):
    parts = [jnp.broadcast_to(r, (TB, LANES)) for r in per_batch_rows]
    return parts[0] if len(parts) == 1 else jnp.concatenate(parts, axis=0)


def _layer_kernel(*refs, BB, TB, start, has_state):
    R = BB * TB
    D, W, DH, GW = D_MODEL, W_POOL, HEAD_DIM, POOL_GW
    it = iter(refs)
    x_ref = next(it)
    mod_ref = next(it)
    if has_state:
        pool0_ref, C0_ref, n0_ref, m0_ref = next(it), next(it), next(it), next(it)
    gnorm_ref, win_ref, wg_ref, gbias_ref = next(it), next(it), next(it), next(it)
    wpool_ref, pscale_ref, ghead_ref, wout_ref, gfinal_ref = (next(it), next(it), next(it),
                                                             next(it), next(it))
    y_ref, pool_out_ref, C_ref, n_ref, m_ref = next(it), next(it), next(it), next(it), next(it)
    h_s, ext_s, pooled_s, zp_s, hm_s, q_s, k_s, v_s, ycat_s = it

    t = pl.program_id(1)

    @pl.when(t == 0)
    def _init():
        if has_state:
            ext_s[:, 0:1, :] = jnp.zeros((BB, 1, W), F32)
            ext_s[:, 1:HIST, :] = pool0_ref[...]
            C_ref[...] = C0_ref[...]
            n_ref[...] = n0_ref[...]
            m_ref[...] = m0_ref[...]
        else:
            ext_s[:, 0:HIST, :] = jnp.zeros((BB, HIST, W), F32)
            C_ref[...] = jnp.zeros_like(C_ref)
            n_ref[...] = jnp.zeros_like(n_ref)
            m_ref[...] = jnp.zeros_like(m_ref)

    for bb in range(BB):
        x = x_ref[bb]
        r = lax.rsqrt(jnp.mean(x * x, axis=-1, keepdims=True) + EPS)
        shift = mod_ref[bb, :, 0:D]
        scale = mod_ref[bb, :, D:2 * D]
        h = (x * r) * gnorm_ref[...] * (1.0 + scale) + shift
        h_s[bb * TB:(bb + 1) * TB, :] = h.astype(BF16)

    def proj(sec):
        return jnp.dot(h_s[...], win_ref[:, sec * W:(sec + 1) * W], preferred_element_type=F32)

    gates = jnp.dot(h_s[...], wg_ref[...], preferred_element_type=F32) + gbias_ref[...]

    tl = lax.broadcasted_iota(jnp.int32, (R, LANES), 0) & (TB - 1)
    ig = gates
    lf = _log_sigmoid(pltpu.roll(gates, LANES - N_HEADS, 1))
    Fc = _seg_scan(lf, tl, TB, jnp.add, 0.0)
    a = ig - Fc
    cmax = _seg_scan(a, tl, TB, jnp.maximum, -jnp.inf)
    m0_rows = [m_ref[bb] for bb in range(BB)]
    m0 = _rows_of(m0_rows, TB)
    m = Fc + jnp.maximum(m0, cmax)
    Fl_rows = [Fc[(bb + 1) * TB - 1:(bb + 1) * TB, :] for bb in range(BB)]
    mL_rows = [m[(bb + 1) * TB - 1:(bb + 1) * TB, :] for bb in range(BB)]
    Fl = _rows_of(Fl_rows, TB)
    mL = _rows_of(mL_rows, TB)
    decay0 = jnp.exp(m0 + Fc - m)
    wL = jnp.exp(a + Fl - mL)
    emm = jnp.exp(-m)
    Fm = Fc - m
    aT = a.T

    xp = proj(0)
    for bb in range(BB):
        ext_s[bb, HIST:HIST + TB, :] = xp[bb * TB:(bb + 1) * TB, :]
    zp_s[...] = _silu(proj(1))
    q_s[...] = proj(2).astype(BF16)
    k_s[...] = (proj(3) * (DH ** -0.5)).astype(BF16)
    v_s[...] = proj(4).astype(BF16)

    pos_head = start + t * TB + lax.broadcasted_iota(jnp.int32, (HIST, 1), 0)
    for g, w in enumerate(POOL_WINDOWS):
        cols = slice(g * GW, (g + 1) * GW)
        inv_head = 1.0 / jnp.minimum(pos_head + 1, w).astype(F32)
        for bb in range(BB):
            ext = ext_s[bb, :, cols]
            win = ext
            s = 1
            while s < w:
                win = win + pltpu.roll(win, s, 0)
                s *= 2
            r0 = bb * TB
            head = win[HIST:2 * HIST, :] * inv_head - ext[HIST:2 * HIST, :]
            tail = win[2 * HIST:, :] * (1.0 / w) - ext[2 * HIST:, :]
            pooled_s[r0:r0 + HIST, cols] = head.astype(BF16)
            pooled_s[r0 + HIST:r0 + TB, cols] = tail.astype(BF16)
    for g in range(N_POOL_GROUPS):
        cols = slice(g * GW, (g + 1) * GW)
        mixed = jnp.dot(pooled_s[:, cols], wpool_ref[g], preferred_element_type=F32)
        ycat_s[:, cols] = (mixed * pscale_ref[:, cols] * zp_s[:, cols]).astype(BF16)

    pool_out_ref[...] = ext_s[:, TB + 1:TB + HIST, :]

    for bb in range(BB):
        ext_s[bb, 0:HIST, :] = ext_s[bb, TB:TB + HIST, :]

    causal = (lax.broadcasted_iota(jnp.int32, (TB, TB), 0)
              >= lax.broadcasted_iota(jnp.int32, (TB, TB), 1))

    for bb in range(BB):
        rows = slice(bb * TB, (bb + 1) * TB)
        dL = jnp.exp(m0_rows[bb] + Fl_rows[bb] - mL_rows[bb])
        for hd in range(N_HEADS):
            cols = slice(hd * DH, (hd + 1) * DH)
            q = q_s[rows, cols]
            k = k_s[rows, cols]
            v = v_s[rows, cols]
            s = lax.dot_general(q, k, (((1,), (1,)), ((), ())), preferred_element_type=F32)
            logD = Fm[rows, hd:hd + 1] + aT[hd:hd + 1, rows]
            S = s * jnp.exp(jnp.where(causal, logD, -jnp.inf))
            C0 = C_ref[bb, hd]
            n0 = n_ref[bb, hd:hd + 1, :]
            d0 = decay0[rows, hd:hd + 1]
            qn = jnp.sum(q.astype(F32) * n0, axis=-1, keepdims=True)
            nq = jnp.sum(S, axis=-1, keepdims=True) + d0 * qn
            num = (jnp.dot(S.astype(BF16), v, preferred_element_type=F32)
                   + d0 * jnp.dot(q, C0.astype(BF16), preferred_element_type=F32))
            den = jnp.maximum(jnp.abs(nq), emm[rows, hd:hd + 1])
            hh = num * (1.0 / den)
            hc = hh - jnp.mean(hh, axis=-1, keepdims=True)
            hn = hc * lax.rsqrt(jnp.mean(hc * hc, axis=-1, keepdims=True) + EPS)
            hm_s[rows, cols] = hn * ghead_ref[:, cols]
            kw = k.astype(F32) * wL[rows, hd:hd + 1]
            dl = dL[:, hd:hd + 1]
            C_ref[bb, hd] = dl * C0 + lax.dot_general(
                kw.astype(BF16), v, (((0,), (0,)), ((), ())), preferred_element_type=F32)
            n_ref[bb, hd:hd + 1, :] = dl * n0 + jnp.sum(kw, axis=0, keepdims=True)
        m_ref[bb] = mL_rows[bb]

    hm_s[...] = hm_s[...] * _sigmoid(proj(5))
    ycat_s[:, W:2 * W] = (hm_s[...] * _silu(proj(6))).astype(BF16)

    y = jnp.dot(ycat_s[...], wout_ref[...], preferred_element_type=F32)
    for bb in range(BB):
        gate = mod_ref[bb, :, 2 * D:3 * D]
        xn = x_ref[bb] + gate * y[bb * TB:(bb + 1) * TB, :]
        r = lax.rsqrt(jnp.mean(xn * xn, axis=-1, keepdims=True) + EPS)
        y_ref[bb] = (xn * r) * gfinal_ref[...]


def _const_spec(shape):
    nd = len(shape)
    return pl.BlockSpec(shape, lambda b, t: (0,) * nd, pipeline_mode=pl.Buffered(1))


def _run_layer(x, mod, mod_row0, state, weights, *, BB, TB, start):
    B, T, D = x.shape
    W, H, DH = W_POOL, N_HEADS, HEAD_DIM
    R = BB * TB
    has_state = state is not None
    grid = (B // BB, T // TB)
    assert mod_row0 % BB == 0
    mod_blk0 = mod_row0 // BB

    in_specs = [
        pl.BlockSpec((BB, TB, D), lambda b, t: (b, t, 0)),
        pl.BlockSpec((BB, 1, 3 * D), lambda b, t: (b + mod_blk0, 0, 0)),
    ]
    args = [x, mod]
    if has_state:
        pool0, C0, n0, m0 = state
        in_specs += [
            pl.BlockSpec((BB, POOL_BUF, W), lambda b, t: (b, 0, 0)),
            pl.BlockSpec((BB, H, DH, DH), lambda b, t: (b, 0, 0, 0)),
            pl.BlockSpec((BB, H, DH), lambda b, t: (b, 0, 0)),
            pl.BlockSpec((BB, 1, LANES), lambda b, t: (b, 0, 0)),
        ]
        m0p = jnp.pad(m0, ((0, 0), (0, LANES - H))).reshape(B, 1, LANES)
        args += [pool0, C0, n0, m0p]
    in_specs += [_const_spec(w.shape) for w in weights]
    args += list(weights)

    out_shape = (
        jax.ShapeDtypeStruct((B, T, D), F32),
        jax.ShapeDtypeStruct((B, POOL_BUF, W), F32),
        jax.ShapeDtypeStruct((B, H, DH, DH), F32),
        jax.ShapeDtypeStruct((B, H, DH), F32),
        jax.ShapeDtypeStruct((B, 1, LANES), F32),
    )
    out_specs = (
        pl.BlockSpec((BB, TB, D), lambda b, t: (b, t, 0)),
        pl.BlockSpec((BB, POOL_BUF, W), lambda b, t: (b, 0, 0)),
        pl.BlockSpec((BB, H, DH, DH), lambda b, t: (b, 0, 0, 0)),
        pl.BlockSpec((BB, H, DH), lambda b, t: (b, 0, 0)),
        pl.BlockSpec((BB, 1, LANES), lambda b, t: (b, 0, 0)),
    )
    scratch = [
        pltpu.VMEM((R, D), BF16),
        pltpu.VMEM((BB, HIST + TB, W), F32),
        pltpu.VMEM((R, W), BF16),
        pltpu.VMEM((R, W), F32),
        pltpu.VMEM((R, W), F32),
        pltpu.VMEM((R, W), BF16),
        pltpu.VMEM((R, W), BF16),
        pltpu.VMEM((R, W), BF16),
        pltpu.VMEM((R, 2 * W), BF16),
    ]
    y, pool, C, n, m = pl.pallas_call(
        functools.partial(_layer_kernel, BB=BB, TB=TB, start=start, has_state=has_state),
        grid=grid,
        in_specs=in_specs,
        out_specs=out_specs,
        out_shape=out_shape,
        scratch_shapes=scratch,
        compiler_params=pltpu.CompilerParams(
            dimension_semantics=("parallel", "arbitrary"),
            vmem_limit_bytes=VMEM_LIMIT_BYTES),
        name="layer_state" if has_state else "layer_fresh",
    )(*args)
    return y, pool, C, n, m[:, 0, :H]


def kernel(x_prompt, x_sample, c_prompt, c_sample, state_pool, state_C, state_n, state_m,
           w_ada, b_ada, g_norm, w_in, b_i, b_f, w_pool, pool_scale, g_head, w_out, g_final):
    depth = w_ada.shape[0]
    assert depth == 1, "single-layer trunk"
    l = 0
    nbp = x_prompt.shape[0]
    n_main = N_MAIN_SECTIONS * W_POOL

    mod = _adaln_mod(jnp.concatenate([c_prompt, c_sample], axis=0), w_ada[l], b_ada[l])

    w_gate = jnp.pad(w_in[l][:, n_main:], ((0, 0), (0, LANES - 2 * N_HEADS))).astype(BF16)
    gate_bias = jnp.pad(jnp.concatenate([b_i[l], b_f[l]]), (0, LANES - 2 * N_HEADS)).reshape(1, LANES)
    weights = (
        g_norm[l].reshape(1, D_MODEL),
        _cast_columns_bf16(w_in[l], n_main, W_POOL),
        w_gate,
        gate_bias,
        w_pool[l].astype(BF16),
        pool_scale[l].reshape(1, W_POOL),
        g_head[l].reshape(1, W_MLSTM),
        w_out[l].astype(BF16),
        g_final.reshape(1, D_MODEL),
    )

    yp, pp, pc, pn, pm = _run_layer(x_prompt, mod, 0, None, weights, BB=1, TB=256, start=0)
    ys, sp, sc, sn, sm = _run_layer(
        x_sample, mod, nbp, (state_pool[l], state_C[l], state_n[l], state_m[l]), weights,
        BB=4, TB=x_sample.shape[1], start=PAST_LEN)
    return (yp, ys, pp[None], pc[None], pn[None], pm[None], sp[None], sc[None], sn[None], sm[None])
```

```python
import functools

import jax
import jax.numpy as jnp
from jax import lax
from jax.experimental import pallas as pl
from jax.experimental.pallas import tpu as pltpu

F32 = jnp.float32
BF16 = jnp.bfloat16

D_MODEL = 1024
W_POOL = 1024
W_MLSTM = 1024
POOL_WINDOWS = (2, 4, 8, 16)
N_POOL_GROUPS = 4
POOL_GW = W_POOL // N_POOL_GROUPS
POOL_BUF = 15
N_HEADS = 4
HEAD_DIM = W_MLSTM // N_HEADS
EPS = 1e-6
N_MAIN_SECTIONS = 7
LANES = 128
PAST_LEN = 2048
HIST = 16
VMEM_LIMIT_BYTES = 56 * 1024 * 1024


def _sigmoid(z):
    return 0.5 * jnp.tanh(0.5 * z) + 0.5


def _silu(z):
    hz = 0.5 * z
    return hz * jnp.tanh(hz) + hz


def _log_sigmoid(z):
    return jnp.minimum(z, 0.0) - jnp.log1p(jnp.exp(-jnp.abs(z)))


def _mod_kernel(c_ref, w_ref, b_ref, o_ref):
    c = c_ref[...]
    mod = jnp.dot(_silu(c), w_ref[...], preferred_element_type=F32) + b_ref[...]
    for b in range(mod.shape[0]):
        o_ref[b] = mod[b:b + 1, :]


def _adaln_mod(c, w_ada, b_ada):
    nb = c.shape[0]
    n_out = w_ada.shape[1]
    blk = D_MODEL
    return pl.pallas_call(
        _mod_kernel,
        grid=(n_out // blk,),
        in_specs=[
            pl.BlockSpec((nb, D_MODEL), lambda j: (0, 0)),
            pl.BlockSpec((D_MODEL, blk), lambda j: (0, j)),
            pl.BlockSpec((1, blk), lambda j: (0, j)),
        ],
        out_specs=pl.BlockSpec((nb, 1, blk), lambda j: (0, 0, j)),
        out_shape=jax.ShapeDtypeStruct((nb, 1, n_out), F32),
        name="adaln_mod",
    )(c, w_ada, b_ada.reshape(1, n_out))


def _cast_kernel(i_ref, o_ref):
    o_ref[...] = i_ref[...].T.astype(o_ref.dtype)


def _transposed_rows_bf16(wt, n_rows, blk):
    k = wt.shape[1]
    return pl.pallas_call(
        _cast_kernel,
        grid=(n_rows // blk,),
        in_specs=[pl.BlockSpec((blk, k), lambda j: (j, 0))],
        out_specs=pl.BlockSpec((k, blk), lambda j: (0, j)),
        out_shape=jax.ShapeDtypeStruct((k, n_rows), BF16),
        name="cast_bf16",
    )(wt)


def _seg_scan(x, tl, seg, combine, fill):
    k = 1
    while k < seg:
        shifted = pltpu.roll(x, k, 0)
        x = combine(x, jnp.where(tl >= k, shifted, fill))
        k *= 2
    return x


def _rows_of(per_batch_rows, TB):
    parts = [jnp.broadcast_to(r, (TB, LANES)) for r in per_batch_rows]
    return parts[0] if len(parts) == 1 else jnp.concatenate(parts, axis=0)


def _layer_kernel(*refs, BB, TB, start, has_state):
    R = BB * TB
    D, W, DH, GW = D_MODEL, W_POOL, HEAD_DIM, POOL_GW
    it = iter(refs)
    x_ref = next(it)
    mod_ref = next(it)
    if has_state:
        pool0_ref, C0_ref, n0_ref, m0_ref = next(it), next(it), next(it), next(it)
    gnorm_ref, win_ref, wg_ref, gbias_ref = next(it), next(it), next(it), next(it)
    wpool_ref, pscale_ref, ghead_ref, wout_ref, gfinal_ref = (next(it), next(it), next(it),
                                                             next(it), next(it))
    y_ref, pool_out_ref, C_ref, n_ref, m_ref = next(it), next(it), next(it), next(it), next(it)
    h_s, ext_s, pooled_s, zp_s, hm_s, q_s, k_s, v_s, ycat_s = it

    t = pl.program_id(1)

    @pl.when(t == 0)
    def _init():
        if has_state:
            ext_s[:, 0:1, :] = jnp.zeros((BB, 1, W), F32)
            ext_s[:, 1:HIST, :] = pool0_ref[...]
            C_ref[...] = C0_ref[...]
            n_ref[...] = n0_ref[...]
            m_ref[...] = m0_ref[...]
        else:
            ext_s[:, 0:HIST, :] = jnp.zeros((BB, HIST, W), F32)
            C_ref[...] = jnp.zeros_like(C_ref)
            n_ref[...] = jnp.zeros_like(n_ref)
            m_ref[...] = jnp.zeros_like(m_ref)

    for bb in range(BB):
        x = x_ref[bb]
        r = lax.rsqrt(jnp.mean(x * x, axis=-1, keepdims=True) + EPS)
        shift = mod_ref[bb, :, 0:D]
        scale = mod_ref[bb, :, D:2 * D]
        h = (x * r) * gnorm_ref[...] * (1.0 + scale) + shift
        h_s[bb * TB:(bb + 1) * TB, :] = h.astype(BF16)

    def proj(sec):
        return jnp.dot(h_s[...], win_ref[:, sec * W:(sec + 1) * W], preferred_element_type=F32)

    gates = jnp.dot(h_s[...], wg_ref[...], preferred_element_type=F32) + gbias_ref[...]

    tl = lax.broadcasted_iota(jnp.int32, (R, LANES), 0) & (TB - 1)
    ig = gates
    lf = _log_sigmoid(pltpu.roll(gates, LANES - N_HEADS, 1))
    Fc = _seg_scan(lf, tl, TB, jnp.add, 0.0)
    a = ig - Fc
    cmax = _seg_scan(a, tl, TB, jnp.maximum, -jnp.inf)
    m0_rows = [m_ref[bb] for bb in range(BB)]
    m0 = _rows_of(m0_rows, TB)
    m = Fc + jnp.maximum(m0, cmax)
    Fl_rows = [Fc[(bb + 1) * TB - 1:(bb + 1) * TB, :] for bb in range(BB)]
    mL_rows = [m[(bb + 1) * TB - 1:(bb + 1) * TB, :] for bb in range(BB)]
    Fl = _rows_of(Fl_rows, TB)
    mL = _rows_of(mL_rows, TB)
    decay0 = jnp.exp(m0 + Fc - m)
    wL = jnp.exp(a + Fl - mL)
    emm = jnp.exp(-m)
    Fm = Fc - m
    aT = a.T

    xp = proj(0)
    for bb in range(BB):
        ext_s[bb, HIST:HIST + TB, :] = xp[bb * TB:(bb + 1) * TB, :]
    zp_s[...] = _silu(proj(1))
    q_s[...] = proj(2).astype(BF16)
    k_s[...] = (proj(3) * (DH ** -0.5)).astype(BF16)
    v_s[...] = proj(4).astype(BF16)

    pos_head = start + t * TB + lax.broadcasted_iota(jnp.int32, (HIST, 1), 0)
    for g, w in enumerate(POOL_WINDOWS):
        cols = slice(g * GW, (g + 1) * GW)
        inv_head = 1.0 / jnp.minimum(pos_head + 1, w).astype(F32)
        for bb in range(BB):
            ext = ext_s[bb, :, cols]
            win = ext
            s = 1
            while s < w:
                win = win + pltpu.roll(win, s, 0)
                s *= 2
            r0 = bb * TB
            head = win[HIST:2 * HIST, :] * inv_head - ext[HIST:2 * HIST, :]
            tail = win[2 * HIST:, :] * (1.0 / w) - ext[2 * HIST:, :]
            pooled_s[r0:r0 + HIST, cols] = head.astype(BF16)
            pooled_s[r0 + HIST:r0 + TB, cols] = tail.astype(BF16)
    for g in range(N_POOL_GROUPS):
        cols = slice(g * GW, (g + 1) * GW)
        mixed = jnp.dot(pooled_s[:, cols], wpool_ref[g], preferred_element_type=F32)
        ycat_s[:, cols] = (mixed * pscale_ref[:, cols] * zp_s[:, cols]).astype(BF16)

    pool_out_ref[...] = ext_s[:, TB + 1:TB + HIST, :]

    for bb in range(BB):
        ext_s[bb, 0:HIST, :] = ext_s[bb, TB:TB + HIST, :]

    causal = (lax.broadcasted_iota(jnp.int32, (TB, TB), 0)
              >= lax.broadcasted_iota(jnp.int32, (TB, TB), 1))

    for bb in range(BB):
        rows = slice(bb * TB, (bb + 1) * TB)
        dL = jnp.exp(m0_rows[bb] + Fl_rows[bb] - mL_rows[bb])
        for hd in range(N_HEADS):
            cols = slice(hd * DH, (hd + 1) * DH)
            q = q_s[rows, cols]
            k = k_s[rows, cols]
            v = v_s[rows, cols]
            s = lax.dot_general(q, k, (((1,), (1,)), ((), ())), preferred_element_type=F32)
            logD = Fm[rows, hd:hd + 1] + aT[hd:hd + 1, rows]
            S = s * jnp.exp(jnp.where(causal, logD, -jnp.inf))
            C0 = C_ref[bb, hd]
            n0 = n_ref[bb, hd:hd + 1, :]
            d0 = decay0[rows, hd:hd + 1]
            qn = jnp.sum(q.astype(F32) * n0, axis=-1, keepdims=True)
            nq = jnp.sum(S, axis=-1, keepdims=True) + d0 * qn
            num = (jnp.dot(S.astype(BF16), v, preferred_element_type=F32)
                   + d0 * jnp.dot(q, C0.astype(BF16), preferred_element_type=F32))
            den = jnp.maximum(jnp.abs(nq), emm[rows, hd:hd + 1])
            hh = num * (1.0 / den)
            hc = hh - jnp.mean(hh, axis=-1, keepdims=True)
            hn = hc * lax.rsqrt(jnp.mean(hc * hc, axis=-1, keepdims=True) + EPS)
            hm_s[rows, cols] = hn * ghead_ref[:, cols]
            kw = k.astype(F32) * wL[rows, hd:hd + 1]
            dl = dL[:, hd:hd + 1]
            C_ref[bb, hd] = dl * C0 + lax.dot_general(
                kw.astype(BF16), v, (((0,), (0,)), ((), ())), preferred_element_type=F32)
            n_ref[bb, hd:hd + 1, :] = dl * n0 + jnp.sum(kw, axis=0, keepdims=True)
        m_ref[bb] = mL_rows[bb]

    hm_s[...] = hm_s[...] * _sigmoid(proj(5))
    ycat_s[:, W:2 * W] = (hm_s[...] * _silu(proj(6))).astype(BF16)

    y = jnp.dot(ycat_s[...], wout_ref[...], preferred_element_type=F32)
    for bb in range(BB):
        gate = mod_ref[bb, :, 2 * D:3 * D]
        xn = x_ref[bb] + gate * y[bb * TB:(bb + 1) * TB, :]
        r = lax.rsqrt(jnp.mean(xn * xn, axis=-1, keepdims=True) + EPS)
        y_ref[bb] = (xn * r) * gfinal_ref[...]


def _const_spec(shape):
    nd = len(shape)
    return pl.BlockSpec(shape, lambda b, t: (0,) * nd, pipeline_mode=pl.Buffered(1))


def _run_layer(x, mod, mod_row0, state, weights, *, BB, TB, start):
    B, T, D = x.shape
    W, H, DH = W_POOL, N_HEADS, HEAD_DIM
    R = BB * TB
    has_state = state is not None
    grid = (B // BB, T // TB)
    assert mod_row0 % BB == 0
    mod_blk0 = mod_row0 // BB

    in_specs = [
        pl.BlockSpec((BB, TB, D), lambda b, t: (b, t, 0)),
        pl.BlockSpec((BB, 1, 3 * D), lambda b, t: (b + mod_blk0, 0, 0)),
    ]
    args = [x, mod]
    if has_state:
        pool0, C0, n0, m0 = state
        in_specs += [
            pl.BlockSpec((BB, POOL_BUF, W), lambda b, t: (b, 0, 0)),
            pl.BlockSpec((BB, H, DH, DH), lambda b, t: (b, 0, 0, 0)),
            pl.BlockSpec((BB, H, DH), lambda b, t: (b, 0, 0)),
            pl.BlockSpec((BB, 1, LANES), lambda b, t: (b, 0, 0)),
        ]
        m0p = jnp.pad(m0, ((0, 0), (0, LANES - H))).reshape(B, 1, LANES)
        args += [pool0, C0, n0, m0p]
    in_specs += [_const_spec(w.shape) for w in weights]
    args += list(weights)

    out_shape = (
        jax.ShapeDtypeStruct((B, T, D), F32),
        jax.ShapeDtypeStruct((B, POOL_BUF, W), F32),
        jax.ShapeDtypeStruct((B, H, DH, DH), F32),
        jax.ShapeDtypeStruct((B, H, DH), F32),
        jax.ShapeDtypeStruct((B, 1, LANES), F32),
    )
    out_specs = (
        pl.BlockSpec((BB, TB, D), lambda b, t: (b, t, 0)),
        pl.BlockSpec((BB, POOL_BUF, W), lambda b, t: (b, 0, 0)),
        pl.BlockSpec((BB, H, DH, DH), lambda b, t: (b, 0, 0, 0)),
        pl.BlockSpec((BB, H, DH), lambda b, t: (b, 0, 0)),
        pl.BlockSpec((BB, 1, LANES), lambda b, t: (b, 0, 0)),
    )
    scratch = [
        pltpu.VMEM((R, D), BF16),
        pltpu.VMEM((BB, HIST + TB, W), F32),
        pltpu.VMEM((R, W), BF16),
        pltpu.VMEM((R, W), F32),
        pltpu.VMEM((R, W), F32),
        pltpu.VMEM((R, W), BF16),
        pltpu.VMEM((R, W), BF16),
        pltpu.VMEM((R, W), BF16),
        pltpu.VMEM((R, 2 * W), BF16),
    ]
    y, pool, C, n, m = pl.pallas_call(
        functools.partial(_layer_kernel, BB=BB, TB=TB, start=start, has_state=has_state),
        grid=grid,
        in_specs=in_specs,
        out_specs=out_specs,
        out_shape=out_shape,
        scratch_shapes=scratch,
        compiler_params=pltpu.CompilerParams(
            dimension_semantics=("parallel", "arbitrary"),
            vmem_limit_bytes=VMEM_LIMIT_BYTES),
        name="layer_state" if has_state else "layer_fresh",
    )(*args)
    return y, pool, C, n, m[:, 0, :H]


def kernel(x_prompt, x_sample, c_prompt, c_sample, state_pool, state_C, state_n, state_m,
           w_ada, b_ada, g_norm, w_in, b_i, b_f, w_pool, pool_scale, g_head, w_out, g_final):
    depth = w_ada.shape[0]
    assert depth == 1, "single-layer trunk"
    l = 0
    nbp = x_prompt.shape[0]
    n_main = N_MAIN_SECTIONS * W_POOL

    mod = _adaln_mod(jnp.concatenate([c_prompt, c_sample], axis=0), w_ada[l], b_ada[l])

    w_in_t = jnp.swapaxes(w_in[l], 0, 1)
    w_gate = jnp.pad(w_in_t[n_main:, :].T, ((0, 0), (0, LANES - 2 * N_HEADS))).astype(BF16)
    gate_bias = jnp.pad(jnp.concatenate([b_i[l], b_f[l]]), (0, LANES - 2 * N_HEADS)).reshape(1, LANES)
    weights = (
        g_norm[l].reshape(1, D_MODEL),
        _transposed_rows_bf16(w_in_t, n_main, W_POOL),
        w_gate,
        gate_bias,
        w_pool[l].astype(BF16),
        pool_scale[l].reshape(1, W_POOL),
        g_head[l].reshape(1, W_MLSTM),
        w_out[l].astype(BF16),
        g_final.reshape(1, D_MODEL),
    )

    yp, pp, pc, pn, pm = _run_layer(x_prompt, mod, 0, None, weights, BB=1, TB=256, start=0)
    ys, sp, sc, sn, sm = _run_layer(
        x_sample, mod, nbp, (state_pool[l], state_C[l], state_n[l], state_m[l]), weights,
        BB=4, TB=x_sample.shape[1], start=PAST_LEN)
    return (yp, ys, pp[None], pc[None], pn[None], pm[None], sp[None], sc[None], sn[None], sm[None])
```

```python
import functools

import jax
import jax.numpy as jnp
from jax import lax
from jax.experimental import pallas as pl
from jax.experimental.pallas import tpu as pltpu

F32 = jnp.float32
BF16 = jnp.bfloat16

D_MODEL = 1024
W_POOL = 1024
W_MLSTM = 1024
POOL_WINDOWS = (2, 4, 8, 16)
N_POOL_GROUPS = 4
POOL_GW = W_POOL // N_POOL_GROUPS
POOL_BUF = 15
N_HEADS = 4
HEAD_DIM = W_MLSTM // N_HEADS
EPS = 1e-6
N_MAIN_SECTIONS = 7
LANES = 128
PAST_LEN = 2048
HIST = 16
VMEM_LIMIT_BYTES = 56 * 1024 * 1024


def _sigmoid(z):
    return 0.5 * jnp.tanh(0.5 * z) + 0.5


def _silu(z):
    hz = 0.5 * z
    return hz * jnp.tanh(hz) + hz


def _log_sigmoid(z):
    return jnp.minimum(z, 0.0) - jnp.log1p(jnp.exp(-jnp.abs(z)))


def _mod_kernel(c_ref, w_ref, b_ref, o_ref):
    c = c_ref[...]
    mod = jnp.dot(_silu(c), w_ref[...], preferred_element_type=F32) + b_ref[...]
    for b in range(mod.shape[0]):
        o_ref[b] = mod[b:b + 1, :]


def _adaln_mod(c, w_ada, b_ada):
    nb = c.shape[0]
    n_out = w_ada.shape[1]
    blk = D_MODEL
    return pl.pallas_call(
        _mod_kernel,
        grid=(n_out // blk,),
        in_specs=[
            pl.BlockSpec((nb, D_MODEL), lambda j: (0, 0)),
            pl.BlockSpec((D_MODEL, blk), lambda j: (0, j)),
            pl.BlockSpec((1, blk), lambda j: (0, j)),
        ],
        out_specs=pl.BlockSpec((nb, 1, blk), lambda j: (0, 0, j)),
        out_shape=jax.ShapeDtypeStruct((nb, 1, n_out), F32),
        name="adaln_mod",
    )(c, w_ada, b_ada.reshape(1, n_out))


def _cast_kernel(i_ref, o_ref):
    o_ref[...] = i_ref[...].T.astype(o_ref.dtype)


def _transposed_rows_bf16(wt, n_rows, blk):
    k = wt.shape[1]
    return pl.pallas_call(
        _cast_kernel,
        grid=(n_rows // blk,),
        in_specs=[pl.BlockSpec((blk, k), lambda j: (j, 0))],
        out_specs=pl.BlockSpec((k, blk), lambda j: (0, j)),
        out_shape=jax.ShapeDtypeStruct((k, n_rows), BF16),
        name="cast_bf16",
    )(wt)


def _copy_kernel(i_ref, o_ref):
    o_ref[...] = i_ref[...]


def _row_block(wt, blk_index, blk):
    k = wt.shape[1]
    return pl.pallas_call(
        _copy_kernel,
        grid=(1,),
        in_specs=[pl.BlockSpec((blk, k), lambda j: (blk_index, 0))],
        out_specs=pl.BlockSpec((blk, k), lambda j: (0, 0)),
        out_shape=jax.ShapeDtypeStruct((blk, k), wt.dtype),
        name="row_block",
    )(wt)


def _seg_scan(x, tl, seg, combine, fill):
    k = 1
    while k < seg:
        shifted = pltpu.roll(x, k, 0)
        x = combine(x, jnp.where(tl >= k, shifted, fill))
        k *= 2
    return x


def _rows_of(per_batch_rows, TB):
    parts = [jnp.broadcast_to(r, (TB, LANES)) for r in per_batch_rows]
    return parts[0] if len(parts) == 1 else jnp.concatenate(parts, axis=0)


def _layer_kernel(*refs, BB, TB, start, has_state):
    R = BB * TB
    D, W, DH, GW = D_MODEL, W_POOL, HEAD_DIM, POOL_GW
    it = iter(refs)
    x_ref = next(it)
    mod_ref = next(it)
    if has_state:
        pool0_ref, C0_ref, n0_ref, m0_ref = next(it), next(it), next(it), next(it)
    gnorm_ref, win_ref, wg_ref, gbias_ref = next(it), next(it), next(it), next(it)
    wpool_ref, pscale_ref, ghead_ref, wout_ref, gfinal_ref = (next(it), next(it), next(it),
                                                             next(it), next(it))
    y_ref, pool_out_ref, C_ref, n_ref, m_ref = next(it), next(it), next(it), next(it), next(it)
    h_s, ext_s, pooled_s, zp_s, hm_s, q_s, k_s, v_s, ycat_s = it

    t = pl.program_id(1)

    @pl.when(t == 0)
    def _init():
        if has_state:
            ext_s[:, 0:1, :] = jnp.zeros((BB, 1, W), F32)
            ext_s[:, 1:HIST, :] = pool0_ref[...]
            C_ref[...] = C0_ref[...]
            n_ref[...] = n0_ref[...]
            m_ref[...] = m0_ref[...]
        else:
            ext_s[:, 0:HIST, :] = jnp.zeros((BB, HIST, W), F32)
            C_ref[...] = jnp.zeros_like(C_ref)
            n_ref[...] = jnp.zeros_like(n_ref)
            m_ref[...] = jnp.zeros_like(m_ref)

    for bb in range(BB):
        x = x_ref[bb]
        r = lax.rsqrt(jnp.mean(x * x, axis=-1, keepdims=True) + EPS)
        shift = mod_ref[bb, :, 0:D]
        scale = mod_ref[bb, :, D:2 * D]
        h = (x * r) * gnorm_ref[...] * (1.0 + scale) + shift
        h_s[bb * TB:(bb + 1) * TB, :] = h.astype(BF16)

    def proj(sec):
        return jnp.dot(h_s[...], win_ref[:, sec * W:(sec + 1) * W], preferred_element_type=F32)

    gates = jnp.dot(h_s[...], wg_ref[...], preferred_element_type=F32) + gbias_ref[...]

    tl = lax.broadcasted_iota(jnp.int32, (R, LANES), 0) & (TB - 1)
    ig = gates
    lf = _log_sigmoid(pltpu.roll(gates, LANES - N_HEADS, 1))
    Fc = _seg_scan(lf, tl, TB, jnp.add, 0.0)
    a = ig - Fc
    cmax = _seg_scan(a, tl, TB, jnp.maximum, -jnp.inf)
    m0_rows = [m_ref[bb] for bb in range(BB)]
    m0 = _rows_of(m0_rows, TB)
    m = Fc + jnp.maximum(m0, cmax)
    Fl_rows = [Fc[(bb + 1) * TB - 1:(bb + 1) * TB, :] for bb in range(BB)]
    mL_rows = [m[(bb + 1) * TB - 1:(bb + 1) * TB, :] for bb in range(BB)]
    Fl = _rows_of(Fl_rows, TB)
    mL = _rows_of(mL_rows, TB)
    decay0 = jnp.exp(m0 + Fc - m)
    wL = jnp.exp(a + Fl - mL)
    emm = jnp.exp(-m)
    Fm = Fc - m
    aT = a.T

    xp = proj(0)
    for bb in range(BB):
        ext_s[bb, HIST:HIST + TB, :] = xp[bb * TB:(bb + 1) * TB, :]
    zp_s[...] = _silu(proj(1))
    q_s[...] = proj(2).astype(BF16)
    k_s[...] = (proj(3) * (DH ** -0.5)).astype(BF16)
    v_s[...] = proj(4).astype(BF16)

    pos_head = start + t * TB + lax.broadcasted_iota(jnp.int32, (HIST, 1), 0)
    for g, w in enumerate(POOL_WINDOWS):
        cols = slice(g * GW, (g + 1) * GW)
        inv_head = 1.0 / jnp.minimum(pos_head + 1, w).astype(F32)
        for bb in range(BB):
            ext = ext_s[bb, :, cols]
            win = ext
            s = 1
            while s < w:
                win = win + pltpu.roll(win, s, 0)
                s *= 2
            r0 = bb * TB
            head = win[HIST:2 * HIST, :] * inv_head - ext[HIST:2 * HIST, :]
            tail = win[2 * HIST:, :] * (1.0 / w) - ext[2 * HIST:, :]
            pooled_s[r0:r0 + HIST, cols] = head.astype(BF16)
            pooled_s[r0 + HIST:r0 + TB, cols] = tail.astype(BF16)
    for g in range(N_POOL_GROUPS):
        cols = slice(g * GW, (g + 1) * GW)
        mixed = jnp.dot(pooled_s[:, cols], wpool_ref[g], preferred_element_type=F32)
        ycat_s[:, cols] = (mixed * pscale_ref[:, cols] * zp_s[:, cols]).astype(BF16)

    pool_out_ref[...] = ext_s[:, TB + 1:TB + HIST, :]

    for bb in range(BB):
        ext_s[bb, 0:HIST, :] = ext_s[bb, TB:TB + HIST, :]

    causal = (lax.broadcasted_iota(jnp.int32, (TB, TB), 0)
              >= lax.broadcasted_iota(jnp.int32, (TB, TB), 1))

    for bb in range(BB):
        rows = slice(bb * TB, (bb + 1) * TB)
        dL = jnp.exp(m0_rows[bb] + Fl_rows[bb] - mL_rows[bb])
        for hd in range(N_HEADS):
            cols = slice(hd * DH, (hd + 1) * DH)
            q = q_s[rows, cols]
            k = k_s[rows, cols]
            v = v_s[rows, cols]
            s = lax.dot_general(q, k, (((1,), (1,)), ((), ())), preferred_element_type=F32)
            logD = Fm[rows, hd:hd + 1] + aT[hd:hd + 1, rows]
            S = s * jnp.exp(jnp.where(causal, logD, -jnp.inf))
            C0 = C_ref[bb, hd]
            n0 = n_ref[bb, hd:hd + 1, :]
            d0 = decay0[rows, hd:hd + 1]
            qn = jnp.sum(q.astype(F32) * n0, axis=-1, keepdims=True)
            nq = jnp.sum(S, axis=-1, keepdims=True) + d0 * qn
            num = (jnp.dot(S.astype(BF16), v, preferred_element_type=F32)
                   + d0 * jnp.dot(q, C0.astype(BF16), preferred_element_type=F32))
            den = jnp.maximum(jnp.abs(nq), emm[rows, hd:hd + 1])
            hh = num * (1.0 / den)
            hc = hh - jnp.mean(hh, axis=-1, keepdims=True)
            hn = hc * lax.rsqrt(jnp.mean(hc * hc, axis=-1, keepdims=True) + EPS)
            hm_s[rows, cols] = hn * ghead_ref[:, cols]
            kw = k.astype(F32) * wL[rows, hd:hd + 1]
            dl = dL[:, hd:hd + 1]
            C_ref[bb, hd] = dl * C0 + lax.dot_general(
                kw.astype(BF16), v, (((0,), (0,)), ((), ())), preferred_element_type=F32)
            n_ref[bb, hd:hd + 1, :] = dl * n0 + jnp.sum(kw, axis=0, keepdims=True)
        m_ref[bb] = mL_rows[bb]

    hm_s[...] = hm_s[...] * _sigmoid(proj(5))
    ycat_s[:, W:2 * W] = (hm_s[...] * _silu(proj(6))).astype(BF16)

    y = jnp.dot(ycat_s[...], wout_ref[...], preferred_element_type=F32)
    for bb in range(BB):
        gate = mod_ref[bb, :, 2 * D:3 * D]
        xn = x_ref[bb] + gate * y[bb * TB:(bb + 1) * TB, :]
        r = lax.rsqrt(jnp.mean(xn * xn, axis=-1, keepdims=True) + EPS)
        y_ref[bb] = (xn * r) * gfinal_ref[...]


def _const_spec(shape):
    nd = len(shape)
    return pl.BlockSpec(shape, lambda b, t: (0,) * nd, pipeline_mode=pl.Buffered(1))


def _run_layer(x, mod, mod_row0, state, weights, *, BB, TB, start):
    B, T, D = x.shape
    W, H, DH = W_POOL, N_HEADS, HEAD_DIM
    R = BB * TB
    has_state = state is not None
    grid = (B // BB, T // TB)
    assert mod_row0 % BB == 0
    mod_blk0 = mod_row0 // BB

    in_specs = [
        pl.BlockSpec((BB, TB, D), lambda b, t: (b, t, 0)),
        pl.BlockSpec((BB, 1, 3 * D), lambda b, t: (b + mod_blk0, 0, 0)),
    ]
    args = [x, mod]
    if has_state:
        pool0, C0, n0, m0 = state
        in_specs += [
            pl.BlockSpec((BB, POOL_BUF, W), lambda b, t: (b, 0, 0)),
            pl.BlockSpec((BB, H, DH, DH), lambda b, t: (b, 0, 0, 0)),
            pl.BlockSpec((BB, H, DH), lambda b, t: (b, 0, 0)),
            pl.BlockSpec((BB, 1, LANES), lambda b, t: (b, 0, 0)),
        ]
        m0p = jnp.pad(m0, ((0, 0), (0, LANES - H))).reshape(B, 1, LANES)
        args += [pool0, C0, n0, m0p]
    in_specs += [_const_spec(w.shape) for w in weights]
    args += list(weights)

    out_shape = (
        jax.ShapeDtypeStruct((B, T, D), F32),
        jax.ShapeDtypeStruct((B, POOL_BUF, W), F32),
        jax.ShapeDtypeStruct((B, H, DH, DH), F32),
        jax.ShapeDtypeStruct((B, H, DH), F32),
        jax.ShapeDtypeStruct((B, 1, LANES), F32),
    )
    out_specs = (
        pl.BlockSpec((BB, TB, D), lambda b, t: (b, t, 0)),
        pl.BlockSpec((BB, POOL_BUF, W), lambda b, t: (b, 0, 0)),
        pl.BlockSpec((BB, H, DH, DH), lambda b, t: (b, 0, 0, 0)),
        pl.BlockSpec((BB, H, DH), lambda b, t: (b, 0, 0)),
        pl.BlockSpec((BB, 1, LANES), lambda b, t: (b, 0, 0)),
    )
    scratch = [
        pltpu.VMEM((R, D), BF16),
        pltpu.VMEM((BB, HIST + TB, W), F32),
        pltpu.VMEM((R, W), BF16),
        pltpu.VMEM((R, W), F32),
        pltpu.VMEM((R, W), F32),
        pltpu.VMEM((R, W), BF16),
        pltpu.VMEM((R, W), BF16),
        pltpu.VMEM((R, W), BF16),
        pltpu.VMEM((R, 2 * W), BF16),
    ]
    y, pool, C, n, m = pl.pallas_call(
        functools.partial(_layer_kernel, BB=BB, TB=TB, start=start, has_state=has_state),
        grid=grid,
        in_specs=in_specs,
        out_specs=out_specs,
        out_shape=out_shape,
        scratch_shapes=scratch,
        compiler_params=pltpu.CompilerParams(
            dimension_semantics=("parallel", "arbitrary"),
            vmem_limit_bytes=VMEM_LIMIT_BYTES),
        name="layer_state" if has_state else "layer_fresh",
    )(*args)
    return y, pool, C, n, m[:, 0, :H]


def kernel(x_prompt, x_sample, c_prompt, c_sample, state_pool, state_C, state_n, state_m,
           w_ada, b_ada, g_norm, w_in, b_i, b_f, w_pool, pool_scale, g_head, w_out, g_final):
    depth = w_ada.shape[0]
    assert depth == 1, "single-layer trunk"
    l = 0
    nbp = x_prompt.shape[0]
    n_main = N_MAIN_SECTIONS * W_POOL

    mod = _adaln_mod(jnp.concatenate([c_prompt, c_sample], axis=0), w_ada[l], b_ada[l])

    w_in_t = jnp.swapaxes(w_in[l], 0, 1)
    n_gate = 2 * N_HEADS
    assert n_main % n_gate == 0 and w_in_t.shape[0] == n_main + n_gate
    w_gate_t = _row_block(w_in_t, n_main // n_gate, n_gate)
    w_gate = jnp.pad(w_gate_t.T, ((0, 0), (0, LANES - n_gate))).astype(BF16)
    gate_bias = jnp.pad(jnp.concatenate([b_i[l], b_f[l]]), (0, LANES - 2 * N_HEADS)).reshape(1, LANES)
    weights = (
        g_norm[l].reshape(1, D_MODEL),
        _transposed_rows_bf16(w_in_t, n_main, W_POOL),
        w_gate,
        gate_bias,
        w_pool[l].astype(BF16),
        pool_scale[l].reshape(1, W_POOL),
        g_head[l].reshape(1, W_MLSTM),
        w_out[l].astype(BF16),
        g_final.reshape(1, D_MODEL),
    )

    yp, pp, pc, pn, pm = _run_layer(x_prompt, mod, 0, None, weights, BB=1, TB=256, start=0)
    ys, sp, sc, sn, sm = _run_layer(
        x_sample, mod, nbp, (state_pool[l], state_C[l], state_n[l], state_m[l]), weights,
        BB=4, TB=x_sample.shape[1], start=PAST_LEN)
    return (yp, ys, pp[None], pc[None], pn[None], pm[None], sp[None], sc[None], sn[None], sm[None])
```

```python
import functools

import jax
import jax.numpy as jnp
from jax import lax
from jax.experimental import pallas as pl
from jax.experimental.pallas import tpu as pltpu

F32 = jnp.float32
BF16 = jnp.bfloat16

D_MODEL = 1024
W_POOL = 1024
W_MLSTM = 1024
POOL_WINDOWS = (2, 4, 8, 16)
N_POOL_GROUPS = 4
POOL_GW = W_POOL // N_POOL_GROUPS
POOL_BUF = 15
N_HEADS = 4
HEAD_DIM = W_MLSTM // N_HEADS
EPS = 1e-6
N_MAIN_SECTIONS = 7
LANES = 128
PAST_LEN = 2048
HIST = 16
VMEM_LIMIT_BYTES = 56 * 1024 * 1024


def _sigmoid(z):
    return 0.5 * jnp.tanh(0.5 * z) + 0.5


def _silu(z):
    hz = 0.5 * z
    return hz * jnp.tanh(hz) + hz


def _log_sigmoid(z):
    return jnp.minimum(z, 0.0) - jnp.log1p(jnp.exp(-jnp.abs(z)))


def _mod_kernel(c_ref, w_ref, b_ref, o_ref):
    c = c_ref[...]
    mod = jnp.dot(_silu(c), w_ref[...], preferred_element_type=F32) + b_ref[...]
    for b in range(mod.shape[0]):
        o_ref[b] = mod[b:b + 1, :]


def _adaln_mod(c, w_ada, b_ada):
    nb = c.shape[0]
    n_out = w_ada.shape[1]
    blk = D_MODEL
    return pl.pallas_call(
        _mod_kernel,
        grid=(n_out // blk,),
        in_specs=[
            pl.BlockSpec((nb, D_MODEL), lambda j: (0, 0)),
            pl.BlockSpec((D_MODEL, blk), lambda j: (0, j)),
            pl.BlockSpec((1, blk), lambda j: (0, j)),
        ],
        out_specs=pl.BlockSpec((nb, 1, blk), lambda j: (0, 0, j)),
        out_shape=jax.ShapeDtypeStruct((nb, 1, n_out), F32),
        name="adaln_mod",
    )(c, w_ada, b_ada.reshape(1, n_out))


def _cast_kernel(i_ref, o_ref):
    o_ref[...] = i_ref[...].T.astype(o_ref.dtype)


def _transposed_rows_bf16(wt, n_rows, blk):
    k = wt.shape[1]
    return pl.pallas_call(
        _cast_kernel,
        grid=(n_rows // blk,),
        in_specs=[pl.BlockSpec((blk, k), lambda j: (j, 0))],
        out_specs=pl.BlockSpec((k, blk), lambda j: (0, j)),
        out_shape=jax.ShapeDtypeStruct((k, n_rows), BF16),
        name="cast_bf16",
    )(wt)


def _copy_kernel(i_ref, o_ref):
    o_ref[...] = i_ref[...]


def _row_block(wt, blk_index, blk):
    k = wt.shape[1]
    return pl.pallas_call(
        _copy_kernel,
        grid=(1,),
        in_specs=[pl.BlockSpec((blk, k), lambda j: (blk_index, 0))],
        out_specs=pl.BlockSpec((blk, k), lambda j: (0, 0)),
        out_shape=jax.ShapeDtypeStruct((blk, k), wt.dtype),
        name="row_block",
    )(wt)


def _seg_scan(x, tl, seg, combine, fill):
    k = 1
    while k < seg:
        shifted = pltpu.roll(x, k, 0)
        x = combine(x, jnp.where(tl >= k, shifted, fill))
        k *= 2
    return x


def _rows_of(per_batch_rows, TB):
    parts = [jnp.broadcast_to(r, (TB, LANES)) for r in per_batch_rows]
    return parts[0] if len(parts) == 1 else jnp.concatenate(parts, axis=0)


def _layer_kernel(*refs, BB, TB, start, has_state):
    R = BB * TB
    D, W, DH, GW = D_MODEL, W_POOL, HEAD_DIM, POOL_GW
    it = iter(refs)
    x_ref = next(it)
    mod_ref = next(it)
    if has_state:
        pool0_ref, C0_ref, n0_ref, m0_ref = next(it), next(it), next(it), next(it)
    gnorm_ref, win_ref, wg_ref, gbias_ref = next(it), next(it), next(it), next(it)
    wpool_ref, pscale_ref, ghead_ref, wout_ref, gfinal_ref = (next(it), next(it), next(it),
                                                             next(it), next(it))
    y_ref, pool_out_ref, C_ref, n_ref, m_ref = next(it), next(it), next(it), next(it), next(it)
    h_s, ext_s, pooled_s, zp_s, hm_s, q_s, k_s, v_s, ycat_s = it

    t = pl.program_id(1)

    @pl.when(t == 0)
    def _init():
        if has_state:
            ext_s[:, 0:1, :] = jnp.zeros((BB, 1, W), F32)
            ext_s[:, 1:HIST, :] = pool0_ref[...]
            C_ref[...] = C0_ref[...]
            n_ref[...] = n0_ref[...]
            m_ref[...] = m0_ref[...]
        else:
            ext_s[:, 0:HIST, :] = jnp.zeros((BB, HIST, W), F32)
            C_ref[...] = jnp.zeros_like(C_ref)
            n_ref[...] = jnp.zeros_like(n_ref)
            m_ref[...] = jnp.zeros_like(m_ref)

    for bb in range(BB):
        x = x_ref[bb]
        r = lax.rsqrt(jnp.mean(x * x, axis=-1, keepdims=True) + EPS)
        shift = mod_ref[bb, :, 0:D]
        scale = mod_ref[bb, :, D:2 * D]
        h = (x * r) * gnorm_ref[...] * (1.0 + scale) + shift
        h_s[bb * TB:(bb + 1) * TB, :] = h.astype(BF16)

    def proj(sec):
        return jnp.dot(h_s[...], win_ref[:, sec * W:(sec + 1) * W], preferred_element_type=F32)

    gates = jnp.dot(h_s[...], wg_ref[...], preferred_element_type=F32) + gbias_ref[...]

    tl = lax.broadcasted_iota(jnp.int32, (R, LANES), 0) & (TB - 1)
    ig = gates
    lf = _log_sigmoid(pltpu.roll(gates, LANES - N_HEADS, 1))
    Fc = _seg_scan(lf, tl, TB, jnp.add, 0.0)
    a = ig - Fc
    cmax = _seg_scan(a, tl, TB, jnp.maximum, -jnp.inf)
    m0_rows = [m_ref[bb] for bb in range(BB)]
    m0 = _rows_of(m0_rows, TB)
    m = Fc + jnp.maximum(m0, cmax)
    Fl_rows = [Fc[(bb + 1) * TB - 1:(bb + 1) * TB, :] for bb in range(BB)]
    mL_rows = [m[(bb + 1) * TB - 1:(bb + 1) * TB, :] for bb in range(BB)]
    Fl = _rows_of(Fl_rows, TB)
    mL = _rows_of(mL_rows, TB)
    decay0 = jnp.exp(m0 + Fc - m)
    wL = jnp.exp(a + Fl - mL)
    emm = jnp.exp(-m)
    Fm = Fc - m
    aT = a.T

    xp = proj(0)
    for bb in range(BB):
        ext_s[bb, HIST:HIST + TB, :] = xp[bb * TB:(bb + 1) * TB, :]
    zp_s[...] = _silu(proj(1))
    q_s[...] = proj(2).astype(BF16)
    k_s[...] = (proj(3) * (DH ** -0.5)).astype(BF16)
    v_s[...] = proj(4).astype(BF16)

    pos_head = start + t * TB + lax.broadcasted_iota(jnp.int32, (HIST, 1), 0)
    for g, w in enumerate(POOL_WINDOWS):
        cols = slice(g * GW, (g + 1) * GW)
        inv_head = 1.0 / jnp.minimum(pos_head + 1, w).astype(F32)
        for bb in range(BB):
            ext = ext_s[bb, :, cols]
            win = ext
            s = 1
            while s < w:
                win = win + pltpu.roll(win, s, 0)
                s *= 2
            r0 = bb * TB
            head = win[HIST:2 * HIST, :] * inv_head - ext[HIST:2 * HIST, :]
            tail = win[2 * HIST:, :] * (1.0 / w) - ext[2 * HIST:, :]
            pooled_s[r0:r0 + HIST, cols] = head.astype(BF16)
            pooled_s[r0 + HIST:r0 + TB, cols] = tail.astype(BF16)
    for g in range(N_POOL_GROUPS):
        cols = slice(g * GW, (g + 1) * GW)
        mixed = jnp.dot(pooled_s[:, cols], wpool_ref[g], preferred_element_type=F32)
        ycat_s[:, cols] = (mixed * pscale_ref[:, cols] * zp_s[:, cols]).astype(BF16)

    pool_out_ref[...] = ext_s[:, TB + 1:TB + HIST, :]

    for bb in range(BB):
        ext_s[bb, 0:HIST, :] = ext_s[bb, TB:TB + HIST, :]

    causal = (lax.broadcasted_iota(jnp.int32, (TB, TB), 0)
              >= lax.broadcasted_iota(jnp.int32, (TB, TB), 1))

    for bb in range(BB):
        rows = slice(bb * TB, (bb + 1) * TB)
        dL = jnp.exp(m0_rows[bb] + Fl_rows[bb] - mL_rows[bb])
        for hd in range(N_HEADS):
            cols = slice(hd * DH, (hd + 1) * DH)
            q = q_s[rows, cols]
            k = k_s[rows, cols]
            v = v_s[rows, cols]
            s = lax.dot_general(q, k, (((1,), (1,)), ((), ())), preferred_element_type=F32)
            logD = Fm[rows, hd:hd + 1] + aT[hd:hd + 1, rows]
            S = s * jnp.exp(jnp.where(causal, logD, -jnp.inf))
            C0 = C_ref[bb, hd]
            n0 = n_ref[bb, hd:hd + 1, :]
            d0 = decay0[rows, hd:hd + 1]
            qn = jnp.sum(q.astype(F32) * n0, axis=-1, keepdims=True)
            nq = jnp.sum(S, axis=-1, keepdims=True) + d0 * qn
            num = (jnp.dot(S.astype(BF16), v, preferred_element_type=F32)
                   + d0 * jnp.dot(q, C0.astype(BF16), preferred_element_type=F32))
            den = jnp.maximum(jnp.abs(nq), emm[rows, hd:hd + 1])
            hh = num * (1.0 / den)
            hc = hh - jnp.mean(hh, axis=-1, keepdims=True)
            hn = hc * lax.rsqrt(jnp.mean(hc * hc, axis=-1, keepdims=True) + EPS)
            hm_s[rows, cols] = hn * ghead_ref[:, cols]
            kw = k.astype(F32) * wL[rows, hd:hd + 1]
            dl = dL[:, hd:hd + 1]
            C_ref[bb, hd] = dl * C0 + lax.dot_general(
                kw.astype(BF16), v, (((0,), (0,)), ((), ())), preferred_element_type=F32)
            n_ref[bb, hd:hd + 1, :] = dl * n0 + jnp.sum(kw, axis=0, keepdims=True)
        m_ref[bb] = mL_rows[bb]

    hm_s[...] = hm_s[...] * _sigmoid(proj(5))
    ycat_s[:, W:2 * W] = (hm_s[...] * _silu(proj(6))).astype(BF16)

    y = jnp.dot(ycat_s[...], wout_ref[...], preferred_element_type=F32)
    for bb in range(BB):
        gate = mod_ref[bb, :, 2 * D:3 * D]
        xn = x_ref[bb] + gate * y[bb * TB:(bb + 1) * TB, :]
        r = lax.rsqrt(jnp.mean(xn * xn, axis=-1, keepdims=True) + EPS)
        y_ref[bb] = (xn * r) * gfinal_ref[...]


def _const_spec(shape):
    nd = len(shape)
    return pl.BlockSpec(shape, lambda b, t: (0,) * nd, pipeline_mode=pl.Buffered(1))


def _run_layer(x, mod, mod_row0, state, weights, *, BB, TB, start):
    B, T, D = x.shape
    W, H, DH = W_POOL, N_HEADS, HEAD_DIM
    R = BB * TB
    has_state = state is not None
    grid = (B // BB, T // TB)
    assert mod_row0 % BB == 0
    mod_blk0 = mod_row0 // BB

    in_specs = [
        pl.BlockSpec((BB, TB, D), lambda b, t: (b, t, 0)),
        pl.BlockSpec((BB, 1, 3 * D), lambda b, t: (b + mod_blk0, 0, 0)),
    ]
    args = [x, mod]
    if has_state:
        pool0, C0, n0, m0 = state
        in_specs += [
            pl.BlockSpec((BB, POOL_BUF, W), lambda b, t: (b, 0, 0)),
            pl.BlockSpec((BB, H, DH, DH), lambda b, t: (b, 0, 0, 0)),
            pl.BlockSpec((BB, H, DH), lambda b, t: (b, 0, 0)),
            pl.BlockSpec((BB, 1, LANES), lambda b, t: (b, 0, 0)),
        ]
        m0p = jnp.pad(m0, ((0, 0), (0, LANES - H))).reshape(B, 1, LANES)
        args += [pool0, C0, n0, m0p]
    in_specs += [_const_spec(w.shape) for w in weights]
    args += list(weights)

    out_shape = (
        jax.ShapeDtypeStruct((B, T, D), F32),
        jax.ShapeDtypeStruct((B, POOL_BUF, W), F32),
        jax.ShapeDtypeStruct((B, H, DH, DH), F32),
        jax.ShapeDtypeStruct((B, H, DH), F32),
        jax.ShapeDtypeStruct((B, 1, LANES), F32),
    )
    out_specs = (
        pl.BlockSpec((BB, TB, D), lambda b, t: (b, t, 0)),
        pl.BlockSpec((BB, POOL_BUF, W), lambda b, t: (b, 0, 0)),
        pl.BlockSpec((BB, H, DH, DH), lambda b, t: (b, 0, 0, 0)),
        pl.BlockSpec((BB, H, DH), lambda b, t: (b, 0, 0)),
        pl.BlockSpec((BB, 1, LANES), lambda b, t: (b, 0, 0)),
    )
    scratch = [
        pltpu.VMEM((R, D), BF16),
        pltpu.VMEM((BB, HIST + TB, W), F32),
        pltpu.VMEM((R, W), BF16),
        pltpu.VMEM((R, W), F32),
        pltpu.VMEM((R, W), F32),
        pltpu.VMEM((R, W), BF16),
        pltpu.VMEM((R, W), BF16),
        pltpu.VMEM((R, W), BF16),
        pltpu.VMEM((R, 2 * W), BF16),
    ]
    y, pool, C, n, m = pl.pallas_call(
        functools.partial(_layer_kernel, BB=BB, TB=TB, start=start, has_state=has_state),
        grid=grid,
        in_specs=in_specs,
        out_specs=out_specs,
        out_shape=out_shape,
        scratch_shapes=scratch,
        compiler_params=pltpu.CompilerParams(
            dimension_semantics=("parallel", "arbitrary"),
            vmem_limit_bytes=VMEM_LIMIT_BYTES),
        name="layer_state" if has_state else "layer_fresh",
    )(*args)
    return y, pool, C, n, m[:, 0, :H]


def kernel(x_prompt, x_sample, c_prompt, c_sample, state_pool, state_C, state_n, state_m,
           w_ada, b_ada, g_norm, w_in, b_i, b_f, w_pool, pool_scale, g_head, w_out, g_final):
    depth = w_ada.shape[0]
    assert depth == 1, "single-layer trunk"
    l = 0
    nbp = x_prompt.shape[0]
    n_main = N_MAIN_SECTIONS * W_POOL

    mod = _adaln_mod(jnp.concatenate([c_prompt, c_sample], axis=0), w_ada[l], b_ada[l])

    w_in_t = jnp.swapaxes(w_in[l], 0, 1)
    n_gate = 2 * N_HEADS
    assert n_main % n_gate == 0 and w_in_t.shape[0] == n_main + n_gate
    w_gate_t = _row_block(w_in_t, n_main // n_gate, n_gate)
    w_gate = jnp.pad(w_gate_t.T, ((0, 0), (0, LANES - n_gate))).astype(BF16)
    gate_bias = jnp.pad(jnp.concatenate([b_i[l], b_f[l]]), (0, LANES - 2 * N_HEADS)).reshape(1, LANES)
    weights = (
        g_norm[l].reshape(1, D_MODEL),
        _transposed_rows_bf16(w_in_t, n_main, W_POOL),
        w_gate,
        gate_bias,
        w_pool[l].astype(BF16),
        pool_scale[l].reshape(1, W_POOL),
        g_head[l].reshape(1, W_MLSTM),
        w_out[l].astype(BF16),
        g_final.reshape(1, D_MODEL),
    )

    yp, pp, pc, pn, pm = _run_layer(x_prompt, mod, 0, None, weights, BB=2, TB=256, start=0)
    ys, sp, sc, sn, sm = _run_layer(
        x_sample, mod, nbp, (state_pool[l], state_C[l], state_n[l], state_m[l]), weights,
        BB=4, TB=x_sample.shape[1], start=PAST_LEN)
    return (yp, ys, pp[None], pc[None], pn[None], pm[None], sp[None], sc[None], sn[None], sm[None])
```

```python
import functools

import jax
import jax.numpy as jnp
from jax import lax
from jax.experimental import pallas as pl
from jax.experimental.pallas import tpu as pltpu

F32 = jnp.float32
BF16 = jnp.bfloat16

D_MODEL = 1024
W_POOL = 1024
W_MLSTM = 1024
POOL_WINDOWS = (2, 4, 8, 16)
N_POOL_GROUPS = 4
POOL_GW = W_POOL // N_POOL_GROUPS
POOL_BUF = 15
N_HEADS = 4
HEAD_DIM = W_MLSTM // N_HEADS
EPS = 1e-6
N_MAIN_SECTIONS = 7
LANES = 128
PAST_LEN = 2048
HIST = 16
VMEM_LIMIT_BYTES = 56 * 1024 * 1024


def _sigmoid(z):
    return 0.5 * jnp.tanh(0.5 * z) + 0.5


def _silu(z):
    hz = 0.5 * z
    return hz * jnp.tanh(hz) + hz


def _log_sigmoid(z):
    return jnp.minimum(z, 0.0) - jnp.log1p(jnp.exp(-jnp.abs(z)))


def _mod_kernel(c_ref, w_ref, b_ref, o_ref):
    c = c_ref[...]
    mod = jnp.dot(_silu(c), w_ref[...], preferred_element_type=F32) + b_ref[...]
    for b in range(mod.shape[0]):
        o_ref[b] = mod[b:b + 1, :]


def _adaln_mod(c, w_ada, b_ada):
    nb = c.shape[0]
    n_out = w_ada.shape[1]
    blk = D_MODEL
    return pl.pallas_call(
        _mod_kernel,
        grid=(n_out // blk,),
        in_specs=[
            pl.BlockSpec((nb, D_MODEL), lambda j: (0, 0)),
            pl.BlockSpec((D_MODEL, blk), lambda j: (0, j)),
            pl.BlockSpec((1, blk), lambda j: (0, j)),
        ],
        out_specs=pl.BlockSpec((nb, 1, blk), lambda j: (0, 0, j)),
        out_shape=jax.ShapeDtypeStruct((nb, 1, n_out), F32),
        name="adaln_mod",
    )(c, w_ada, b_ada.reshape(1, n_out))


def _cast_kernel(i_ref, o_ref):
    o_ref[...] = i_ref[...].T.astype(o_ref.dtype)


def _transposed_rows_bf16(wt, n_rows, blk):
    k = wt.shape[1]
    return pl.pallas_call(
        _cast_kernel,
        grid=(n_rows // blk,),
        in_specs=[pl.BlockSpec((blk, k), lambda j: (j, 0))],
        out_specs=pl.BlockSpec((k, blk), lambda j: (0, j)),
        out_shape=jax.ShapeDtypeStruct((k, n_rows), BF16),
        name="cast_bf16",
    )(wt)


def _copy_kernel(i_ref, o_ref):
    o_ref[...] = i_ref[...]


def _row_block(wt, blk_index, blk):
    k = wt.shape[1]
    return pl.pallas_call(
        _copy_kernel,
        grid=(1,),
        in_specs=[pl.BlockSpec((blk, k), lambda j: (blk_index, 0))],
        out_specs=pl.BlockSpec((blk, k), lambda j: (0, 0)),
        out_shape=jax.ShapeDtypeStruct((blk, k), wt.dtype),
        name="row_block",
    )(wt)


def _seg_scan(x, tl, seg, combine, fill):
    k = 1
    while k < seg:
        shifted = pltpu.roll(x, k, 0)
        x = combine(x, jnp.where(tl >= k, shifted, fill))
        k *= 2
    return x


def _rows_of(per_batch_rows, TB):
    parts = [jnp.broadcast_to(r, (TB, LANES)) for r in per_batch_rows]
    return parts[0] if len(parts) == 1 else jnp.concatenate(parts, axis=0)


def _layer_kernel(*refs, BB, TB, NT, start, has_state, defer_out):
    R = BB * TB
    D, W, DH, GW = D_MODEL, W_POOL, HEAD_DIM, POOL_GW
    it = iter(refs)
    x_ref = next(it)
    xprev_ref = next(it) if defer_out else x_ref
    mod_ref = next(it)
    if has_state:
        pool0_ref, C0_ref, n0_ref, m0_ref = next(it), next(it), next(it), next(it)
    gnorm_ref, win_ref, wg_ref, gbias_ref = next(it), next(it), next(it), next(it)
    wpool_ref, pscale_ref, ghead_ref, wout_ref, gfinal_ref = (next(it), next(it), next(it),
                                                             next(it), next(it))
    y_ref, pool_out_ref, C_ref, n_ref, m_ref = next(it), next(it), next(it), next(it), next(it)
    h_s, ext_s, pooled_s, zp_s, hm_s, q_s, k_s, v_s, ycat_s = it

    t = pl.program_id(1)

    @pl.when(t == 0)
    def _init():
        if has_state:
            ext_s[:, 0:1, :] = jnp.zeros((BB, 1, W), F32)
            ext_s[:, 1:HIST, :] = pool0_ref[...]
            C_ref[...] = C0_ref[...]
            n_ref[...] = n0_ref[...]
            m_ref[...] = m0_ref[...]
        else:
            ext_s[:, 0:HIST, :] = jnp.zeros((BB, HIST, W), F32)
            C_ref[...] = jnp.zeros_like(C_ref)
            n_ref[...] = jnp.zeros_like(n_ref)
            m_ref[...] = jnp.zeros_like(m_ref)

    def norm_stage():
        for bb in range(BB):
            x = x_ref[bb]
            r = lax.rsqrt(jnp.mean(x * x, axis=-1, keepdims=True) + EPS)
            shift = mod_ref[bb, :, 0:D]
            scale = mod_ref[bb, :, D:2 * D]
            h = (x * r) * gnorm_ref[...] * (1.0 + scale) + shift
            h_s[bb * TB:(bb + 1) * TB, :] = h.astype(BF16)

    def main():
        def proj(sec):
            return jnp.dot(h_s[...], win_ref[:, sec * W:(sec + 1) * W], preferred_element_type=F32)

        gates = jnp.dot(h_s[...], wg_ref[...], preferred_element_type=F32) + gbias_ref[...]

        tl = lax.broadcasted_iota(jnp.int32, (R, LANES), 0) & (TB - 1)
        ig = gates
        lf = _log_sigmoid(pltpu.roll(gates, LANES - N_HEADS, 1))
        Fc = _seg_scan(lf, tl, TB, jnp.add, 0.0)
        a = ig - Fc
        cmax = _seg_scan(a, tl, TB, jnp.maximum, -jnp.inf)
        m0_rows = [m_ref[bb] for bb in range(BB)]
        m0 = _rows_of(m0_rows, TB)
        m = Fc + jnp.maximum(m0, cmax)
        Fl_rows = [Fc[(bb + 1) * TB - 1:(bb + 1) * TB, :] for bb in range(BB)]
        mL_rows = [m[(bb + 1) * TB - 1:(bb + 1) * TB, :] for bb in range(BB)]
        Fl = _rows_of(Fl_rows, TB)
        mL = _rows_of(mL_rows, TB)
        decay0 = jnp.exp(m0 + Fc - m)
        wL = jnp.exp(a + Fl - mL)
        emm = jnp.exp(-m)
        Fm = Fc - m
        aT = a.T

        xp = proj(0)
        for bb in range(BB):
            ext_s[bb, HIST:HIST + TB, :] = xp[bb * TB:(bb + 1) * TB, :]
        zp_s[...] = _silu(proj(1))
        q_s[...] = proj(2).astype(BF16)
        k_s[...] = (proj(3) * (DH ** -0.5)).astype(BF16)
        v_s[...] = proj(4).astype(BF16)

        pos_head = start + t * TB + lax.broadcasted_iota(jnp.int32, (HIST, 1), 0)
        for g, w in enumerate(POOL_WINDOWS):
            cols = slice(g * GW, (g + 1) * GW)
            inv_head = 1.0 / jnp.minimum(pos_head + 1, w).astype(F32)
            for bb in range(BB):
                ext = ext_s[bb, :, cols]
                win = ext
                s = 1
                while s < w:
                    win = win + pltpu.roll(win, s, 0)
                    s *= 2
                r0 = bb * TB
                head = win[HIST:2 * HIST, :] * inv_head - ext[HIST:2 * HIST, :]
                tail = win[2 * HIST:, :] * (1.0 / w) - ext[2 * HIST:, :]
                pooled_s[r0:r0 + HIST, cols] = head.astype(BF16)
                pooled_s[r0 + HIST:r0 + TB, cols] = tail.astype(BF16)
        for g in range(N_POOL_GROUPS):
            cols = slice(g * GW, (g + 1) * GW)
            mixed = jnp.dot(pooled_s[:, cols], wpool_ref[g], preferred_element_type=F32)
            ycat_s[:, cols] = (mixed * pscale_ref[:, cols] * zp_s[:, cols]).astype(BF16)

        pool_out_ref[...] = ext_s[:, TB + 1:TB + HIST, :]

        for bb in range(BB):
            ext_s[bb, 0:HIST, :] = ext_s[bb, TB:TB + HIST, :]

        causal = (lax.broadcasted_iota(jnp.int32, (TB, TB), 0)
                  >= lax.broadcasted_iota(jnp.int32, (TB, TB), 1))

        for bb in range(BB):
            rows = slice(bb * TB, (bb + 1) * TB)
            dL = jnp.exp(m0_rows[bb] + Fl_rows[bb] - mL_rows[bb])
            for hd in range(N_HEADS):
                cols = slice(hd * DH, (hd + 1) * DH)
                q = q_s[rows, cols]
                k = k_s[rows, cols]
                v = v_s[rows, cols]
                s = lax.dot_general(q, k, (((1,), (1,)), ((), ())), preferred_element_type=F32)
                logD = Fm[rows, hd:hd + 1] + aT[hd:hd + 1, rows]
                S = s * jnp.exp(jnp.where(causal, logD, -jnp.inf))
                C0 = C_ref[bb, hd]
                n0 = n_ref[bb, hd:hd + 1, :]
                d0 = decay0[rows, hd:hd + 1]
                qn = jnp.sum(q.astype(F32) * n0, axis=-1, keepdims=True)
                nq = jnp.sum(S, axis=-1, keepdims=True) + d0 * qn
                num = (jnp.dot(S.astype(BF16), v, preferred_element_type=F32)
                       + d0 * jnp.dot(q, C0.astype(BF16), preferred_element_type=F32))
                den = jnp.maximum(jnp.abs(nq), emm[rows, hd:hd + 1])
                hh = num * (1.0 / den)
                hc = hh - jnp.mean(hh, axis=-1, keepdims=True)
                hn = hc * lax.rsqrt(jnp.mean(hc * hc, axis=-1, keepdims=True) + EPS)
                hm_s[rows, cols] = hn * ghead_ref[:, cols]
                kw = k.astype(F32) * wL[rows, hd:hd + 1]
                dl = dL[:, hd:hd + 1]
                C_ref[bb, hd] = dl * C0 + lax.dot_general(
                    kw.astype(BF16), v, (((0,), (0,)), ((), ())), preferred_element_type=F32)
                n_ref[bb, hd:hd + 1, :] = dl * n0 + jnp.sum(kw, axis=0, keepdims=True)
            m_ref[bb] = mL_rows[bb]

        hm_s[...] = hm_s[...] * _sigmoid(proj(5))
        ycat_s[:, W:2 * W] = (hm_s[...] * _silu(proj(6))).astype(BF16)


    def out_stage():
        y = jnp.dot(ycat_s[...], wout_ref[...], preferred_element_type=F32)
        for bb in range(BB):
            gate = mod_ref[bb, :, 2 * D:3 * D]
            xn = xprev_ref[bb] + gate * y[bb * TB:(bb + 1) * TB, :]
            r = lax.rsqrt(jnp.mean(xn * xn, axis=-1, keepdims=True) + EPS)
            y_ref[bb] = (xn * r) * gfinal_ref[...]

    if not defer_out:
        norm_stage()
        main()
        out_stage()
        return

    @pl.when((pl.program_id(0) == 0) & (t == 0))
    def _clear():
        ycat_s[...] = jnp.zeros_like(ycat_s)

    @pl.when(t < NT)
    def _steady():
        norm_stage()
        out_stage()
        main()

    pl.when(t == NT)(out_stage)


def _const_spec(shape):
    nd = len(shape)
    return pl.BlockSpec(shape, lambda b, t: (0,) * nd, pipeline_mode=pl.Buffered(1))


def _run_layer(x, mod, mod_row0, state, weights, *, BB, TB, start):
    B, T, D = x.shape
    W, H, DH = W_POOL, N_HEADS, HEAD_DIM
    R = BB * TB
    has_state = state is not None
    NT = T // TB
    defer_out = NT > 1
    assert mod_row0 % BB == 0
    mod_blk0 = mod_row0 // BB

    if defer_out:
        grid = (B // BB, NT + 1)
        cur_map = lambda b, t: (b, jnp.minimum(t, NT - 1), 0)
        out_map = lambda b, t: (b, jnp.maximum(t - 1, 0), 0)
        in_specs = [pl.BlockSpec((BB, TB, D), cur_map), pl.BlockSpec((BB, TB, D), out_map)]
        args = [x, x]
    else:
        grid = (B // BB, NT)
        out_map = lambda b, t: (b, t, 0)
        in_specs = [pl.BlockSpec((BB, TB, D), out_map)]
        args = [x]
    in_specs.append(pl.BlockSpec((BB, 1, 3 * D), lambda b, t: (b + mod_blk0, 0, 0)))
    args.append(mod)
    if has_state:
        pool0, C0, n0, m0 = state
        in_specs += [
            pl.BlockSpec((BB, POOL_BUF, W), lambda b, t: (b, 0, 0)),
            pl.BlockSpec((BB, H, DH, DH), lambda b, t: (b, 0, 0, 0)),
            pl.BlockSpec((BB, H, DH), lambda b, t: (b, 0, 0)),
            pl.BlockSpec((BB, 1, LANES), lambda b, t: (b, 0, 0)),
        ]
        m0p = jnp.pad(m0, ((0, 0), (0, LANES - H))).reshape(B, 1, LANES)
        args += [pool0, C0, n0, m0p]
    in_specs += [_const_spec(w.shape) for w in weights]
    args += list(weights)

    out_shape = (
        jax.ShapeDtypeStruct((B, T, D), F32),
        jax.ShapeDtypeStruct((B, POOL_BUF, W), F32),
        jax.ShapeDtypeStruct((B, H, DH, DH), F32),
        jax.ShapeDtypeStruct((B, H, DH), F32),
        jax.ShapeDtypeStruct((B, 1, LANES), F32),
    )
    out_specs = (
        pl.BlockSpec((BB, TB, D), out_map),
        pl.BlockSpec((BB, POOL_BUF, W), lambda b, t: (b, 0, 0)),
        pl.BlockSpec((BB, H, DH, DH), lambda b, t: (b, 0, 0, 0)),
        pl.BlockSpec((BB, H, DH), lambda b, t: (b, 0, 0)),
        pl.BlockSpec((BB, 1, LANES), lambda b, t: (b, 0, 0)),
    )
    scratch = [
        pltpu.VMEM((R, D), BF16),
        pltpu.VMEM((BB, HIST + TB, W), F32),
        pltpu.VMEM((R, W), BF16),
        pltpu.VMEM((R, W), F32),
        pltpu.VMEM((R, W), F32),
        pltpu.VMEM((R, W), BF16),
        pltpu.VMEM((R, W), BF16),
        pltpu.VMEM((R, W), BF16),
        pltpu.VMEM((R, 2 * W), BF16),
    ]
    y, pool, C, n, m = pl.pallas_call(
        functools.partial(_layer_kernel, BB=BB, TB=TB, NT=NT, start=start,
                          has_state=has_state, defer_out=defer_out),
        grid=grid,
        in_specs=in_specs,
        out_specs=out_specs,
        out_shape=out_shape,
        scratch_shapes=scratch,
        compiler_params=pltpu.CompilerParams(
            dimension_semantics=("parallel", "arbitrary"),
            vmem_limit_bytes=VMEM_LIMIT_BYTES),
        name="layer_state" if has_state else "layer_fresh",
    )(*args)
    return y, pool, C, n, m[:, 0, :H]


def kernel(x_prompt, x_sample, c_prompt, c_sample, state_pool, state_C, state_n, state_m,
           w_ada, b_ada, g_norm, w_in, b_i, b_f, w_pool, pool_scale, g_head, w_out, g_final):
    depth = w_ada.shape[0]
    assert depth == 1, "single-layer trunk"
    l = 0
    nbp = x_prompt.shape[0]
    n_main = N_MAIN_SECTIONS * W_POOL

    mod = _adaln_mod(jnp.concatenate([c_prompt, c_sample], axis=0), w_ada[l], b_ada[l])

    w_in_t = jnp.swapaxes(w_in[l], 0, 1)
    n_gate = 2 * N_HEADS
    assert n_main % n_gate == 0 and w_in_t.shape[0] == n_main + n_gate
    w_gate_t = _row_block(w_in_t, n_main // n_gate, n_gate)
    w_gate = jnp.pad(w_gate_t.T, ((0, 0), (0, LANES - n_gate))).astype(BF16)
    gate_bias = jnp.pad(jnp.concatenate([b_i[l], b_f[l]]), (0, LANES - 2 * N_HEADS)).reshape(1, LANES)
    weights = (
        g_norm[l].reshape(1, D_MODEL),
        _transposed_rows_bf16(w_in_t, n_main, W_POOL),
        w_gate,
        gate_bias,
        w_pool[l].astype(BF16),
        pool_scale[l].reshape(1, W_POOL),
        g_head[l].reshape(1, W_MLSTM),
        w_out[l].astype(BF16),
        g_final.reshape(1, D_MODEL),
    )

    yp, pp, pc, pn, pm = _run_layer(x_prompt, mod, 0, None, weights, BB=1, TB=256, start=0)
    ys, sp, sc, sn, sm = _run_layer(
        x_sample, mod, nbp, (state_pool[l], state_C[l], state_n[l], state_m[l]), weights,
        BB=4, TB=x_sample.shape[1], start=PAST_LEN)
    return (yp, ys, pp[None], pc[None], pn[None], pm[None], sp[None], sc[None], sn[None], sm[None])
```

```python
import functools

import jax
import jax.numpy as jnp
from jax import lax
from jax.experimental import pallas as pl
from jax.experimental.pallas import tpu as pltpu

F32 = jnp.float32
BF16 = jnp.bfloat16

D_MODEL = 1024
W_POOL = 1024
W_MLSTM = 1024
POOL_WINDOWS = (2, 4, 8, 16)
N_POOL_GROUPS = 4
POOL_GW = W_POOL // N_POOL_GROUPS
POOL_BUF = 15
N_HEADS = 4
HEAD_DIM = W_MLSTM // N_HEADS
EPS = 1e-6
N_MAIN_SECTIONS = 7
LANES = 128
PAST_LEN = 2048
HIST = 16
VMEM_LIMIT_BYTES = 56 * 1024 * 1024


def _sigmoid(z):
    return 0.5 * jnp.tanh(0.5 * z) + 0.5


def _silu(z):
    hz = 0.5 * z
    return hz * jnp.tanh(hz) + hz


def _log_sigmoid(z):
    return jnp.minimum(z, 0.0) - jnp.log1p(jnp.exp(-jnp.abs(z)))


def _mod_kernel(c_ref, w_ref, b_ref, o_ref):
    c = c_ref[...]
    mod = jnp.dot(_silu(c), w_ref[...], preferred_element_type=F32) + b_ref[...]
    for b in range(mod.shape[0]):
        o_ref[b] = mod[b:b + 1, :]


def _adaln_mod(c, w_ada, b_ada):
    nb = c.shape[0]
    n_out = w_ada.shape[1]
    blk = D_MODEL
    return pl.pallas_call(
        _mod_kernel,
        grid=(n_out // blk,),
        in_specs=[
            pl.BlockSpec((nb, D_MODEL), lambda j: (0, 0)),
            pl.BlockSpec((D_MODEL, blk), lambda j: (0, j)),
            pl.BlockSpec((1, blk), lambda j: (0, j)),
        ],
        out_specs=pl.BlockSpec((nb, 1, blk), lambda j: (0, 0, j)),
        out_shape=jax.ShapeDtypeStruct((nb, 1, n_out), F32),
        name="adaln_mod",
    )(c, w_ada, b_ada.reshape(1, n_out))


def _cast_kernel(i_ref, o_ref):
    o_ref[...] = i_ref[...].T.astype(o_ref.dtype)


def _transposed_rows_bf16(wt, n_rows, blk):
    k = wt.shape[1]
    return pl.pallas_call(
        _cast_kernel,
        grid=(n_rows // blk,),
        in_specs=[pl.BlockSpec((blk, k), lambda j: (j, 0))],
        out_specs=pl.BlockSpec((k, blk), lambda j: (0, j)),
        out_shape=jax.ShapeDtypeStruct((k, n_rows), BF16),
        name="cast_bf16",
    )(wt)


def _copy_kernel(i_ref, o_ref):
    o_ref[...] = i_ref[...]


def _row_block(wt, blk_index, blk):
    k = wt.shape[1]
    return pl.pallas_call(
        _copy_kernel,
        grid=(1,),
        in_specs=[pl.BlockSpec((blk, k), lambda j: (blk_index, 0))],
        out_specs=pl.BlockSpec((blk, k), lambda j: (0, 0)),
        out_shape=jax.ShapeDtypeStruct((blk, k), wt.dtype),
        name="row_block",
    )(wt)


def _seg_scan(x, tl, seg, combine, fill):
    k = 1
    while k < seg:
        shifted = pltpu.roll(x, k, 0)
        x = combine(x, jnp.where(tl >= k, shifted, fill))
        k *= 2
    return x


def _rows_of(per_batch_rows, TB):
    parts = [jnp.broadcast_to(r, (TB, LANES)) for r in per_batch_rows]
    return parts[0] if len(parts) == 1 else jnp.concatenate(parts, axis=0)


def _layer_kernel(*refs, BB, TB, NT, start, has_state, defer_out):
    R = BB * TB
    D, W, DH, GW = D_MODEL, W_POOL, HEAD_DIM, POOL_GW
    it = iter(refs)
    x_ref = next(it)
    xprev_ref = next(it) if defer_out else x_ref
    mod_ref = next(it)
    if has_state:
        pool0_ref, C0_ref, n0_ref, m0_ref = next(it), next(it), next(it), next(it)
    gnorm_ref, win_ref, wg_ref, gbias_ref = next(it), next(it), next(it), next(it)
    wpool_ref, pscale_ref, ghead_ref, wout_ref, gfinal_ref = (next(it), next(it), next(it),
                                                             next(it), next(it))
    y_ref, pool_out_ref, C_ref, n_ref, m_ref = next(it), next(it), next(it), next(it), next(it)
    h_s, ext_s, pooled_s, zp_s, hm_s, q_s, k_s, v_s, ycat_s = it

    t = pl.program_id(1)

    @pl.when(t == 0)
    def _init():
        if has_state:
            ext_s[:, 0:1, :] = jnp.zeros((BB, 1, W), F32)
            ext_s[:, 1:HIST, :] = pool0_ref[...]
            C_ref[...] = C0_ref[...]
            n_ref[...] = n0_ref[...]
            m_ref[...] = m0_ref[...]
        else:
            ext_s[:, 0:HIST, :] = jnp.zeros((BB, HIST, W), F32)
            C_ref[...] = jnp.zeros_like(C_ref)
            n_ref[...] = jnp.zeros_like(n_ref)
            m_ref[...] = jnp.zeros_like(m_ref)

    def norm_stage():
        for bb in range(BB):
            x = x_ref[bb]
            r = lax.rsqrt(jnp.mean(x * x, axis=-1, keepdims=True) + EPS)
            shift = mod_ref[bb, :, 0:D]
            scale = mod_ref[bb, :, D:2 * D]
            h = (x * r) * gnorm_ref[...] * (1.0 + scale) + shift
            h_s[bb * TB:(bb + 1) * TB, :] = h.astype(BF16)

    def main():
        def proj(sec):
            return jnp.dot(h_s[...], win_ref[:, sec * W:(sec + 1) * W], preferred_element_type=F32)

        gates = jnp.dot(h_s[...], wg_ref[...], preferred_element_type=F32) + gbias_ref[...]

        tl = lax.broadcasted_iota(jnp.int32, (R, LANES), 0) & (TB - 1)
        ig = gates
        lf = _log_sigmoid(pltpu.roll(gates, LANES - N_HEADS, 1))
        Fc = _seg_scan(lf, tl, TB, jnp.add, 0.0)
        a = ig - Fc
        cmax = _seg_scan(a, tl, TB, jnp.maximum, -jnp.inf)
        m0_rows = [m_ref[bb] for bb in range(BB)]
        m0 = _rows_of(m0_rows, TB)
        m = Fc + jnp.maximum(m0, cmax)
        Fl_rows = [Fc[(bb + 1) * TB - 1:(bb + 1) * TB, :] for bb in range(BB)]
        mL_rows = [m[(bb + 1) * TB - 1:(bb + 1) * TB, :] for bb in range(BB)]
        Fl = _rows_of(Fl_rows, TB)
        mL = _rows_of(mL_rows, TB)
        decay0 = jnp.exp(m0 + Fc - m)
        wL = jnp.exp(a + Fl - mL)
        emm = jnp.exp(-m)
        Fm = Fc - m
        aT = a.T

        xp = proj(0)
        for bb in range(BB):
            ext_s[bb, HIST:HIST + TB, :] = xp[bb * TB:(bb + 1) * TB, :]
        zp_s[...] = _silu(proj(1))
        q_s[...] = proj(2).astype(BF16)
        k_s[...] = (proj(3) * (DH ** -0.5)).astype(BF16)
        v_s[...] = proj(4).astype(BF16)

        pos_head = start + t * TB + lax.broadcasted_iota(jnp.int32, (HIST, 1), 0)
        for g, w in enumerate(POOL_WINDOWS):
            cols = slice(g * GW, (g + 1) * GW)
            inv_head = 1.0 / jnp.minimum(pos_head + 1, w).astype(F32)
            for bb in range(BB):
                ext = ext_s[bb, :, cols]
                win = ext
                s = 1
                while s < w:
                    win = win + pltpu.roll(win, s, 0)
                    s *= 2
                r0 = bb * TB
                head = win[HIST:2 * HIST, :] * inv_head - ext[HIST:2 * HIST, :]
                tail = win[2 * HIST:, :] * (1.0 / w) - ext[2 * HIST:, :]
                pooled_s[r0:r0 + HIST, cols] = head.astype(BF16)
                pooled_s[r0 + HIST:r0 + TB, cols] = tail.astype(BF16)
        for g in range(N_POOL_GROUPS):
            cols = slice(g * GW, (g + 1) * GW)
            mixed = jnp.dot(pooled_s[:, cols], wpool_ref[g], preferred_element_type=F32)
            ycat_s[:, cols] = (mixed * pscale_ref[:, cols] * zp_s[:, cols]).astype(BF16)

        pool_out_ref[...] = ext_s[:, TB + 1:TB + HIST, :]

        for bb in range(BB):
            ext_s[bb, 0:HIST, :] = ext_s[bb, TB:TB + HIST, :]

        causal = (lax.broadcasted_iota(jnp.int32, (TB, TB), 0)
                  >= lax.broadcasted_iota(jnp.int32, (TB, TB), 1))

        for bb in range(BB):
            rows = slice(bb * TB, (bb + 1) * TB)
            dL = jnp.exp(m0_rows[bb] + Fl_rows[bb] - mL_rows[bb])
            for hd in range(N_HEADS):
                cols = slice(hd * DH, (hd + 1) * DH)
                q = q_s[rows, cols]
                k = k_s[rows, cols]
                v = v_s[rows, cols]
                s = lax.dot_general(q, k, (((1,), (1,)), ((), ())), preferred_element_type=F32)
                logD = Fm[rows, hd:hd + 1] + aT[hd:hd + 1, rows]
                S = s * jnp.exp(jnp.where(causal, logD, -jnp.inf))
                C0 = C_ref[bb, hd]
                n0 = n_ref[bb, hd:hd + 1, :]
                d0 = decay0[rows, hd:hd + 1]
                qn = jnp.sum(q.astype(F32) * n0, axis=-1, keepdims=True)
                nq = jnp.sum(S, axis=-1, keepdims=True) + d0 * qn
                num = (jnp.dot(S.astype(BF16), v, preferred_element_type=F32)
                       + d0 * jnp.dot(q, C0.astype(BF16), preferred_element_type=F32))
                den = jnp.maximum(jnp.abs(nq), emm[rows, hd:hd + 1])
                hh = num * (1.0 / den)
                hc = hh - jnp.mean(hh, axis=-1, keepdims=True)
                hn = hc * lax.rsqrt(jnp.mean(hc * hc, axis=-1, keepdims=True) + EPS)
                hm_s[rows, cols] = hn * ghead_ref[:, cols]
                kw = k.astype(F32) * wL[rows, hd:hd + 1]
                dl = dL[:, hd:hd + 1]
                C_ref[bb, hd] = dl * C0 + lax.dot_general(
                    kw.astype(BF16), v, (((0,), (0,)), ((), ())), preferred_element_type=F32)
                n_ref[bb, hd:hd + 1, :] = dl * n0 + jnp.sum(kw, axis=0, keepdims=True)
            m_ref[bb] = mL_rows[bb]

        hm_s[...] = hm_s[...] * _sigmoid(proj(5))
        ycat_s[:, W:2 * W] = (hm_s[...] * _silu(proj(6))).astype(BF16)


    def out_stage():
        y = jnp.dot(ycat_s[...], wout_ref[...], preferred_element_type=F32)
        for bb in range(BB):
            gate = mod_ref[bb, :, 2 * D:3 * D]
            xn = xprev_ref[bb] + gate * y[bb * TB:(bb + 1) * TB, :]
            r = lax.rsqrt(jnp.mean(xn * xn, axis=-1, keepdims=True) + EPS)
            y_ref[bb] = (xn * r) * gfinal_ref[...]

    if not defer_out:
        @pl.when(t >= 0)
        def _body():
            norm_stage()
            main()
            out_stage()
        return

    @pl.when((pl.program_id(0) == 0) & (t == 0))
    def _clear():
        ycat_s[...] = jnp.zeros_like(ycat_s)

    @pl.when(t < NT)
    def _steady():
        norm_stage()
        out_stage()
        main()

    pl.when(t == NT)(out_stage)


def _const_spec(shape):
    nd = len(shape)
    return pl.BlockSpec(shape, lambda b, t: (0,) * nd, pipeline_mode=pl.Buffered(1))


def _run_layer(x, mod, mod_row0, state, weights, *, BB, TB, start):
    B, T, D = x.shape
    W, H, DH = W_POOL, N_HEADS, HEAD_DIM
    R = BB * TB
    has_state = state is not None
    NT = T // TB
    defer_out = False
    assert mod_row0 % BB == 0
    mod_blk0 = mod_row0 // BB

    if defer_out:
        grid = (B // BB, NT + 1)
        cur_map = lambda b, t: (b, jnp.minimum(t, NT - 1), 0)
        out_map = lambda b, t: (b, jnp.maximum(t - 1, 0), 0)
        in_specs = [pl.BlockSpec((BB, TB, D), cur_map), pl.BlockSpec((BB, TB, D), out_map)]
        args = [x, x]
    else:
        grid = (B // BB, NT)
        out_map = lambda b, t: (b, t, 0)
        in_specs = [pl.BlockSpec((BB, TB, D), out_map)]
        args = [x]
    in_specs.append(pl.BlockSpec((BB, 1, 3 * D), lambda b, t: (b + mod_blk0, 0, 0)))
    args.append(mod)
    if has_state:
        pool0, C0, n0, m0 = state
        in_specs += [
            pl.BlockSpec((BB, POOL_BUF, W), lambda b, t: (b, 0, 0)),
            pl.BlockSpec((BB, H, DH, DH), lambda b, t: (b, 0, 0, 0)),
            pl.BlockSpec((BB, H, DH), lambda b, t: (b, 0, 0)),
            pl.BlockSpec((BB, 1, LANES), lambda b, t: (b, 0, 0)),
        ]
        m0p = jnp.pad(m0, ((0, 0), (0, LANES - H))).reshape(B, 1, LANES)
        args += [pool0, C0, n0, m0p]
    in_specs += [_const_spec(w.shape) for w in weights]
    args += list(weights)

    out_shape = (
        jax.ShapeDtypeStruct((B, T, D), F32),
        jax.ShapeDtypeStruct((B, POOL_BUF, W), F32),
        jax.ShapeDtypeStruct((B, H, DH, DH), F32),
        jax.ShapeDtypeStruct((B, H, DH), F32),
        jax.ShapeDtypeStruct((B, 1, LANES), F32),
    )
    out_specs = (
        pl.BlockSpec((BB, TB, D), out_map),
        pl.BlockSpec((BB, POOL_BUF, W), lambda b, t: (b, 0, 0)),
        pl.BlockSpec((BB, H, DH, DH), lambda b, t: (b, 0, 0, 0)),
        pl.BlockSpec((BB, H, DH), lambda b, t: (b, 0, 0)),
        pl.BlockSpec((BB, 1, LANES), lambda b, t: (b, 0, 0)),
    )
    scratch = [
        pltpu.VMEM((R, D), BF16),
        pltpu.VMEM((BB, HIST + TB, W), F32),
        pltpu.VMEM((R, W), BF16),
        pltpu.VMEM((R, W), F32),
        pltpu.VMEM((R, W), F32),
        pltpu.VMEM((R, W), BF16),
        pltpu.VMEM((R, W), BF16),
        pltpu.VMEM((R, W), BF16),
        pltpu.VMEM((R, 2 * W), BF16),
    ]
    y, pool, C, n, m = pl.pallas_call(
        functools.partial(_layer_kernel, BB=BB, TB=TB, NT=NT, start=start,
                          has_state=has_state, defer_out=defer_out),
        grid=grid,
        in_specs=in_specs,
        out_specs=out_specs,
        out_shape=out_shape,
        scratch_shapes=scratch,
        compiler_params=pltpu.CompilerParams(
            dimension_semantics=("parallel", "arbitrary"),
            vmem_limit_bytes=VMEM_LIMIT_BYTES),
        name="layer_state" if has_state else "layer_fresh",
    )(*args)
    return y, pool, C, n, m[:, 0, :H]


def kernel(x_prompt, x_sample, c_prompt, c_sample, state_pool, state_C, state_n, state_m,
           w_ada, b_ada, g_norm, w_in, b_i, b_f, w_pool, pool_scale, g_head, w_out, g_final):
    depth = w_ada.shape[0]
    assert depth == 1, "single-layer trunk"
    l = 0
    nbp = x_prompt.shape[0]
    n_main = N_MAIN_SECTIONS * W_POOL

    mod = _adaln_mod(jnp.concatenate([c_prompt, c_sample], axis=0), w_ada[l], b_ada[l])

    w_in_t = jnp.swapaxes(w_in[l], 0, 1)
    n_gate = 2 * N_HEADS
    assert n_main % n_gate == 0 and w_in_t.shape[0] == n_main + n_gate
    w_gate_t = _row_block(w_in_t, n_main // n_gate, n_gate)
    w_gate = jnp.pad(w_gate_t.T, ((0, 0), (0, LANES - n_gate))).astype(BF16)
    gate_bias = jnp.pad(jnp.concatenate([b_i[l], b_f[l]]), (0, LANES - 2 * N_HEADS)).reshape(1, LANES)
    weights = (
        g_norm[l].reshape(1, D_MODEL),
        _transposed_rows_bf16(w_in_t, n_main, W_POOL),
        w_gate,
        gate_bias,
        w_pool[l].astype(BF16),
        pool_scale[l].reshape(1, W_POOL),
        g_head[l].reshape(1, W_MLSTM),
        w_out[l].astype(BF16),
        g_final.reshape(1, D_MODEL),
    )

    yp, pp, pc, pn, pm = _run_layer(x_prompt, mod, 0, None, weights, BB=2, TB=256, start=0)
    ys, sp, sc, sn, sm = _run_layer(
        x_sample, mod, nbp, (state_pool[l], state_C[l], state_n[l], state_m[l]), weights,
        BB=4, TB=x_sample.shape[1], start=PAST_LEN)
    return (yp, ys, pp[None], pc[None], pn[None], pm[None], sp[None], sc[None], sn[None], sm[None])
```

```python
import functools

import jax
import jax.numpy as jnp
from jax import lax
from jax.experimental import pallas as pl
from jax.experimental.pallas import tpu as pltpu

F32 = jnp.float32
BF16 = jnp.bfloat16

D_MODEL = 1024
W_POOL = 1024
W_MLSTM = 1024
POOL_WINDOWS = (2, 4, 8, 16)
N_POOL_GROUPS = 4
POOL_GW = W_POOL // N_POOL_GROUPS
POOL_BUF = 15
N_HEADS = 4
HEAD_DIM = W_MLSTM // N_HEADS
EPS = 1e-6
N_MAIN_SECTIONS = 7
LANES = 128
PAST_LEN = 2048
HIST = 16
VMEM_LIMIT_BYTES = 56 * 1024 * 1024


def _sigmoid(z):
    return 0.5 * jnp.tanh(0.5 * z) + 0.5


def _silu(z):
    hz = 0.5 * z
    return hz * jnp.tanh(hz) + hz


def _log_sigmoid(z):
    return jnp.minimum(z, 0.0) - jnp.log1p(jnp.exp(-jnp.abs(z)))


def _mod_kernel(c_ref, w_ref, b_ref, o_ref):
    c = c_ref[...]
    mod = jnp.dot(_silu(c), w_ref[...], preferred_element_type=F32) + b_ref[...]
    for b in range(mod.shape[0]):
        o_ref[b] = mod[b:b + 1, :]


def _adaln_mod(c, w_ada, b_ada):
    nb = c.shape[0]
    n_out = w_ada.shape[1]
    blk = D_MODEL // 2
    return pl.pallas_call(
        _mod_kernel,
        grid=(n_out // blk,),
        in_specs=[
            pl.BlockSpec((nb, D_MODEL), lambda j: (0, 0)),
            pl.BlockSpec((D_MODEL, blk), lambda j: (0, j)),
            pl.BlockSpec((1, blk), lambda j: (0, j)),
        ],
        out_specs=pl.BlockSpec((nb, 1, blk), lambda j: (0, 0, j)),
        out_shape=jax.ShapeDtypeStruct((nb, 1, n_out), F32),
        name="adaln_mod",
    )(c, w_ada, b_ada.reshape(1, n_out))


def _cast_kernel(i_ref, o_ref):
    o_ref[...] = i_ref[...].T.astype(o_ref.dtype)


def _transposed_rows_bf16(wt, n_rows, blk):
    k = wt.shape[1]
    return pl.pallas_call(
        _cast_kernel,
        grid=(n_rows // blk,),
        in_specs=[pl.BlockSpec((blk, k), lambda j: (j, 0))],
        out_specs=pl.BlockSpec((k, blk), lambda j: (0, j)),
        out_shape=jax.ShapeDtypeStruct((k, n_rows), BF16),
        name="cast_bf16",
    )(wt)


def _plain_cast_kernel(i_ref, o_ref):
    o_ref[...] = i_ref[...].astype(o_ref.dtype)


def _rows_bf16(w, blk):
    n, k = w.shape
    return pl.pallas_call(
        _plain_cast_kernel,
        grid=(n // blk,),
        in_specs=[pl.BlockSpec((blk, k), lambda j: (j, 0))],
        out_specs=pl.BlockSpec((blk, k), lambda j: (j, 0)),
        out_shape=jax.ShapeDtypeStruct((n, k), BF16),
        name="cast_rows_bf16",
    )(w)


def _copy_kernel(i_ref, o_ref):
    o_ref[...] = i_ref[...]


def _row_block(wt, blk_index, blk):
    k = wt.shape[1]
    return pl.pallas_call(
        _copy_kernel,
        grid=(1,),
        in_specs=[pl.BlockSpec((blk, k), lambda j: (blk_index, 0))],
        out_specs=pl.BlockSpec((blk, k), lambda j: (0, 0)),
        out_shape=jax.ShapeDtypeStruct((blk, k), wt.dtype),
        name="row_block",
    )(wt)


def _seg_scan(x, tl, seg, combine, fill):
    k = 1
    while k < seg:
        shifted = pltpu.roll(x, k, 0)
        x = combine(x, jnp.where(tl >= k, shifted, fill))
        k *= 2
    return x


def _rows_of(per_batch_rows, TB):
    parts = [jnp.broadcast_to(r, (TB, LANES)) for r in per_batch_rows]
    return parts[0] if len(parts) == 1 else jnp.concatenate(parts, axis=0)


def _layer_kernel(*refs, BB, TB, start, has_state):
    R = BB * TB
    D, W, DH, GW = D_MODEL, W_POOL, HEAD_DIM, POOL_GW
    it = iter(refs)
    x_ref = next(it)
    mod_ref = next(it)
    if has_state:
        pool0_ref, C0_ref, n0_ref, m0_ref = next(it), next(it), next(it), next(it)
    gnorm_ref, win_ref, wg_ref, gbias_ref = next(it), next(it), next(it), next(it)
    wpool_ref, pscale_ref, ghead_ref, wout_ref, gfinal_ref = (next(it), next(it), next(it),
                                                             next(it), next(it))
    y_ref, pool_out_ref, C_ref, n_ref, m_ref = next(it), next(it), next(it), next(it), next(it)
    h_s, ext_s, pooled_s, zp_s, hm_s, q_s, k_s, v_s, ycat_s, gain_s = it

    t = pl.program_id(1)

    @pl.when(t == 0)
    def _init():
        if has_state:
            ext_s[:, 0:1, :] = jnp.zeros((BB, 1, W), F32)
            ext_s[:, 1:HIST, :] = pool0_ref[...]
            C_ref[...] = C0_ref[...]
            n_ref[...] = n0_ref[...]
            m_ref[...] = m0_ref[...]
        else:
            ext_s[:, 0:HIST, :] = jnp.zeros((BB, HIST, W), F32)
            C_ref[...] = jnp.zeros_like(C_ref)
            n_ref[...] = jnp.zeros_like(n_ref)
            m_ref[...] = jnp.zeros_like(m_ref)
        gain_s[...] = gnorm_ref[...] * (1.0 + mod_ref[:, :, D:2 * D])

    for bb in range(BB):
        x = x_ref[bb]
        r = lax.rsqrt(jnp.mean(x * x, axis=-1, keepdims=True) + EPS)
        shift = mod_ref[bb, :, 0:D]
        h = (x * r) * gain_s[bb] + shift
        h_s[bb * TB:(bb + 1) * TB, :] = h.astype(BF16)

    def proj(sec):
        return jnp.dot(h_s[...], win_ref[:, sec * W:(sec + 1) * W], preferred_element_type=F32)

    gates = jnp.dot(h_s[...], wg_ref[...], preferred_element_type=F32) + gbias_ref[...]

    tl = lax.broadcasted_iota(jnp.int32, (R, LANES), 0) & (TB - 1)
    ig = gates
    lf = _log_sigmoid(pltpu.roll(gates, LANES - N_HEADS, 1))
    Fc = _seg_scan(lf, tl, TB, jnp.add, 0.0)
    a = ig - Fc
    cmax = _seg_scan(a, tl, TB, jnp.maximum, -jnp.inf)
    m0_rows = [m_ref[bb] for bb in range(BB)]
    m0 = _rows_of(m0_rows, TB)
    m = Fc + jnp.maximum(m0, cmax)
    Fl_rows = [Fc[(bb + 1) * TB - 1:(bb + 1) * TB, :] for bb in range(BB)]
    mL_rows = [m[(bb + 1) * TB - 1:(bb + 1) * TB, :] for bb in range(BB)]
    Fl = _rows_of(Fl_rows, TB)
    mL = _rows_of(mL_rows, TB)
    decay0 = jnp.exp(m0 + Fc - m)
    wL = jnp.exp(a + Fl - mL)
    emm = jnp.exp(-m)
    Fm = Fc - m
    aT = a.T

    xp = proj(0)
    for bb in range(BB):
        ext_s[bb, HIST:HIST + TB, :] = xp[bb * TB:(bb + 1) * TB, :]
    zp_s[...] = _silu(proj(1))
    q_s[...] = proj(2).astype(BF16)
    k_s[...] = (proj(3) * (DH ** -0.5)).astype(BF16)
    v_s[...] = proj(4).astype(BF16)

    pos_head = start + t * TB + lax.broadcasted_iota(jnp.int32, (HIST, 1), 0)
    for g, w in enumerate(POOL_WINDOWS):
        cols = slice(g * GW, (g + 1) * GW)
        inv_head = 1.0 / jnp.minimum(pos_head + 1, w).astype(F32)
        for bb in range(BB):
            ext = ext_s[bb, :, cols]
            win = ext
            s = 1
            while s < w:
                win = win + pltpu.roll(win, s, 0)
                s *= 2
            r0 = bb * TB
            head = win[HIST:2 * HIST, :] * inv_head - ext[HIST:2 * HIST, :]
            tail = win[2 * HIST:, :] * (1.0 / w) - ext[2 * HIST:, :]
            pooled_s[r0:r0 + HIST, cols] = head.astype(BF16)
            pooled_s[r0 + HIST:r0 + TB, cols] = tail.astype(BF16)
    for g in range(N_POOL_GROUPS):
        cols = slice(g * GW, (g + 1) * GW)
        mixed = jnp.dot(pooled_s[:, cols], wpool_ref[g], preferred_element_type=F32)
        ycat_s[:, cols] = (mixed * pscale_ref[:, cols] * zp_s[:, cols]).astype(BF16)

    pool_out_ref[...] = ext_s[:, TB + 1:TB + HIST, :]

    for bb in range(BB):
        ext_s[bb, 0:HIST, :] = ext_s[bb, TB:TB + HIST, :]

    causal = (lax.broadcasted_iota(jnp.int32, (TB, TB), 0)
              >= lax.broadcasted_iota(jnp.int32, (TB, TB), 1))

    for bb in range(BB):
        rows = slice(bb * TB, (bb + 1) * TB)
        dL = jnp.exp(m0_rows[bb] + Fl_rows[bb] - mL_rows[bb])
        for hd in range(N_HEADS):
            cols = slice(hd * DH, (hd + 1) * DH)
            q = q_s[rows, cols]
            k = k_s[rows, cols]
            v = v_s[rows, cols]
            s = lax.dot_general(q, k, (((1,), (1,)), ((), ())), preferred_element_type=F32)
            logD = Fm[rows, hd:hd + 1] + aT[hd:hd + 1, rows]
            S = s * jnp.exp(jnp.where(causal, logD, -jnp.inf))
            C0 = C_ref[bb, hd]
            n0 = n_ref[bb, hd:hd + 1, :]
            d0 = decay0[rows, hd:hd + 1]
            qn = jnp.sum(q.astype(F32) * n0, axis=-1, keepdims=True)
            nq = jnp.sum(S, axis=-1, keepdims=True) + d0 * qn
            num = (jnp.dot(S.astype(BF16), v, preferred_element_type=F32)
                   + d0 * jnp.dot(q, C0.astype(BF16), preferred_element_type=F32))
            den = jnp.maximum(jnp.abs(nq), emm[rows, hd:hd + 1])
            hh = num * (1.0 / den)
            hc = hh - jnp.mean(hh, axis=-1, keepdims=True)
            hn = hc * lax.rsqrt(jnp.mean(hc * hc, axis=-1, keepdims=True) + EPS)
            hm_s[rows, cols] = hn * ghead_ref[:, cols]
            kw = k.astype(F32) * wL[rows, hd:hd + 1]
            dl = dL[:, hd:hd + 1]
            C_ref[bb, hd] = dl * C0 + lax.dot_general(
                kw.astype(BF16), v, (((0,), (0,)), ((), ())), preferred_element_type=F32)
            n_ref[bb, hd:hd + 1, :] = dl * n0 + jnp.sum(kw, axis=0, keepdims=True)
        m_ref[bb] = mL_rows[bb]

    hm_s[...] = hm_s[...] * _sigmoid(proj(5))
    ycat_s[:, W:2 * W] = (hm_s[...] * _silu(proj(6))).astype(BF16)

    y = jnp.dot(ycat_s[...], wout_ref[...], preferred_element_type=F32)
    for bb in range(BB):
        gate = mod_ref[bb, :, 2 * D:3 * D]
        xn = x_ref[bb] + gate * y[bb * TB:(bb + 1) * TB, :]
        r = lax.rsqrt(jnp.mean(xn * xn, axis=-1, keepdims=True) + EPS)
        y_ref[bb] = (xn * r) * gfinal_ref[...]


def _const_spec(shape):
    nd = len(shape)
    return pl.BlockSpec(shape, lambda b, t: (0,) * nd, pipeline_mode=pl.Buffered(1))


def _run_layer(x, mod, mod_row0, state, weights, *, BB, TB, start):
    B, T, D = x.shape
    W, H, DH = W_POOL, N_HEADS, HEAD_DIM
    R = BB * TB
    has_state = state is not None
    grid = (B // BB, T // TB)
    assert mod_row0 % BB == 0
    mod_blk0 = mod_row0 // BB

    in_specs = [
        pl.BlockSpec((BB, TB, D), lambda b, t: (b, t, 0)),
        pl.BlockSpec((BB, 1, 3 * D), lambda b, t: (b + mod_blk0, 0, 0)),
    ]
    args = [x, mod]
    if has_state:
        pool0, C0, n0, m0 = state
        in_specs += [
            pl.BlockSpec((BB, POOL_BUF, W), lambda b, t: (b, 0, 0)),
            pl.BlockSpec((BB, H, DH, DH), lambda b, t: (b, 0, 0, 0)),
            pl.BlockSpec((BB, H, DH), lambda b, t: (b, 0, 0)),
            pl.BlockSpec((BB, 1, LANES), lambda b, t: (b, 0, 0)),
        ]
        m0p = jnp.pad(m0, ((0, 0), (0, LANES - H))).reshape(B, 1, LANES)
        args += [pool0, C0, n0, m0p]
    in_specs += [_const_spec(w.shape) for w in weights]
    args += list(weights)

    out_shape = (
        jax.ShapeDtypeStruct((B, T, D), F32),
        jax.ShapeDtypeStruct((B, POOL_BUF, W), F32),
        jax.ShapeDtypeStruct((B, H, DH, DH), F32),
        jax.ShapeDtypeStruct((B, H, DH), F32),
        jax.ShapeDtypeStruct((B, 1, LANES), F32),
    )
    out_specs = (
        pl.BlockSpec((BB, TB, D), lambda b, t: (b, t, 0)),
        pl.BlockSpec((BB, POOL_BUF, W), lambda b, t: (b, 0, 0)),
        pl.BlockSpec((BB, H, DH, DH), lambda b, t: (b, 0, 0, 0)),
        pl.BlockSpec((BB, H, DH), lambda b, t: (b, 0, 0)),
        pl.BlockSpec((BB, 1, LANES), lambda b, t: (b, 0, 0)),
    )
    scratch = [
        pltpu.VMEM((R, D), BF16),
        pltpu.VMEM((BB, HIST + TB, W), F32),
        pltpu.VMEM((R, W), BF16),
        pltpu.VMEM((R, W), F32),
        pltpu.VMEM((R, W), F32),
        pltpu.VMEM((R, W), BF16),
        pltpu.VMEM((R, W), BF16),
        pltpu.VMEM((R, W), BF16),
        pltpu.VMEM((R, 2 * W), BF16),
        pltpu.VMEM((BB, 1, D), F32),
    ]
    y, pool, C, n, m = pl.pallas_call(
        functools.partial(_layer_kernel, BB=BB, TB=TB, start=start, has_state=has_state),
        grid=grid,
        in_specs=in_specs,
        out_specs=out_specs,
        out_shape=out_shape,
        scratch_shapes=scratch,
        compiler_params=pltpu.CompilerParams(
            dimension_semantics=("parallel", "arbitrary"),
            vmem_limit_bytes=VMEM_LIMIT_BYTES),
        name="layer_state" if has_state else "layer_fresh",
    )(*args)
    return y, pool, C, n, m[:, 0, :H]


def kernel(x_prompt, x_sample, c_prompt, c_sample, state_pool, state_C, state_n, state_m,
           w_ada, b_ada, g_norm, w_in, b_i, b_f, w_pool, pool_scale, g_head, w_out, g_final):
    depth = w_ada.shape[0]
    assert depth == 1, "single-layer trunk"
    l = 0
    nbp = x_prompt.shape[0]
    n_main = N_MAIN_SECTIONS * W_POOL

    mod = _adaln_mod(jnp.concatenate([c_prompt, c_sample], axis=0), w_ada[l], b_ada[l])

    w_in_t = jnp.swapaxes(w_in[l], 0, 1)
    n_gate = 2 * N_HEADS
    assert n_main % n_gate == 0 and w_in_t.shape[0] == n_main + n_gate
    w_gate_t = _row_block(w_in_t, n_main // n_gate, n_gate)
    w_gate = jnp.pad(w_gate_t.T, ((0, 0), (0, LANES - n_gate))).astype(BF16)
    gate_bias = jnp.pad(jnp.concatenate([b_i[l], b_f[l]]), (0, LANES - 2 * N_HEADS)).reshape(1, LANES)
    weights = (
        g_norm[l].reshape(1, D_MODEL),
        _transposed_rows_bf16(w_in_t, n_main, W_POOL),
        w_gate,
        gate_bias,
        w_pool[l].astype(BF16),
        pool_scale[l].reshape(1, W_POOL),
        g_head[l].reshape(1, W_MLSTM),
        _rows_bf16(w_out[l], 512),
        g_final.reshape(1, D_MODEL),
    )

    yp, pp, pc, pn, pm = _run_layer(x_prompt, mod, 0, None, weights, BB=2, TB=256, start=0)
    ys, sp, sc, sn, sm = _run_layer(
        x_sample, mod, nbp, (state_pool[l], state_C[l], state_n[l], state_m[l]), weights,
        BB=4, TB=x_sample.shape[1], start=PAST_LEN)
    return (yp, ys, pp[None], pc[None], pn[None], pm[None], sp[None], sc[None], sn[None], sm[None])
```

```python
import functools

import jax
import jax.numpy as jnp
from jax import lax
from jax.experimental import pallas as pl
from jax.experimental.pallas import tpu as pltpu

F32 = jnp.float32
BF16 = jnp.bfloat16

D_MODEL = 1024
W_POOL = 1024
W_MLSTM = 1024
POOL_WINDOWS = (2, 4, 8, 16)
N_POOL_GROUPS = 4
POOL_GW = W_POOL // N_POOL_GROUPS
POOL_BUF = 15
N_HEADS = 4
HEAD_DIM = W_MLSTM // N_HEADS
EPS = 1e-6
N_MAIN_SECTIONS = 7
LANES = 128
PAST_LEN = 2048
HIST = 16
OUT_COLS = 256
VMEM_LIMIT_BYTES = 56 * 1024 * 1024


def _sigmoid(z):
    return 0.5 * jnp.tanh(0.5 * z) + 0.5


def _silu(z):
    hz = 0.5 * z
    return hz * jnp.tanh(hz) + hz


def _log_sigmoid(z):
    return jnp.minimum(z, 0.0) - jnp.log1p(jnp.exp(-jnp.abs(z)))


def _mod_kernel(c_ref, w_ref, b_ref, o_ref):
    c = c_ref[...]
    mod = jnp.dot(_silu(c).astype(BF16), w_ref[...].astype(BF16),
                  preferred_element_type=F32) + b_ref[...]
    for b in range(mod.shape[0]):
        o_ref[b] = mod[b:b + 1, :]


def _adaln_mod(c, w_ada, b_ada):
    nb = c.shape[0]
    n_out = w_ada.shape[1]
    blk = D_MODEL // 2
    return pl.pallas_call(
        _mod_kernel,
        grid=(n_out // blk,),
        in_specs=[
            pl.BlockSpec((nb, D_MODEL), lambda j: (0, 0)),
            pl.BlockSpec((D_MODEL, blk), lambda j: (0, j)),
            pl.BlockSpec((1, blk), lambda j: (0, j)),
        ],
        out_specs=pl.BlockSpec((nb, 1, blk), lambda j: (0, 0, j)),
        out_shape=jax.ShapeDtypeStruct((nb, 1, n_out), F32),
        name="adaln_mod",
    )(c, w_ada, b_ada.reshape(1, n_out))


def _cast_kernel(i_ref, o_ref):
    o_ref[...] = i_ref[...].T.astype(o_ref.dtype)


def _transposed_rows_bf16(wt, n_rows, blk):
    k = wt.shape[1]
    return pl.pallas_call(
        _cast_kernel,
        grid=(n_rows // blk,),
        in_specs=[pl.BlockSpec((blk, k), lambda j: (j, 0))],
        out_specs=pl.BlockSpec((k, blk), lambda j: (0, j)),
        out_shape=jax.ShapeDtypeStruct((k, n_rows), BF16),
        name="cast_bf16",
    )(wt)


def _plain_cast_kernel(i_ref, o_ref):
    o_ref[...] = i_ref[...].astype(o_ref.dtype)


def _rows_bf16(w, blk):
    n, k = w.shape
    return pl.pallas_call(
        _plain_cast_kernel,
        grid=(n // blk,),
        in_specs=[pl.BlockSpec((blk, k), lambda j: (j, 0))],
        out_specs=pl.BlockSpec((blk, k), lambda j: (j, 0)),
        out_shape=jax.ShapeDtypeStruct((n, k), BF16),
        name="cast_rows_bf16",
    )(w)


def _copy_kernel(i_ref, o_ref):
    o_ref[...] = i_ref[...]


def _row_block(wt, blk_index, blk):
    k = wt.shape[1]
    return pl.pallas_call(
        _copy_kernel,
        grid=(1,),
        in_specs=[pl.BlockSpec((blk, k), lambda j: (blk_index, 0))],
        out_specs=pl.BlockSpec((blk, k), lambda j: (0, 0)),
        out_shape=jax.ShapeDtypeStruct((blk, k), wt.dtype),
        name="row_block",
    )(wt)


def _seg_scan(x, tl, seg, combine, fill):
    k = 1
    while k < seg:
        shifted = pltpu.roll(x, k, 0)
        x = combine(x, jnp.where(tl >= k, shifted, fill))
        k *= 2
    return x


def _rows_of(per_batch_rows, TB):
    parts = [jnp.broadcast_to(r, (TB, LANES)) for r in per_batch_rows]
    return parts[0] if len(parts) == 1 else jnp.concatenate(parts, axis=0)


def _layer_kernel(*refs, BB, TB, start, has_state):
    R = BB * TB
    D, W, DH, GW = D_MODEL, W_POOL, HEAD_DIM, POOL_GW
    it = iter(refs)
    x_ref = next(it)
    mod_ref = next(it)
    if has_state:
        pool0_ref, C0_ref, n0_ref, m0_ref = next(it), next(it), next(it), next(it)
    gnorm_ref, win_ref, wg_ref, gbias_ref = next(it), next(it), next(it), next(it)
    wpool_ref, pscale_ref, ghead_ref, wout_ref, gfinal_ref = (next(it), next(it), next(it),
                                                             next(it), next(it))
    y_ref, pool_out_ref, C_ref, n_ref, m_ref = next(it), next(it), next(it), next(it), next(it)
    h_s, ext_s, pooled_s, zp_s, hm_s, q_s, k_s, v_s, ycat_s, gain_s = it

    t = pl.program_id(1)

    @pl.when(t == 0)
    def _init():
        if has_state:
            ext_s[:, 0:1, :] = jnp.zeros((BB, 1, W), F32)
            ext_s[:, 1:HIST, :] = pool0_ref[...]
            C_ref[...] = C0_ref[...]
            n_ref[...] = n0_ref[...]
            m_ref[...] = m0_ref[...]
        else:
            ext_s[:, 0:HIST, :] = jnp.zeros((BB, HIST, W), F32)
            C_ref[...] = jnp.zeros_like(C_ref)
            n_ref[...] = jnp.zeros_like(n_ref)
            m_ref[...] = jnp.zeros_like(m_ref)
        gain_s[...] = gnorm_ref[...] * (1.0 + mod_ref[:, :, D:2 * D])

    for bb in range(BB):
        x = x_ref[bb]
        r = lax.rsqrt(jnp.mean(x * x, axis=-1, keepdims=True) + EPS)
        shift = mod_ref[bb, :, 0:D]
        h = (x * r) * gain_s[bb] + shift
        h_s[bb * TB:(bb + 1) * TB, :] = h.astype(BF16)

    def proj(sec):
        return jnp.dot(h_s[...], win_ref[:, sec * W:(sec + 1) * W], preferred_element_type=F32)

    gates = jnp.dot(h_s[...], wg_ref[...], preferred_element_type=F32) + gbias_ref[...]

    tl = lax.broadcasted_iota(jnp.int32, (R, LANES), 0) & (TB - 1)
    ig = gates
    lf = _log_sigmoid(pltpu.roll(gates, LANES - N_HEADS, 1))
    Fc = _seg_scan(lf, tl, TB, jnp.add, 0.0)
    a = ig - Fc
    cmax = _seg_scan(a, tl, TB, jnp.maximum, -jnp.inf)
    m0_rows = [m_ref[bb] for bb in range(BB)]
    m0 = _rows_of(m0_rows, TB)
    m = Fc + jnp.maximum(m0, cmax)
    Fl_rows = [Fc[(bb + 1) * TB - 1:(bb + 1) * TB, :] for bb in range(BB)]
    mL_rows = [m[(bb + 1) * TB - 1:(bb + 1) * TB, :] for bb in range(BB)]
    Fl = _rows_of(Fl_rows, TB)
    mL = _rows_of(mL_rows, TB)
    decay0 = jnp.exp(m0 + Fc - m)
    wL = jnp.exp(a + Fl - mL)
    emm = jnp.exp(-m)
    Fm = Fc - m
    aT = a.T

    xp = proj(0)
    for bb in range(BB):
        ext_s[bb, HIST:HIST + TB, :] = xp[bb * TB:(bb + 1) * TB, :]
    zp_s[...] = _silu(proj(1))
    q_s[...] = proj(2).astype(BF16)
    k_s[...] = (proj(3) * (DH ** -0.5)).astype(BF16)
    v_s[...] = proj(4).astype(BF16)

    pos_head = start + t * TB + lax.broadcasted_iota(jnp.int32, (HIST, 1), 0)
    for g, w in enumerate(POOL_WINDOWS):
        cols = slice(g * GW, (g + 1) * GW)
        inv_head = 1.0 / jnp.minimum(pos_head + 1, w).astype(F32)
        for bb in range(BB):
            ext = ext_s[bb, :, cols]
            win = ext
            s = 1
            while s < w:
                win = win + pltpu.roll(win, s, 0)
                s *= 2
            r0 = bb * TB
            head = win[HIST:2 * HIST, :] * inv_head - ext[HIST:2 * HIST, :]
            tail = win[2 * HIST:, :] * (1.0 / w) - ext[2 * HIST:, :]
            pooled_s[r0:r0 + HIST, cols] = head.astype(BF16)
            pooled_s[r0 + HIST:r0 + TB, cols] = tail.astype(BF16)
    for g in range(N_POOL_GROUPS):
        cols = slice(g * GW, (g + 1) * GW)
        mixed = jnp.dot(pooled_s[:, cols], wpool_ref[g], preferred_element_type=F32)
        ycat_s[:, cols] = (mixed * pscale_ref[:, cols] * zp_s[:, cols]).astype(BF16)

    pool_out_ref[...] = ext_s[:, TB + 1:TB + HIST, :]

    for bb in range(BB):
        ext_s[bb, 0:HIST, :] = ext_s[bb, TB:TB + HIST, :]

    causal = (lax.broadcasted_iota(jnp.int32, (TB, TB), 0)
              >= lax.broadcasted_iota(jnp.int32, (TB, TB), 1))

    for bb in range(BB):
        rows = slice(bb * TB, (bb + 1) * TB)
        dL = jnp.exp(m0_rows[bb] + Fl_rows[bb] - mL_rows[bb])
        for hd in range(N_HEADS):
            cols = slice(hd * DH, (hd + 1) * DH)
            q = q_s[rows, cols]
            k = k_s[rows, cols]
            v = v_s[rows, cols]
            s = lax.dot_general(q, k, (((1,), (1,)), ((), ())), preferred_element_type=F32)
            logD = Fm[rows, hd:hd + 1] + aT[hd:hd + 1, rows]
            S = s * jnp.exp(jnp.where(causal, logD, -jnp.inf))
            C0 = C_ref[bb, hd]
            n0 = n_ref[bb, hd:hd + 1, :]
            d0 = decay0[rows, hd:hd + 1]
            qn = jnp.sum(q.astype(F32) * n0, axis=-1, keepdims=True)
            nq = jnp.sum(S, axis=-1, keepdims=True) + d0 * qn
            num = (jnp.dot(S.astype(BF16), v, preferred_element_type=F32)
                   + d0 * jnp.dot(q, C0.astype(BF16), preferred_element_type=F32))
            den = jnp.maximum(jnp.abs(nq), emm[rows, hd:hd + 1])
            hh = num * (1.0 / den)
            hc = hh - jnp.mean(hh, axis=-1, keepdims=True)
            hn = hc * lax.rsqrt(jnp.mean(hc * hc, axis=-1, keepdims=True) + EPS)
            hm_s[rows, cols] = hn * ghead_ref[:, cols]
            kw = k.astype(F32) * wL[rows, hd:hd + 1]
            dl = dL[:, hd:hd + 1]
            C_ref[bb, hd] = dl * C0 + lax.dot_general(
                kw.astype(BF16), v, (((0,), (0,)), ((), ())), preferred_element_type=F32)
            n_ref[bb, hd:hd + 1, :] = dl * n0 + jnp.sum(kw, axis=0, keepdims=True)
        m_ref[bb] = mL_rows[bb]

    hm_s[...] = hm_s[...] * _sigmoid(proj(5))
    ycat_s[:, W:2 * W] = (hm_s[...] * _silu(proj(6))).astype(BF16)

    ssq = [jnp.zeros((TB, 1), F32) for _ in range(BB)]
    for c0 in range(0, D, OUT_COLS):
        cols = slice(c0, c0 + OUT_COLS)
        y = jnp.dot(ycat_s[...], wout_ref[:, cols], preferred_element_type=F32)
        for bb in range(BB):
            gate = mod_ref[bb, :, 2 * D + c0:2 * D + c0 + OUT_COLS]
            xn = x_ref[bb, :, cols] + gate * y[bb * TB:(bb + 1) * TB, :]
            ssq[bb] = ssq[bb] + jnp.sum(xn * xn, axis=-1, keepdims=True)
            y_ref[bb, :, cols] = xn
    for bb in range(BB):
        r = lax.rsqrt(ssq[bb] * (1.0 / D) + EPS)
        y_ref[bb] = (y_ref[bb] * r) * gfinal_ref[...]


def _const_spec(shape):
    nd = len(shape)
    return pl.BlockSpec(shape, lambda b, t: (0,) * nd, pipeline_mode=pl.Buffered(1))


def _run_layer(x, mod, mod_row0, state, weights, *, BB, TB, start):
    B, T, D = x.shape
    W, H, DH = W_POOL, N_HEADS, HEAD_DIM
    R = BB * TB
    has_state = state is not None
    grid = (B // BB, T // TB)
    assert mod_row0 % BB == 0
    mod_blk0 = mod_row0 // BB

    in_specs = [
        pl.BlockSpec((BB, TB, D), lambda b, t: (b, t, 0)),
        pl.BlockSpec((BB, 1, 3 * D), lambda b, t: (b + mod_blk0, 0, 0)),
    ]
    args = [x, mod]
    if has_state:
        pool0, C0, n0, m0 = state
        in_specs += [
            pl.BlockSpec((BB, POOL_BUF, W), lambda b, t: (b, 0, 0)),
            pl.BlockSpec((BB, H, DH, DH), lambda b, t: (b, 0, 0, 0)),
            pl.BlockSpec((BB, H, DH), lambda b, t: (b, 0, 0)),
            pl.BlockSpec((BB, 1, LANES), lambda b, t: (b, 0, 0)),
        ]
        m0p = jnp.pad(m0, ((0, 0), (0, LANES - H))).reshape(B, 1, LANES)
        args += [pool0, C0, n0, m0p]
    in_specs += [_const_spec(w.shape) for w in weights]
    args += list(weights)

    out_shape = (
        jax.ShapeDtypeStruct((B, T, D), F32),
        jax.ShapeDtypeStruct((B, POOL_BUF, W), F32),
        jax.ShapeDtypeStruct((B, H, DH, DH), F32),
        jax.ShapeDtypeStruct((B, H, DH), F32),
        jax.ShapeDtypeStruct((B, 1, LANES), F32),
    )
    out_specs = (
        pl.BlockSpec((BB, TB, D), lambda b, t: (b, t, 0)),
        pl.BlockSpec((BB, POOL_BUF, W), lambda b, t: (b, 0, 0)),
        pl.BlockSpec((BB, H, DH, DH), lambda b, t: (b, 0, 0, 0)),
        pl.BlockSpec((BB, H, DH), lambda b, t: (b, 0, 0)),
        pl.BlockSpec((BB, 1, LANES), lambda b, t: (b, 0, 0)),
    )
    scratch = [
        pltpu.VMEM((R, D), BF16),
        pltpu.VMEM((BB, HIST + TB, W), F32),
        pltpu.VMEM((R, W), BF16),
        pltpu.VMEM((R, W), F32),
        pltpu.VMEM((R, W), F32),
        pltpu.VMEM((R, W), BF16),
        pltpu.VMEM((R, W), BF16),
        pltpu.VMEM((R, W), BF16),
        pltpu.VMEM((R, 2 * W), BF16),
        pltpu.VMEM((BB, 1, D), F32),
    ]
    y, pool, C, n, m = pl.pallas_call(
        functools.partial(_layer_kernel, BB=BB, TB=TB, start=start, has_state=has_state),
        grid=grid,
        in_specs=in_specs,
        out_specs=out_specs,
        out_shape=out_shape,
        scratch_shapes=scratch,
        compiler_params=pltpu.CompilerParams(
            dimension_semantics=("parallel", "arbitrary"),
            vmem_limit_bytes=VMEM_LIMIT_BYTES),
        name="layer_state" if has_state else "layer_fresh",
    )(*args)
    return y, pool, C, n, m[:, 0, :H]


def kernel(x_prompt, x_sample, c_prompt, c_sample, state_pool, state_C, state_n, state_m,
           w_ada, b_ada, g_norm, w_in, b_i, b_f, w_pool, pool_scale, g_head, w_out, g_final):
    depth = w_ada.shape[0]
    assert depth == 1, "single-layer trunk"
    l = 0
    nbp = x_prompt.shape[0]
    n_main = N_MAIN_SECTIONS * W_POOL

    mod = _adaln_mod(jnp.concatenate([c_prompt, c_sample], axis=0), w_ada[l], b_ada[l])

    w_in_t = jnp.swapaxes(w_in[l], 0, 1)
    n_gate = 2 * N_HEADS
    assert n_main % n_gate == 0 and w_in_t.shape[0] == n_main + n_gate
    w_gate_t = _row_block(w_in_t, n_main // n_gate, n_gate)
    w_gate = jnp.pad(w_gate_t.T, ((0, 0), (0, LANES - n_gate))).astype(BF16)
    gate_bias = jnp.pad(jnp.concatenate([b_i[l], b_f[l]]), (0, LANES - 2 * N_HEADS)).reshape(1, LANES)
    weights = (
        g_norm[l].reshape(1, D_MODEL),
        _transposed_rows_bf16(w_in_t, n_main, W_POOL),
        w_gate,
        gate_bias,
        w_pool[l].astype(BF16),
        pool_scale[l].reshape(1, W_POOL),
        g_head[l].reshape(1, W_MLSTM),
        _rows_bf16(w_out[l], 512),
        g_final.reshape(1, D_MODEL),
    )

    yp, pp, pc, pn, pm = _run_layer(x_prompt, mod, 0, None, weights, BB=2, TB=256, start=0)
    ys, sp, sc, sn, sm = _run_layer(
        x_sample, mod, nbp, (state_pool[l], state_C[l], state_n[l], state_m[l]), weights,
        BB=4, TB=x_sample.shape[1], start=PAST_LEN)
    return (yp, ys, pp[None], pc[None], pn[None], pm[None], sp[None], sc[None], sn[None], sm[None])
```

```python
import functools

import jax
import jax.numpy as jnp
from jax import lax
from jax.experimental import pallas as pl
from jax.experimental.pallas import tpu as pltpu

F32 = jnp.float32
BF16 = jnp.bfloat16

D_MODEL = 1024
W_POOL = 1024
W_MLSTM = 1024
POOL_WINDOWS = (2, 4, 8, 16)
N_POOL_GROUPS = 4
POOL_GW = W_POOL // N_POOL_GROUPS
POOL_BUF = 15
N_HEADS = 4
HEAD_DIM = W_MLSTM // N_HEADS
EPS = 1e-6
N_MAIN_SECTIONS = 7
LANES = 128
SUB = 8
PAST_LEN = 2048
HIST = 16
OUT_COLS = 256
MIN_ROWS_FOR_COLUMN_BLOCKS = 256
VMEM_LIMIT_BYTES = 56 * 1024 * 1024
MLSTM_CHUNK = 256
MAX_STEP_ROWS = 512
STATE_WINDOW_BYTES = 16 * 1024 * 1024


def _sigmoid(z):
    return 0.5 * jnp.tanh(0.5 * z) + 0.5


def _silu(z):
    hz = 0.5 * z
    return hz * jnp.tanh(hz) + hz


def _log_sigmoid(z):
    return jnp.minimum(z, 0.0) - jnp.log1p(jnp.exp(-jnp.abs(z)))


def _mod_kernel(c_ref, w_ref, b_ref, o_ref):
    c = c_ref[...]
    mod = jnp.dot(_silu(c), w_ref[...], preferred_element_type=F32) + b_ref[...]
    for b in range(mod.shape[0]):
        o_ref[b] = mod[b:b + 1, :]


def _adaln_mod(c, w_ada, b_ada):
    nb = c.shape[0]
    n_out = w_ada.shape[1]
    blk = D_MODEL // 2
    return pl.pallas_call(
        _mod_kernel,
        grid=(n_out // blk,),
        in_specs=[
            pl.BlockSpec((nb, D_MODEL), lambda j: (0, 0)),
            pl.BlockSpec((D_MODEL, blk), lambda j: (0, j)),
            pl.BlockSpec((1, blk), lambda j: (0, j)),
        ],
        out_specs=pl.BlockSpec((nb, 1, blk), lambda j: (0, 0, j)),
        out_shape=jax.ShapeDtypeStruct((nb, 1, n_out), F32),
        name="adaln_mod",
    )(c, w_ada, b_ada.reshape(1, n_out))


def _cast_kernel(i_ref, o_ref):
    o_ref[...] = i_ref[...].T.astype(o_ref.dtype)


def _transposed_rows_bf16(wt, n_rows, blk):
    k = wt.shape[1]
    return pl.pallas_call(
        _cast_kernel,
        grid=(n_rows // blk,),
        in_specs=[pl.BlockSpec((blk, k), lambda j: (j, 0))],
        out_specs=pl.BlockSpec((k, blk), lambda j: (0, j)),
        out_shape=jax.ShapeDtypeStruct((k, n_rows), BF16),
        name="cast_bf16",
    )(wt)


def _plain_cast_kernel(i_ref, o_ref):
    o_ref[...] = i_ref[...].astype(o_ref.dtype)


def _rows_bf16(w, blk):
    n, k = w.shape
    return pl.pallas_call(
        _plain_cast_kernel,
        grid=(n // blk,),
        in_specs=[pl.BlockSpec((blk, k), lambda j: (j, 0))],
        out_specs=pl.BlockSpec((blk, k), lambda j: (j, 0)),
        out_shape=jax.ShapeDtypeStruct((n, k), BF16),
        name="cast_rows_bf16",
    )(w)


def _copy_kernel(i_ref, o_ref):
    o_ref[...] = i_ref[...]


def _row_block(wt, blk_index, blk):
    k = wt.shape[1]
    return pl.pallas_call(
        _copy_kernel,
        grid=(1,),
        in_specs=[pl.BlockSpec((blk, k), lambda j: (blk_index, 0))],
        out_specs=pl.BlockSpec((blk, k), lambda j: (0, 0)),
        out_shape=jax.ShapeDtypeStruct((blk, k), wt.dtype),
        name="row_block",
    )(wt)


def _seg_scan(x, tl, seg, combine, fill):
    k = 1
    while k < seg:
        shifted = pltpu.roll(x, k, 1)
        x = combine(x, jnp.where(tl >= k, shifted, fill))
        k *= 2
    return x


def _lanes_of(per_batch_cols, TB):
    parts = [jnp.broadcast_to(c, (SUB, TB)) for c in per_batch_cols]
    return parts[0] if len(parts) == 1 else jnp.concatenate(parts, axis=1)


def _layer_kernel(*refs, BB, TB, start, has_state):
    R = BB * TB
    D, W, DH, GW = D_MODEL, W_POOL, HEAD_DIM, POOL_GW
    it = iter(refs)
    x_ref = next(it)
    mod_ref = next(it)
    if has_state:
        pool0_ref, C0_ref, n0_ref, m0_ref = next(it), next(it), next(it), next(it)
    gnorm_ref, win_ref, wg_ref, gbias_ref = next(it), next(it), next(it), next(it)
    wpool_ref, pscale_ref, ghead_ref, wout_ref, gfinal_ref = (next(it), next(it), next(it),
                                                             next(it), next(it))
    y_ref, pool_out_ref, C_ref, n_ref, m_ref = next(it), next(it), next(it), next(it), next(it)
    h_s, ext_s, pooled_s, zp_s, hm_s, q_s, k_s, v_s, ycat_s, gain_s = it

    t = pl.program_id(1)

    @pl.when(t == 0)
    def _init():
        if has_state:
            ext_s[:, 0:1, :] = jnp.zeros((BB, 1, W), F32)
            ext_s[:, 1:HIST, :] = pool0_ref[...]
            C_ref[...] = C0_ref[...]
            n_ref[...] = n0_ref[...]
            m_ref[...] = m0_ref[...]
        else:
            ext_s[:, 0:HIST, :] = jnp.zeros((BB, HIST, W), F32)
            C_ref[...] = jnp.zeros_like(C_ref)
            n_ref[...] = jnp.zeros_like(n_ref)
            m_ref[...] = jnp.zeros_like(m_ref)
        gain_s[...] = gnorm_ref[...] * (1.0 + mod_ref[:, :, D:2 * D])

    for bb in range(BB):
        x = x_ref[bb]
        r = lax.rsqrt(jnp.mean(x * x, axis=-1, keepdims=True) + EPS)
        shift = mod_ref[bb, :, 0:D]
        h = (x * r) * gain_s[bb] + shift
        h_s[bb * TB:(bb + 1) * TB, :] = h.astype(BF16)

    def proj(sec):
        return jnp.dot(h_s[...], win_ref[:, sec * W:(sec + 1) * W], preferred_element_type=F32)

    gates = jnp.dot(h_s[...], wg_ref[...], preferred_element_type=F32) + gbias_ref[...]

    g8 = gates.T[0:SUB, :]
    tl = lax.broadcasted_iota(jnp.int32, (SUB, R), 1) & (TB - 1)
    ig = g8
    lf = _log_sigmoid(pltpu.roll(g8, SUB - N_HEADS, 0))
    Fc = _seg_scan(lf, tl, TB, jnp.add, 0.0)
    a = ig - Fc
    cmax = _seg_scan(a, tl, TB, jnp.maximum, -jnp.inf)
    m0_cols = [m_ref[bb][:, 0:1] for bb in range(BB)]
    m0 = _lanes_of(m0_cols, TB)
    m = Fc + jnp.maximum(m0, cmax)
    Fl_cols = [Fc[:, (bb + 1) * TB - 1:(bb + 1) * TB] for bb in range(BB)]
    mL_cols = [m[:, (bb + 1) * TB - 1:(bb + 1) * TB] for bb in range(BB)]
    Fl = _lanes_of(Fl_cols, TB)
    mL = _lanes_of(mL_cols, TB)
    per_row = jnp.concatenate(
        [Fc - m, jnp.exp(m0 + Fc - m), jnp.exp(a + Fl - mL), jnp.exp(-m),
         jnp.zeros((LANES - 4 * SUB, R), F32)], axis=0).T
    Fm, decay0 = per_row[:, 0:SUB], per_row[:, SUB:2 * SUB]
    wL, emm = per_row[:, 2 * SUB:3 * SUB], per_row[:, 3 * SUB:4 * SUB]

    xp = proj(0)
    for bb in range(BB):
        ext_s[bb, HIST:HIST + TB, :] = xp[bb * TB:(bb + 1) * TB, :]

    pos_head = start + t * TB + lax.broadcasted_iota(jnp.int32, (HIST, 1), 0)
    for g, w in enumerate(POOL_WINDOWS):
        cols = slice(g * GW, (g + 1) * GW)
        inv_head = 1.0 / jnp.minimum(pos_head + 1, w).astype(F32)
        for bb in range(BB):
            ext = ext_s[bb, :, cols]
            win = ext
            s = 1
            while s < w:
                win = win + pltpu.roll(win, s, 0)
                s *= 2
            r0 = bb * TB
            head = win[HIST:2 * HIST, :] * inv_head - ext[HIST:2 * HIST, :]
            tail = win[2 * HIST:, :] * (1.0 / w) - ext[2 * HIST:, :]
            pooled_s[r0:r0 + HIST, cols] = head.astype(BF16)
            pooled_s[r0 + HIST:r0 + TB, cols] = tail.astype(BF16)

    zp_s[...] = _silu(proj(1))
    for g in range(N_POOL_GROUPS):
        cols = slice(g * GW, (g + 1) * GW)
        mixed = jnp.dot(pooled_s[:, cols], wpool_ref[g], preferred_element_type=F32)
        ycat_s[:, cols] = (mixed * pscale_ref[:, cols] * zp_s[:, cols]).astype(BF16)
    q_s[...] = proj(2).astype(BF16)
    k_s[...] = (proj(3) * (DH ** -0.5)).astype(BF16)
    v_s[...] = proj(4).astype(BF16)

    pool_out_ref[...] = ext_s[:, TB + 1:TB + HIST, :]

    for bb in range(BB):
        ext_s[bb, 0:HIST, :] = ext_s[bb, TB:TB + HIST, :]

    causal = (lax.broadcasted_iota(jnp.int32, (TB, TB), 0)
              >= lax.broadcasted_iota(jnp.int32, (TB, TB), 1))

    for bb in range(BB):
        rows = slice(bb * TB, (bb + 1) * TB)
        dL = jnp.exp(m0_cols[bb] + Fl_cols[bb] - mL_cols[bb])
        for hd in range(N_HEADS):
            cols = slice(hd * DH, (hd + 1) * DH)
            q = q_s[rows, cols]
            k = k_s[rows, cols]
            v = v_s[rows, cols]
            s = lax.dot_general(q, k, (((1,), (1,)), ((), ())), preferred_element_type=F32)
            logD = Fm[rows, hd:hd + 1] + a[hd:hd + 1, rows]
            S = s * jnp.exp(jnp.where(causal, logD, -jnp.inf))
            C0 = C_ref[bb, hd]
            n0 = n_ref[bb, hd:hd + 1, :]
            d0 = decay0[rows, hd:hd + 1]
            qn = jnp.sum(q.astype(F32) * n0, axis=-1, keepdims=True)
            nq = jnp.sum(S, axis=-1, keepdims=True) + d0 * qn
            num = (jnp.dot(S.astype(BF16), v, preferred_element_type=F32)
                   + d0 * jnp.dot(q, C0.astype(BF16), preferred_element_type=F32))
            den = jnp.maximum(jnp.abs(nq), emm[rows, hd:hd + 1])
            hh = num * (1.0 / den)
            hc = hh - jnp.mean(hh, axis=-1, keepdims=True)
            hn = hc * lax.rsqrt(jnp.mean(hc * hc, axis=-1, keepdims=True) + EPS)
            hm_s[rows, cols] = hn * ghead_ref[:, cols]
            kw = k.astype(F32) * wL[rows, hd:hd + 1]
            dl = dL[hd:hd + 1, :]
            C_ref[bb, hd] = dl * C0 + lax.dot_general(
                kw.astype(BF16), v, (((0,), (0,)), ((), ())), preferred_element_type=F32)
            n_ref[bb, hd:hd + 1, :] = dl * n0 + jnp.sum(kw, axis=0, keepdims=True)
        m_ref[bb] = jnp.broadcast_to(mL_cols[bb], (SUB, LANES))

    hm_s[...] = hm_s[...] * _sigmoid(proj(5))
    ycat_s[:, W:2 * W] = (hm_s[...] * _silu(proj(6))).astype(BF16)

    out_cols = OUT_COLS if R >= MIN_ROWS_FOR_COLUMN_BLOCKS else D
    ssq = [jnp.zeros((TB, 1), F32) for _ in range(BB)]
    for c0 in range(0, D, out_cols):
        cols = slice(c0, c0 + out_cols)
        y = jnp.dot(ycat_s[...], wout_ref[:, cols], preferred_element_type=F32)
        for bb in range(BB):
            gate = mod_ref[bb, :, 2 * D + c0:2 * D + c0 + out_cols]
            xn = x_ref[bb, :, cols] + gate * y[bb * TB:(bb + 1) * TB, :]
            ssq[bb] = ssq[bb] + jnp.sum(xn * xn, axis=-1, keepdims=True)
            y_ref[bb, :, cols] = xn
    for bb in range(BB):
        r = lax.rsqrt(ssq[bb] * (1.0 / D) + EPS)
        y_ref[bb] = (y_ref[bb] * r) * gfinal_ref[...]


def _const_spec(shape):
    nd = len(shape)
    return pl.BlockSpec(shape, lambda b, t: (0,) * nd, pipeline_mode=pl.Buffered(1))


def _block_plan(n_batch, n_tokens, has_state):
    tb = min(n_tokens, MLSTM_CHUNK)
    c_bytes = N_HEADS * HEAD_DIM * HEAD_DIM * 4
    n_windows = 4 if has_state else 2
    bb = min(n_batch, MAX_STEP_ROWS // tb, STATE_WINDOW_BYTES // (n_windows * c_bytes))
    assert n_tokens % tb == 0 and n_batch % bb == 0 and tb >= 2 * HIST
    return bb, tb


def _run_layer(x, mod, mod_row0, state, weights, *, start):
    B, T, D = x.shape
    W, H, DH = W_POOL, N_HEADS, HEAD_DIM
    has_state = state is not None
    BB, TB = _block_plan(B, T, has_state)
    R = BB * TB
    grid = (B // BB, T // TB)
    assert mod_row0 % BB == 0
    mod_blk0 = mod_row0 // BB

    in_specs = [
        pl.BlockSpec((BB, TB, D), lambda b, t: (b, t, 0)),
        pl.BlockSpec((BB, 1, 3 * D), lambda b, t: (b + mod_blk0, 0, 0)),
    ]
    args = [x, mod]
    if has_state:
        pool0, C0, n0, m0 = state
        in_specs += [
            pl.BlockSpec((BB, POOL_BUF, W), lambda b, t: (b, 0, 0)),
            pl.BlockSpec((BB, H, DH, DH), lambda b, t: (b, 0, 0, 0)),
            pl.BlockSpec((BB, H, DH), lambda b, t: (b, 0, 0)),
            pl.BlockSpec((BB, SUB, LANES), lambda b, t: (b, 0, 0)),
        ]
        m0p = jnp.broadcast_to(jnp.pad(m0, ((0, 0), (0, SUB - H)))[:, :, None], (B, SUB, LANES))
        args += [pool0, C0, n0, m0p]
    in_specs += [_const_spec(w.shape) for w in weights]
    args += list(weights)

    out_shape = (
        jax.ShapeDtypeStruct((B, T, D), F32),
        jax.ShapeDtypeStruct((B, POOL_BUF, W), F32),
        jax.ShapeDtypeStruct((B, H, DH, DH), F32),
        jax.ShapeDtypeStruct((B, H, DH), F32),
        jax.ShapeDtypeStruct((B, SUB, LANES), F32),
    )
    out_specs = (
        pl.BlockSpec((BB, TB, D), lambda b, t: (b, t, 0)),
        pl.BlockSpec((BB, POOL_BUF, W), lambda b, t: (b, 0, 0)),
        pl.BlockSpec((BB, H, DH, DH), lambda b, t: (b, 0, 0, 0)),
        pl.BlockSpec((BB, H, DH), lambda b, t: (b, 0, 0)),
        pl.BlockSpec((BB, SUB, LANES), lambda b, t: (b, 0, 0)),
    )
    scratch = [
        pltpu.VMEM((R, D), BF16),
        pltpu.VMEM((BB, HIST + TB, W), F32),
        pltpu.VMEM((R, W), BF16),
        pltpu.VMEM((R, W), F32),
        pltpu.VMEM((R, W), F32),
        pltpu.VMEM((R, W), BF16),
        pltpu.VMEM((R, W), BF16),
        pltpu.VMEM((R, W), BF16),
        pltpu.VMEM((R, 2 * W), BF16),
        pltpu.VMEM((BB, 1, D), F32),
    ]
    y, pool, C, n, m = pl.pallas_call(
        functools.partial(_layer_kernel, BB=BB, TB=TB, start=start, has_state=has_state),
        grid=grid,
        in_specs=in_specs,
        out_specs=out_specs,
        out_shape=out_shape,
        scratch_shapes=scratch,
        compiler_params=pltpu.CompilerParams(
            dimension_semantics=("parallel", "arbitrary"),
            vmem_limit_bytes=VMEM_LIMIT_BYTES),
        name="layer_state" if has_state else "layer_fresh",
    )(*args)
    return y, pool, C, n, m[:, :H, 0]


def kernel(x_prompt, x_sample, c_prompt, c_sample, state_pool, state_C, state_n, state_m,
           w_ada, b_ada, g_norm, w_in, b_i, b_f, w_pool, pool_scale, g_head, w_out, g_final):
    depth = w_ada.shape[0]
    assert depth == 1, "single-layer trunk"
    l = 0
    nbp = x_prompt.shape[0]
    n_main = N_MAIN_SECTIONS * W_POOL

    mod = _adaln_mod(jnp.concatenate([c_prompt, c_sample], axis=0), w_ada[l], b_ada[l])

    w_in_t = jnp.swapaxes(w_in[l], 0, 1)
    n_gate = 2 * N_HEADS
    assert n_main % n_gate == 0 and w_in_t.shape[0] == n_main + n_gate
    w_gate_t = _row_block(w_in_t, n_main // n_gate, n_gate)
    w_gate = jnp.pad(w_gate_t.T, ((0, 0), (0, LANES - n_gate))).astype(BF16)
    gate_bias = jnp.pad(jnp.concatenate([b_i[l], b_f[l]]), (0, LANES - 2 * N_HEADS)).reshape(1, LANES)
    weights = (
        g_norm[l].reshape(1, D_MODEL),
        _transposed_rows_bf16(w_in_t, n_main, W_POOL),
        w_gate,
        gate_bias,
        w_pool[l].astype(BF16),
        pool_scale[l].reshape(1, W_POOL),
        g_head[l].reshape(1, W_MLSTM),
        _rows_bf16(w_out[l], 512),
        g_final.reshape(1, D_MODEL),
    )

    yp, pp, pc, pn, pm = _run_layer(x_prompt, mod, 0, None, weights, start=0)
    ys, sp, sc, sn, sm = _run_layer(
        x_sample, mod, nbp, (state_pool[l], state_C[l], state_n[l], state_m[l]), weights,
        start=PAST_LEN)
    return (yp, ys, pp[None], pc[None], pn[None], pm[None], sp[None], sc[None], sn[None], sm[None])
```

```python
import functools

import jax
import jax.numpy as jnp
from jax import lax
from jax.experimental import pallas as pl
from jax.experimental.pallas import tpu as pltpu

F32 = jnp.float32
BF16 = jnp.bfloat16

D_MODEL = 1024
W_POOL = 1024
W_MLSTM = 1024
POOL_WINDOWS = (2, 4, 8, 16)
N_POOL_GROUPS = 4
POOL_GW = W_POOL // N_POOL_GROUPS
POOL_BUF = 15
N_HEADS = 4
HEAD_DIM = W_MLSTM // N_HEADS
EPS = 1e-6
N_MAIN_SECTIONS = 7
LANES = 128
SUB = 8
PAST_LEN = 2048
HIST = 16
OUT_COLS = 256
MIN_ROWS_FOR_COLUMN_BLOCKS = 256
VMEM_LIMIT_BYTES = 56 * 1024 * 1024
MLSTM_CHUNK = 256
MAX_STEP_ROWS = 512
STATE_WINDOW_BYTES = 16 * 1024 * 1024


def _sigmoid(z):
    return 0.5 * jnp.tanh(0.5 * z) + 0.5


def _silu(z):
    hz = 0.5 * z
    return hz * jnp.tanh(hz) + hz


def _log_sigmoid(z):
    return jnp.minimum(z, 0.0) - jnp.log1p(jnp.exp(-jnp.abs(z)))


def _mod_kernel(c_ref, w_ref, b_ref, o_ref):
    c = c_ref[...]
    mod = jnp.dot(_silu(c), w_ref[...], preferred_element_type=F32) + b_ref[...]
    for b in range(mod.shape[0]):
        o_ref[b] = mod[b:b + 1, :]


def _adaln_mod(c, w_ada, b_ada):
    nb = c.shape[0]
    n_out = w_ada.shape[1]
    blk = D_MODEL // 2
    return pl.pallas_call(
        _mod_kernel,
        grid=(n_out // blk,),
        in_specs=[
            pl.BlockSpec((nb, D_MODEL), lambda j: (0, 0)),
            pl.BlockSpec((D_MODEL, blk), lambda j: (0, j)),
            pl.BlockSpec((1, blk), lambda j: (0, j)),
        ],
        out_specs=pl.BlockSpec((nb, 1, blk), lambda j: (0, 0, j)),
        out_shape=jax.ShapeDtypeStruct((nb, 1, n_out), F32),
        name="adaln_mod",
    )(c, w_ada, b_ada.reshape(1, n_out))


def _cast_kernel(i_ref, o_ref):
    o_ref[...] = i_ref[...].T.astype(o_ref.dtype)


def _transposed_rows_bf16(wt, n_rows, blk):
    k = wt.shape[1]
    return pl.pallas_call(
        _cast_kernel,
        grid=(n_rows // blk,),
        in_specs=[pl.BlockSpec((blk, k), lambda j: (j, 0))],
        out_specs=pl.BlockSpec((k, blk), lambda j: (0, j)),
        out_shape=jax.ShapeDtypeStruct((k, n_rows), BF16),
        name="cast_bf16",
    )(wt)


def _plain_cast_kernel(i_ref, o_ref):
    o_ref[...] = i_ref[...].astype(o_ref.dtype)


def _rows_bf16(w, blk):
    n, k = w.shape
    return pl.pallas_call(
        _plain_cast_kernel,
        grid=(n // blk,),
        in_specs=[pl.BlockSpec((blk, k), lambda j: (j, 0))],
        out_specs=pl.BlockSpec((blk, k), lambda j: (j, 0)),
        out_shape=jax.ShapeDtypeStruct((n, k), BF16),
        name="cast_rows_bf16",
    )(w)


def _copy_kernel(i_ref, o_ref):
    o_ref[...] = i_ref[...]


def _row_block(wt, blk_index, blk):
    k = wt.shape[1]
    return pl.pallas_call(
        _copy_kernel,
        grid=(1,),
        in_specs=[pl.BlockSpec((blk, k), lambda j: (blk_index, 0))],
        out_specs=pl.BlockSpec((blk, k), lambda j: (0, 0)),
        out_shape=jax.ShapeDtypeStruct((blk, k), wt.dtype),
        name="row_block",
    )(wt)


def _seg_scan(x, tl, seg, combine, fill):
    k = 1
    while k < seg:
        shifted = pltpu.roll(x, k, 1)
        x = combine(x, jnp.where(tl >= k, shifted, fill))
        k *= 2
    return x


def _lanes_of(per_batch_cols, TB):
    parts = [jnp.broadcast_to(c, (SUB, TB)) for c in per_batch_cols]
    return parts[0] if len(parts) == 1 else jnp.concatenate(parts, axis=1)


def _layer_kernel(*refs, BB, TB, start, has_state):
    R = BB * TB
    o_gate_early = TB < MLSTM_CHUNK
    D, W, DH, GW = D_MODEL, W_POOL, HEAD_DIM, POOL_GW
    it = iter(refs)
    x_ref = next(it)
    mod_ref = next(it)
    if has_state:
        pool0_ref, C0_ref, n0_ref, m0_ref = next(it), next(it), next(it), next(it)
    gnorm_ref, win_ref, wg_ref, gbias_ref = next(it), next(it), next(it), next(it)
    wpool_ref, pscale_ref, ghead_ref, wout_ref, gfinal_ref = (next(it), next(it), next(it),
                                                             next(it), next(it))
    y_ref, pool_out_ref, C_ref, n_ref, m_ref = next(it), next(it), next(it), next(it), next(it)
    h_s, ext_s, pooled_s, zp_s, hm_s, q_s, k_s, v_s, ycat_s, gain_s, og_s = it

    t = pl.program_id(1)

    @pl.when(t == 0)
    def _init():
        if has_state:
            ext_s[:, 0:1, :] = jnp.zeros((BB, 1, W), F32)
            ext_s[:, 1:HIST, :] = pool0_ref[...]
            C_ref[...] = C0_ref[...]
            n_ref[...] = n0_ref[...]
            m_ref[...] = m0_ref[...]
        else:
            ext_s[:, 0:HIST, :] = jnp.zeros((BB, HIST, W), F32)
            C_ref[...] = jnp.zeros_like(C_ref)
            n_ref[...] = jnp.zeros_like(n_ref)
            m_ref[...] = jnp.zeros_like(m_ref)
        gain_s[...] = gnorm_ref[...] * (1.0 + mod_ref[:, :, D:2 * D])

    for bb in range(BB):
        x = x_ref[bb]
        r = lax.rsqrt(jnp.mean(x * x, axis=-1, keepdims=True) + EPS)
        shift = mod_ref[bb, :, 0:D]
        h = (x * r) * gain_s[bb] + shift
        h_s[bb * TB:(bb + 1) * TB, :] = h.astype(BF16)

    def proj(sec):
        return jnp.dot(h_s[...], win_ref[:, sec * W:(sec + 1) * W], preferred_element_type=F32)

    gates = jnp.dot(h_s[...], wg_ref[...], preferred_element_type=F32) + gbias_ref[...]

    g8 = gates.T[0:SUB, :]
    tl = lax.broadcasted_iota(jnp.int32, (SUB, R), 1) & (TB - 1)
    ig = g8
    lf = _log_sigmoid(pltpu.roll(g8, SUB - N_HEADS, 0))
    Fc = _seg_scan(lf, tl, TB, jnp.add, 0.0)
    a = ig - Fc
    cmax = _seg_scan(a, tl, TB, jnp.maximum, -jnp.inf)
    m0_cols = [m_ref[bb][:, 0:1] for bb in range(BB)]
    m0 = _lanes_of(m0_cols, TB)
    m = Fc + jnp.maximum(m0, cmax)
    Fl_cols = [Fc[:, (bb + 1) * TB - 1:(bb + 1) * TB] for bb in range(BB)]
    mL_cols = [m[:, (bb + 1) * TB - 1:(bb + 1) * TB] for bb in range(BB)]
    Fl = _lanes_of(Fl_cols, TB)
    mL = _lanes_of(mL_cols, TB)
    per_row = jnp.concatenate(
        [Fc - m, jnp.exp(m0 + Fc - m), jnp.exp(a + Fl - mL), jnp.exp(-m),
         jnp.zeros((LANES - 4 * SUB, R), F32)], axis=0).T
    Fm, decay0 = per_row[:, 0:SUB], per_row[:, SUB:2 * SUB]
    wL, emm = per_row[:, 2 * SUB:3 * SUB], per_row[:, 3 * SUB:4 * SUB]

    xp = proj(0)
    for bb in range(BB):
        ext_s[bb, HIST:HIST + TB, :] = xp[bb * TB:(bb + 1) * TB, :]

    pos_head = start + t * TB + lax.broadcasted_iota(jnp.int32, (HIST, 1), 0)
    for g, w in enumerate(POOL_WINDOWS):
        cols = slice(g * GW, (g + 1) * GW)
        inv_head = 1.0 / jnp.minimum(pos_head + 1, w).astype(F32)
        for bb in range(BB):
            ext = ext_s[bb, :, cols]
            win = ext
            s = 1
            while s < w:
                win = win + pltpu.roll(win, s, 0)
                s *= 2
            r0 = bb * TB
            head = win[HIST:2 * HIST, :] * inv_head - ext[HIST:2 * HIST, :]
            tail = win[2 * HIST:, :] * (1.0 / w) - ext[2 * HIST:, :]
            pooled_s[r0:r0 + HIST, cols] = head.astype(BF16)
            pooled_s[r0 + HIST:r0 + TB, cols] = tail.astype(BF16)

    zp_s[...] = _silu(proj(1))
    for g in range(N_POOL_GROUPS):
        cols = slice(g * GW, (g + 1) * GW)
        mixed = jnp.dot(pooled_s[:, cols], wpool_ref[g], preferred_element_type=F32)
        ycat_s[:, cols] = (mixed * pscale_ref[:, cols] * zp_s[:, cols]).astype(BF16)
    q_s[...] = proj(2).astype(BF16)
    k_s[...] = (proj(3) * (DH ** -0.5)).astype(BF16)
    v_s[...] = proj(4).astype(BF16)
    if o_gate_early:
        og_s[...] = _sigmoid(proj(5))

    pool_out_ref[...] = ext_s[:, TB + 1:TB + HIST, :]

    for bb in range(BB):
        ext_s[bb, 0:HIST, :] = ext_s[bb, TB:TB + HIST, :]

    causal = (lax.broadcasted_iota(jnp.int32, (TB, TB), 0)
              >= lax.broadcasted_iota(jnp.int32, (TB, TB), 1))

    for bb in range(BB):
        rows = slice(bb * TB, (bb + 1) * TB)
        dL = jnp.exp(m0_cols[bb] + Fl_cols[bb] - mL_cols[bb])
        for hd in range(N_HEADS):
            cols = slice(hd * DH, (hd + 1) * DH)
            q = q_s[rows, cols]
            k = k_s[rows, cols]
            v = v_s[rows, cols]
            s = lax.dot_general(q, k, (((1,), (1,)), ((), ())), preferred_element_type=F32)
            logD = Fm[rows, hd:hd + 1] + a[hd:hd + 1, rows]
            S = s * jnp.exp(jnp.where(causal, logD, -jnp.inf))
            C0 = C_ref[bb, hd]
            n0 = n_ref[bb, hd:hd + 1, :]
            d0 = decay0[rows, hd:hd + 1]
            qn = jnp.sum(q.astype(F32) * n0, axis=-1, keepdims=True)
            nq = jnp.sum(S, axis=-1, keepdims=True) + d0 * qn
            num = (jnp.dot(S.astype(BF16), v, preferred_element_type=F32)
                   + d0 * jnp.dot(q, C0.astype(BF16), preferred_element_type=F32))
            den = jnp.maximum(jnp.abs(nq), emm[rows, hd:hd + 1])
            hh = num * (1.0 / den)
            hc = hh - jnp.mean(hh, axis=-1, keepdims=True)
            hn = hc * lax.rsqrt(jnp.mean(hc * hc, axis=-1, keepdims=True) + EPS)
            hm = hn * ghead_ref[:, cols]
            hm_s[rows, cols] = hm * og_s[rows, cols] if o_gate_early else hm
            kw = k.astype(F32) * wL[rows, hd:hd + 1]
            dl = dL[hd:hd + 1, :]
            C_ref[bb, hd] = dl * C0 + lax.dot_general(
                kw.astype(BF16), v, (((0,), (0,)), ((), ())), preferred_element_type=F32)
            n_ref[bb, hd:hd + 1, :] = dl * n0 + jnp.sum(kw, axis=0, keepdims=True)
        m_ref[bb] = jnp.broadcast_to(mL_cols[bb], (SUB, LANES))

    if not o_gate_early:
        hm_s[...] = hm_s[...] * _sigmoid(proj(5))
    ycat_s[:, W:2 * W] = (hm_s[...] * _silu(proj(6))).astype(BF16)

    out_cols = OUT_COLS if R >= MIN_ROWS_FOR_COLUMN_BLOCKS else D
    ssq = [jnp.zeros((TB, 1), F32) for _ in range(BB)]
    for c0 in range(0, D, out_cols):
        cols = slice(c0, c0 + out_cols)
        y = jnp.dot(ycat_s[...], wout_ref[:, cols], preferred_element_type=F32)
        for bb in range(BB):
            gate = mod_ref[bb, :, 2 * D + c0:2 * D + c0 + out_cols]
            xn = x_ref[bb, :, cols] + gate * y[bb * TB:(bb + 1) * TB, :]
            ssq[bb] = ssq[bb] + jnp.sum(xn * xn, axis=-1, keepdims=True)
            y_ref[bb, :, cols] = xn
    for bb in range(BB):
        r = lax.rsqrt(ssq[bb] * (1.0 / D) + EPS)
        y_ref[bb] = (y_ref[bb] * r) * gfinal_ref[...]


def _const_spec(shape):
    nd = len(shape)
    return pl.BlockSpec(shape, lambda b, t: (0,) * nd, pipeline_mode=pl.Buffered(1))


def _block_plan(n_batch, n_tokens, has_state):
    tb = min(n_tokens, MLSTM_CHUNK)
    c_bytes = N_HEADS * HEAD_DIM * HEAD_DIM * 4
    n_windows = 4 if has_state else 2
    bb = min(n_batch, MAX_STEP_ROWS // tb, STATE_WINDOW_BYTES // (n_windows * c_bytes))
    assert n_tokens % tb == 0 and n_batch % bb == 0 and tb >= 2 * HIST
    return bb, tb


def _run_layer(x, mod, mod_row0, state, weights, *, start):
    B, T, D = x.shape
    W, H, DH = W_POOL, N_HEADS, HEAD_DIM
    has_state = state is not None
    BB, TB = _block_plan(B, T, has_state)
    R = BB * TB
    grid = (B // BB, T // TB)
    assert mod_row0 % BB == 0
    mod_blk0 = mod_row0 // BB

    in_specs = [
        pl.BlockSpec((BB, TB, D), lambda b, t: (b, t, 0)),
        pl.BlockSpec((BB, 1, 3 * D), lambda b, t: (b + mod_blk0, 0, 0)),
    ]
    args = [x, mod]
    if has_state:
        pool0, C0, n0, m0 = state
        in_specs += [
            pl.BlockSpec((BB, POOL_BUF, W), lambda b, t: (b, 0, 0)),
            pl.BlockSpec((BB, H, DH, DH), lambda b, t: (b, 0, 0, 0)),
            pl.BlockSpec((BB, H, DH), lambda b, t: (b, 0, 0)),
            pl.BlockSpec((BB, SUB, LANES), lambda b, t: (b, 0, 0)),
        ]
        m0p = jnp.broadcast_to(jnp.pad(m0, ((0, 0), (0, SUB - H)))[:, :, None], (B, SUB, LANES))
        args += [pool0, C0, n0, m0p]
    in_specs += [_const_spec(w.shape) for w in weights]
    args += list(weights)

    out_shape = (
        jax.ShapeDtypeStruct((B, T, D), F32),
        jax.ShapeDtypeStruct((B, POOL_BUF, W), F32),
        jax.ShapeDtypeStruct((B, H, DH, DH), F32),
        jax.ShapeDtypeStruct((B, H, DH), F32),
        jax.ShapeDtypeStruct((B, SUB, LANES), F32),
    )
    out_specs = (
        pl.BlockSpec((BB, TB, D), lambda b, t: (b, t, 0)),
        pl.BlockSpec((BB, POOL_BUF, W), lambda b, t: (b, 0, 0)),
        pl.BlockSpec((BB, H, DH, DH), lambda b, t: (b, 0, 0, 0)),
        pl.BlockSpec((BB, H, DH), lambda b, t: (b, 0, 0)),
        pl.BlockSpec((BB, SUB, LANES), lambda b, t: (b, 0, 0)),
    )
    scratch = [
        pltpu.VMEM((R, D), BF16),
        pltpu.VMEM((BB, HIST + TB, W), F32),
        pltpu.VMEM((R, W), BF16),
        pltpu.VMEM((R, W), F32),
        pltpu.VMEM((R, W), F32),
        pltpu.VMEM((R, W), BF16),
        pltpu.VMEM((R, W), BF16),
        pltpu.VMEM((R, W), BF16),
        pltpu.VMEM((R, 2 * W), BF16),
        pltpu.VMEM((BB, 1, D), F32),
        pltpu.VMEM((R, W) if TB < MLSTM_CHUNK else (SUB, LANES), F32),
    ]
    y, pool, C, n, m = pl.pallas_call(
        functools.partial(_layer_kernel, BB=BB, TB=TB, start=start, has_state=has_state),
        grid=grid,
        in_specs=in_specs,
        out_specs=out_specs,
        out_shape=out_shape,
        scratch_shapes=scratch,
        compiler_params=pltpu.CompilerParams(
            dimension_semantics=("parallel", "arbitrary"),
            vmem_limit_bytes=VMEM_LIMIT_BYTES),
        name="layer_state" if has_state else "layer_fresh",
    )(*args)
    return y, pool, C, n, m[:, :H, 0]


def kernel(x_prompt, x_sample, c_prompt, c_sample, state_pool, state_C, state_n, state_m,
           w_ada, b_ada, g_norm, w_in, b_i, b_f, w_pool, pool_scale, g_head, w_out, g_final):
    depth = w_ada.shape[0]
    assert depth == 1, "single-layer trunk"
    l = 0
    nbp = x_prompt.shape[0]
    n_main = N_MAIN_SECTIONS * W_POOL

    mod = _adaln_mod(jnp.concatenate([c_prompt, c_sample], axis=0), w_ada[l], b_ada[l])

    w_in_t = jnp.swapaxes(w_in[l], 0, 1)
    n_gate = 2 * N_HEADS
    assert n_main % n_gate == 0 and w_in_t.shape[0] == n_main + n_gate
    w_gate_t = _row_block(w_in_t, n_main // n_gate, n_gate)
    w_gate = jnp.pad(w_gate_t.T, ((0, 0), (0, LANES - n_gate))).astype(BF16)
    gate_bias = jnp.pad(jnp.concatenate([b_i[l], b_f[l]]), (0, LANES - 2 * N_HEADS)).reshape(1, LANES)
    weights = (
        g_norm[l].reshape(1, D_MODEL),
        _transposed_rows_bf16(w_in_t, n_main, W_POOL),
        w_gate,
        gate_bias,
        w_pool[l].astype(BF16),
        pool_scale[l].reshape(1, W_POOL),
        g_head[l].reshape(1, W_MLSTM),
        _rows_bf16(w_out[l], 512),
        g_final.reshape(1, D_MODEL),
    )

    yp, pp, pc, pn, pm = _run_layer(x_prompt, mod, 0, None, weights, start=0)
    ys, sp, sc, sn, sm = _run_layer(
        x_sample, mod, nbp, (state_pool[l], state_C[l], state_n[l], state_m[l]), weights,
        start=PAST_LEN)
    return (yp, ys, pp[None], pc[None], pn[None], pm[None], sp[None], sc[None], sn[None], sm[None])
```

```python
import functools

import jax
import jax.numpy as jnp
from jax import lax
from jax.experimental import pallas as pl
from jax.experimental.pallas import tpu as pltpu

F32 = jnp.float32
BF16 = jnp.bfloat16

D_MODEL = 1024
W_POOL = 1024
W_MLSTM = 1024
POOL_WINDOWS = (2, 4, 8, 16)
N_POOL_GROUPS = 4
POOL_GW = W_POOL // N_POOL_GROUPS
POOL_BUF = 15
N_HEADS = 4
HEAD_DIM = W_MLSTM // N_HEADS
EPS = 1e-6
N_MAIN_SECTIONS = 7
LANES = 128
SUB = 8
PAST_LEN = 2048
HIST = 16
OUT_COLS = 256
MIN_ROWS_FOR_COLUMN_BLOCKS = 256
VMEM_LIMIT_BYTES = 56 * 1024 * 1024
MLSTM_CHUNK = 256
MAX_STEP_ROWS = 512
STATE_WINDOW_BYTES = 16 * 1024 * 1024


def _sigmoid(z):
    return 0.5 * jnp.tanh(0.5 * z) + 0.5


def _silu(z):
    hz = 0.5 * z
    return hz * jnp.tanh(hz) + hz


def _log_sigmoid(z):
    return jnp.minimum(z, 0.0) - jnp.log1p(jnp.exp(-jnp.abs(z)))


def _mod_kernel(c_ref, w_ref, b_ref, o_ref):
    c = c_ref[...]
    mod = jnp.dot(_silu(c), w_ref[...], preferred_element_type=F32) + b_ref[...]
    for b in range(mod.shape[0]):
        o_ref[b] = mod[b:b + 1, :]


def _adaln_mod(c, w_ada, b_ada):
    nb = c.shape[0]
    n_out = w_ada.shape[1]
    blk = D_MODEL // 2
    return pl.pallas_call(
        _mod_kernel,
        grid=(n_out // blk,),
        in_specs=[
            pl.BlockSpec((nb, D_MODEL), lambda j: (0, 0)),
            pl.BlockSpec((D_MODEL, blk), lambda j: (0, j)),
            pl.BlockSpec((1, blk), lambda j: (0, j)),
        ],
        out_specs=pl.BlockSpec((nb, 1, blk), lambda j: (0, 0, j)),
        out_shape=jax.ShapeDtypeStruct((nb, 1, n_out), F32),
        name="adaln_mod",
    )(c, w_ada, b_ada.reshape(1, n_out))


def _cast_kernel(i_ref, o_ref):
    o_ref[...] = i_ref[...].T.astype(o_ref.dtype)


def _transposed_rows_bf16(wt, n_rows, blk):
    k = wt.shape[1]
    return pl.pallas_call(
        _cast_kernel,
        grid=(n_rows // blk,),
        in_specs=[pl.BlockSpec((blk, k), lambda j: (j, 0))],
        out_specs=pl.BlockSpec((k, blk), lambda j: (0, j)),
        out_shape=jax.ShapeDtypeStruct((k, n_rows), BF16),
        name="cast_bf16",
    )(wt)


def _plain_cast_kernel(i_ref, o_ref):
    o_ref[...] = i_ref[...].astype(o_ref.dtype)


def _rows_bf16(w, blk):
    n, k = w.shape
    return pl.pallas_call(
        _plain_cast_kernel,
        grid=(n // blk,),
        in_specs=[pl.BlockSpec((blk, k), lambda j: (j, 0))],
        out_specs=pl.BlockSpec((blk, k), lambda j: (j, 0)),
        out_shape=jax.ShapeDtypeStruct((n, k), BF16),
        name="cast_rows_bf16",
    )(w)


def _copy_kernel(i_ref, o_ref):
    o_ref[...] = i_ref[...]


def _row_block(wt, blk_index, blk):
    k = wt.shape[1]
    return pl.pallas_call(
        _copy_kernel,
        grid=(1,),
        in_specs=[pl.BlockSpec((blk, k), lambda j: (blk_index, 0))],
        out_specs=pl.BlockSpec((blk, k), lambda j: (0, 0)),
        out_shape=jax.ShapeDtypeStruct((blk, k), wt.dtype),
        name="row_block",
    )(wt)


def _seg_scan(x, tl, seg, combine, fill):
    k = 1
    while k < seg:
        shifted = pltpu.roll(x, k, 1)
        x = combine(x, jnp.where(tl >= k, shifted, fill))
        k *= 2
    return x


def _lanes_of(per_batch_cols, TB):
    parts = [jnp.broadcast_to(c, (SUB, TB)) for c in per_batch_cols]
    return parts[0] if len(parts) == 1 else jnp.concatenate(parts, axis=1)


def _layer_kernel(*refs, BB, TB, start, has_state):
    R = BB * TB
    o_gate_early = TB < MLSTM_CHUNK
    D, W, DH, GW = D_MODEL, W_POOL, HEAD_DIM, POOL_GW
    it = iter(refs)
    x_ref = next(it)
    mod_ref = next(it)
    if has_state:
        pool0_ref, C0_ref, n0_ref, m0_ref = next(it), next(it), next(it), next(it)
    gnorm_ref, win_ref, wg_ref, gbias_ref = next(it), next(it), next(it), next(it)
    wpool_ref, pscale_ref, ghead_ref, wout_ref, gfinal_ref = (next(it), next(it), next(it),
                                                             next(it), next(it))
    y_ref, pool_out_ref, C_ref, n_ref, m_ref = next(it), next(it), next(it), next(it), next(it)
    h_s, ext_s, pooled_s, zp_s, hm_s, q_s, k_s, v_s, ycat_s, gain_s, og_s = it

    t = pl.program_id(1)

    @pl.when(t == 0)
    def _init():
        if has_state:
            ext_s[:, 0:1, :] = jnp.zeros((BB, 1, W), F32)
            ext_s[:, 1:HIST, :] = pool0_ref[...]
            C_ref[...] = C0_ref[...]
            n_ref[...] = n0_ref[...]
            m_ref[...] = m0_ref[...]
        else:
            ext_s[:, 0:HIST, :] = jnp.zeros((BB, HIST, W), F32)
            C_ref[...] = jnp.zeros_like(C_ref)
            n_ref[...] = jnp.zeros_like(n_ref)
            m_ref[...] = jnp.zeros_like(m_ref)
        gain_s[...] = gnorm_ref[...] * (1.0 + mod_ref[:, :, D:2 * D])

    for bb in range(BB):
        x = x_ref[bb]
        r = lax.rsqrt(jnp.mean(x * x, axis=-1, keepdims=True) + EPS)
        shift = mod_ref[bb, :, 0:D]
        h = (x * r) * gain_s[bb] + shift
        h_s[bb * TB:(bb + 1) * TB, :] = h.astype(BF16)

    def proj(sec):
        return jnp.dot(h_s[...], win_ref[:, sec * W:(sec + 1) * W], preferred_element_type=F32)

    gates = jnp.dot(h_s[...], wg_ref[...], preferred_element_type=F32) + gbias_ref[...]

    g8 = gates.T[0:SUB, :]
    tl = lax.broadcasted_iota(jnp.int32, (SUB, R), 1) & (TB - 1)
    ig = g8
    lf = _log_sigmoid(pltpu.roll(g8, SUB - N_HEADS, 0))
    Fc = _seg_scan(lf, tl, TB, jnp.add, 0.0)
    a = ig - Fc
    cmax = _seg_scan(a, tl, TB, jnp.maximum, -jnp.inf)
    m0_cols = [m_ref[bb][:, 0:1] for bb in range(BB)]
    m0 = _lanes_of(m0_cols, TB)
    m = Fc + jnp.maximum(m0, cmax)
    Fl_cols = [Fc[:, (bb + 1) * TB - 1:(bb + 1) * TB] for bb in range(BB)]
    mL_cols = [m[:, (bb + 1) * TB - 1:(bb + 1) * TB] for bb in range(BB)]
    Fl = _lanes_of(Fl_cols, TB)
    mL = _lanes_of(mL_cols, TB)
    per_row = jnp.concatenate(
        [Fc - m, jnp.exp(m0 + Fc - m), jnp.exp(a + Fl - mL), jnp.exp(-m),
         jnp.zeros((LANES - 4 * SUB, R), F32)], axis=0).T
    Fm, decay0 = per_row[:, 0:SUB], per_row[:, SUB:2 * SUB]
    wL, emm = per_row[:, 2 * SUB:3 * SUB], per_row[:, 3 * SUB:4 * SUB]

    xp = proj(0)
    for bb in range(BB):
        ext_s[bb, HIST:HIST + TB, :] = xp[bb * TB:(bb + 1) * TB, :]

    pos_head = start + t * TB + lax.broadcasted_iota(jnp.int32, (HIST, 1), 0)
    for g, w in enumerate(POOL_WINDOWS):
        cols = slice(g * GW, (g + 1) * GW)
        inv_head = 1.0 / jnp.minimum(pos_head + 1, w).astype(F32)
        for bb in range(BB):
            ext = ext_s[bb, :, cols]
            win = ext
            s = 1
            while s < w:
                win = win + pltpu.roll(win, s, 0)
                s *= 2
            r0 = bb * TB
            head = win[HIST:2 * HIST, :] * inv_head - ext[HIST:2 * HIST, :]
            tail = win[2 * HIST:, :] * (1.0 / w) - ext[2 * HIST:, :]
            pooled_s[r0:r0 + HIST, cols] = head.astype(BF16)
            pooled_s[r0 + HIST:r0 + TB, cols] = tail.astype(BF16)

    zp_s[...] = _silu(proj(1))
    for g in range(N_POOL_GROUPS):
        cols = slice(g * GW, (g + 1) * GW)
        mixed = jnp.dot(pooled_s[:, cols], wpool_ref[g], preferred_element_type=F32)
        ycat_s[:, cols] = (mixed * pscale_ref[:, cols] * zp_s[:, cols]).astype(BF16)
    q_s[...] = proj(2).astype(BF16)
    k_s[...] = (proj(3) * (DH ** -0.5)).astype(BF16)
    v_s[...] = proj(4).astype(BF16)
    if o_gate_early:
        og_s[...] = _sigmoid(proj(5))

    pool_out_ref[...] = ext_s[:, TB + 1:TB + HIST, :]

    for bb in range(BB):
        ext_s[bb, 0:HIST, :] = ext_s[bb, TB:TB + HIST, :]

    causal = (lax.broadcasted_iota(jnp.int32, (TB, TB), 0)
              >= lax.broadcasted_iota(jnp.int32, (TB, TB), 1))

    for bb in range(BB):
        rows = slice(bb * TB, (bb + 1) * TB)
        dL = jnp.exp(m0_cols[bb] + Fl_cols[bb] - mL_cols[bb])
        for hd in range(N_HEADS):
            cols = slice(hd * DH, (hd + 1) * DH)
            q = q_s[rows, cols]
            k = k_s[rows, cols]
            v = v_s[rows, cols]
            s = lax.dot_general(q, k, (((1,), (1,)), ((), ())), preferred_element_type=F32)
            logD = Fm[rows, hd:hd + 1] + a[hd:hd + 1, rows]
            S = s * jnp.exp(jnp.where(causal, logD, -jnp.inf))
            C0 = C_ref[bb, hd]
            n0 = n_ref[bb, hd:hd + 1, :]
            d0 = decay0[rows, hd:hd + 1]
            qn = jnp.sum(q.astype(F32) * n0, axis=-1, keepdims=True)
            nq = jnp.sum(S, axis=-1, keepdims=True) + d0 * qn
            num = (jnp.dot(S.astype(BF16), v, preferred_element_type=F32)
                   + d0 * jnp.dot(q, C0.astype(BF16), preferred_element_type=F32))
            den = jnp.maximum(jnp.abs(nq), emm[rows, hd:hd + 1])
            hh = num * (1.0 / den)
            hc = hh - jnp.mean(hh, axis=-1, keepdims=True)
            hn = hc * lax.rsqrt(jnp.mean(hc * hc, axis=-1, keepdims=True) + EPS)
            hm = hn * ghead_ref[:, cols]
            hm_s[rows, cols] = hm * og_s[rows, cols] if o_gate_early else hm
            kw = k.astype(F32) * wL[rows, hd:hd + 1]
            dl = dL[hd:hd + 1, :]
            C_ref[bb, hd] = dl * C0 + lax.dot_general(
                kw.astype(BF16), v, (((0,), (0,)), ((), ())), preferred_element_type=F32)
            n_ref[bb, hd:hd + 1, :] = dl * n0 + jnp.sum(kw, axis=0, keepdims=True)
        m_ref[bb] = jnp.broadcast_to(mL_cols[bb], (SUB, LANES))

        if not o_gate_early:
            hb = h_s[rows, :]
            o = jnp.dot(hb, win_ref[:, 5 * W:6 * W], preferred_element_type=F32)
            hm_s[rows, :] = hm_s[rows, :] * _sigmoid(o)
            zm = jnp.dot(hb, win_ref[:, 6 * W:7 * W], preferred_element_type=F32)
            ycat_s[rows, W:2 * W] = (hm_s[rows, :] * _silu(zm)).astype(BF16)

    if o_gate_early:
        ycat_s[:, W:2 * W] = (hm_s[...] * _silu(proj(6))).astype(BF16)

    out_cols = OUT_COLS if R >= MIN_ROWS_FOR_COLUMN_BLOCKS else D
    ssq = [jnp.zeros((TB, 1), F32) for _ in range(BB)]
    for c0 in range(0, D, out_cols):
        cols = slice(c0, c0 + out_cols)
        y = jnp.dot(ycat_s[...], wout_ref[:, cols], preferred_element_type=F32)
        for bb in range(BB):
            gate = mod_ref[bb, :, 2 * D + c0:2 * D + c0 + out_cols]
            xn = x_ref[bb, :, cols] + gate * y[bb * TB:(bb + 1) * TB, :]
            ssq[bb] = ssq[bb] + jnp.sum(xn * xn, axis=-1, keepdims=True)
            y_ref[bb, :, cols] = xn
    for bb in range(BB):
        r = lax.rsqrt(ssq[bb] * (1.0 / D) + EPS)
        y_ref[bb] = (y_ref[bb] * r) * gfinal_ref[...]


def _const_spec(shape):
    nd = len(shape)
    return pl.BlockSpec(shape, lambda b, t: (0,) * nd, pipeline_mode=pl.Buffered(1))


def _block_plan(n_batch, n_tokens, has_state):
    tb = min(n_tokens, MLSTM_CHUNK)
    c_bytes = N_HEADS * HEAD_DIM * HEAD_DIM * 4
    n_windows = 4 if has_state else 2
    bb = min(n_batch, MAX_STEP_ROWS // tb, STATE_WINDOW_BYTES // (n_windows * c_bytes))
    assert n_tokens % tb == 0 and n_batch % bb == 0 and tb >= 2 * HIST
    return bb, tb


def _run_layer(x, mod, mod_row0, state, weights, *, start):
    B, T, D = x.shape
    W, H, DH = W_POOL, N_HEADS, HEAD_DIM
    has_state = state is not None
    BB, TB = _block_plan(B, T, has_state)
    R = BB * TB
    grid = (B // BB, T // TB)
    assert mod_row0 % BB == 0
    mod_blk0 = mod_row0 // BB

    in_specs = [
        pl.BlockSpec((BB, TB, D), lambda b, t: (b, t, 0)),
        pl.BlockSpec((BB, 1, 3 * D), lambda b, t: (b + mod_blk0, 0, 0)),
    ]
    args = [x, mod]
    if has_state:
        pool0, C0, n0, m0 = state
        in_specs += [
            pl.BlockSpec((BB, POOL_BUF, W), lambda b, t: (b, 0, 0)),
            pl.BlockSpec((BB, H, DH, DH), lambda b, t: (b, 0, 0, 0)),
            pl.BlockSpec((BB, H, DH), lambda b, t: (b, 0, 0)),
            pl.BlockSpec((BB, SUB, LANES), lambda b, t: (b, 0, 0)),
        ]
        m0p = jnp.broadcast_to(jnp.pad(m0, ((0, 0), (0, SUB - H)))[:, :, None], (B, SUB, LANES))
        args += [pool0, C0, n0, m0p]
    in_specs += [_const_spec(w.shape) for w in weights]
    args += list(weights)

    out_shape = (
        jax.ShapeDtypeStruct((B, T, D), F32),
        jax.ShapeDtypeStruct((B, POOL_BUF, W), F32),
        jax.ShapeDtypeStruct((B, H, DH, DH), F32),
        jax.ShapeDtypeStruct((B, H, DH), F32),
        jax.ShapeDtypeStruct((B, SUB, LANES), F32),
    )
    out_specs = (
        pl.BlockSpec((BB, TB, D), lambda b, t: (b, t, 0)),
        pl.BlockSpec((BB, POOL_BUF, W), lambda b, t: (b, 0, 0)),
        pl.BlockSpec((BB, H, DH, DH), lambda b, t: (b, 0, 0, 0)),
        pl.BlockSpec((BB, H, DH), lambda b, t: (b, 0, 0)),
        pl.BlockSpec((BB, SUB, LANES), lambda b, t: (b, 0, 0)),
    )
    scratch = [
        pltpu.VMEM((R, D), BF16),
        pltpu.VMEM((BB, HIST + TB, W), F32),
        pltpu.VMEM((R, W), BF16),
        pltpu.VMEM((R, W), F32),
        pltpu.VMEM((R, W), F32),
        pltpu.VMEM((R, W), BF16),
        pltpu.VMEM((R, W), BF16),
        pltpu.VMEM((R, W), BF16),
        pltpu.VMEM((R, 2 * W), BF16),
        pltpu.VMEM((BB, 1, D), F32),
        pltpu.VMEM((R, W) if TB < MLSTM_CHUNK else (SUB, LANES), F32),
    ]
    y, pool, C, n, m = pl.pallas_call(
        functools.partial(_layer_kernel, BB=BB, TB=TB, start=start, has_state=has_state),
        grid=grid,
        in_specs=in_specs,
        out_specs=out_specs,
        out_shape=out_shape,
        scratch_shapes=scratch,
        compiler_params=pltpu.CompilerParams(
            dimension_semantics=("parallel", "arbitrary"),
            vmem_limit_bytes=VMEM_LIMIT_BYTES),
        name="layer_state" if has_state else "layer_fresh",
    )(*args)
    return y, pool, C, n, m[:, :H, 0]


def kernel(x_prompt, x_sample, c_prompt, c_sample, state_pool, state_C, state_n, state_m,
           w_ada, b_ada, g_norm, w_in, b_i, b_f, w_pool, pool_scale, g_head, w_out, g_final):
    depth = w_ada.shape[0]
    assert depth == 1, "single-layer trunk"
    l = 0
    nbp = x_prompt.shape[0]
    n_main = N_MAIN_SECTIONS * W_POOL

    mod = _adaln_mod(jnp.concatenate([c_prompt, c_sample], axis=0), w_ada[l], b_ada[l])

    w_in_t = jnp.swapaxes(w_in[l], 0, 1)
    n_gate = 2 * N_HEADS
    assert n_main % n_gate == 0 and w_in_t.shape[0] == n_main + n_gate
    w_gate_t = _row_block(w_in_t, n_main // n_gate, n_gate)
    w_gate = jnp.pad(w_gate_t.T, ((0, 0), (0, LANES - n_gate))).astype(BF16)
    gate_bias = jnp.pad(jnp.concatenate([b_i[l], b_f[l]]), (0, LANES - 2 * N_HEADS)).reshape(1, LANES)
    weights = (
        g_norm[l].reshape(1, D_MODEL),
        _transposed_rows_bf16(w_in_t, n_main, W_POOL),
        w_gate,
        gate_bias,
        w_pool[l].astype(BF16),
        pool_scale[l].reshape(1, W_POOL),
        g_head[l].reshape(1, W_MLSTM),
        _rows_bf16(w_out[l], 512),
        g_final.reshape(1, D_MODEL),
    )

    yp, pp, pc, pn, pm = _run_layer(x_prompt, mod, 0, None, weights, start=0)
    ys, sp, sc, sn, sm = _run_layer(
        x_sample, mod, nbp, (state_pool[l], state_C[l], state_n[l], state_m[l]), weights,
        start=PAST_LEN)
    return (yp, ys, pp[None], pc[None], pn[None], pm[None], sp[None], sc[None], sn[None], sm[None])
```

```python
import functools

import jax
import jax.numpy as jnp
from jax import lax
from jax.experimental import pallas as pl
from jax.experimental.pallas import tpu as pltpu

F32 = jnp.float32
BF16 = jnp.bfloat16

D_MODEL = 1024
W_POOL = 1024
W_MLSTM = 1024
POOL_WINDOWS = (2, 4, 8, 16)
N_POOL_GROUPS = 4
POOL_GW = W_POOL // N_POOL_GROUPS
POOL_BUF = 15
N_HEADS = 4
HEAD_DIM = W_MLSTM // N_HEADS
EPS = 1e-6
N_MAIN_SECTIONS = 7
LANES = 128
SUB = 8
PAST_LEN = 2048
HIST = 16
OUT_COLS = 512
MIN_ROWS_FOR_COLUMN_BLOCKS = 256
VMEM_LIMIT_BYTES = 56 * 1024 * 1024
MLSTM_CHUNK = 256
MAX_STEP_ROWS = 512
STATE_WINDOW_BYTES = 16 * 1024 * 1024


def _sigmoid(z):
    return 0.5 * jnp.tanh(0.5 * z) + 0.5


def _silu(z):
    hz = 0.5 * z
    return hz * jnp.tanh(hz) + hz


def _log_sigmoid(z):
    return jnp.minimum(z, 0.0) - jnp.log1p(jnp.exp(-jnp.abs(z)))


def _mod_kernel(c_ref, w_ref, b_ref, o_ref):
    c = c_ref[...]
    mod = jnp.dot(_silu(c), w_ref[...], preferred_element_type=F32) + b_ref[...]
    for b in range(mod.shape[0]):
        o_ref[b] = mod[b:b + 1, :]


def _adaln_mod(c, w_ada, b_ada):
    nb = c.shape[0]
    n_out = w_ada.shape[1]
    blk = D_MODEL // 2
    return pl.pallas_call(
        _mod_kernel,
        grid=(n_out // blk,),
        in_specs=[
            pl.BlockSpec((nb, D_MODEL), lambda j: (0, 0)),
            pl.BlockSpec((D_MODEL, blk), lambda j: (0, j)),
            pl.BlockSpec((1, blk), lambda j: (0, j)),
        ],
        out_specs=pl.BlockSpec((nb, 1, blk), lambda j: (0, 0, j)),
        out_shape=jax.ShapeDtypeStruct((nb, 1, n_out), F32),
        name="adaln_mod",
    )(c, w_ada, b_ada.reshape(1, n_out))


def _cast_kernel(i_ref, o_ref):
    o_ref[...] = i_ref[...].T.astype(o_ref.dtype)


def _transposed_rows_bf16(wt, n_rows, blk):
    k = wt.shape[1]
    return pl.pallas_call(
        _cast_kernel,
        grid=(n_rows // blk,),
        in_specs=[pl.BlockSpec((blk, k), lambda j: (j, 0))],
        out_specs=pl.BlockSpec((k, blk), lambda j: (0, j)),
        out_shape=jax.ShapeDtypeStruct((k, n_rows), BF16),
        name="cast_bf16",
    )(wt)


def _plain_cast_kernel(i_ref, o_ref):
    o_ref[...] = i_ref[...].astype(o_ref.dtype)


def _rows_bf16(w, blk):
    n, k = w.shape
    return pl.pallas_call(
        _plain_cast_kernel,
        grid=(n // blk,),
        in_specs=[pl.BlockSpec((blk, k), lambda j: (j, 0))],
        out_specs=pl.BlockSpec((blk, k), lambda j: (j, 0)),
        out_shape=jax.ShapeDtypeStruct((n, k), BF16),
        name="cast_rows_bf16",
    )(w)


def _copy_kernel(i_ref, o_ref):
    o_ref[...] = i_ref[...]


def _row_block(wt, blk_index, blk):
    k = wt.shape[1]
    return pl.pallas_call(
        _copy_kernel,
        grid=(1,),
        in_specs=[pl.BlockSpec((blk, k), lambda j: (blk_index, 0))],
        out_specs=pl.BlockSpec((blk, k), lambda j: (0, 0)),
        out_shape=jax.ShapeDtypeStruct((blk, k), wt.dtype),
        name="row_block",
    )(wt)


def _seg_scan(x, tl, seg, combine, fill):
    k = 1
    while k < seg:
        shifted = pltpu.roll(x, k, 1)
        x = combine(x, jnp.where(tl >= k, shifted, fill))
        k *= 2
    return x


def _lanes_of(per_batch_cols, TB):
    parts = [jnp.broadcast_to(c, (SUB, TB)) for c in per_batch_cols]
    return parts[0] if len(parts) == 1 else jnp.concatenate(parts, axis=1)


def _layer_kernel(*refs, BB, TB, start, has_state):
    R = BB * TB
    o_gate_early = TB < MLSTM_CHUNK
    D, W, DH, GW = D_MODEL, W_POOL, HEAD_DIM, POOL_GW
    it = iter(refs)
    x_ref = next(it)
    mod_ref = next(it)
    if has_state:
        pool0_ref, C0_ref, n0_ref, m0_ref = next(it), next(it), next(it), next(it)
    gnorm_ref, win_ref, wg_ref, gbias_ref = next(it), next(it), next(it), next(it)
    wpool_ref, pscale_ref, ghead_ref, wout_ref, gfinal_ref = (next(it), next(it), next(it),
                                                             next(it), next(it))
    y_ref, pool_out_ref, C_ref, n_ref, m_ref = next(it), next(it), next(it), next(it), next(it)
    h_s, ext_s, pooled_s, zp_s, hm_s, q_s, k_s, v_s, ycat_s, gain_s, og_s = it

    t = pl.program_id(1)

    @pl.when(t == 0)
    def _init():
        if has_state:
            ext_s[:, 0:1, :] = jnp.zeros((BB, 1, W), F32)
            ext_s[:, 1:HIST, :] = pool0_ref[...]
            C_ref[...] = C0_ref[...]
            n_ref[...] = n0_ref[...]
            m_ref[...] = m0_ref[...]
        else:
            ext_s[:, 0:HIST, :] = jnp.zeros((BB, HIST, W), F32)
            C_ref[...] = jnp.zeros_like(C_ref)
            n_ref[...] = jnp.zeros_like(n_ref)
            m_ref[...] = jnp.zeros_like(m_ref)
        gain_s[...] = gnorm_ref[...] * (1.0 + mod_ref[:, :, D:2 * D])

    for bb in range(BB):
        x = x_ref[bb]
        r = lax.rsqrt(jnp.mean(x * x, axis=-1, keepdims=True) + EPS)
        shift = mod_ref[bb, :, 0:D]
        h = (x * r) * gain_s[bb] + shift
        h_s[bb * TB:(bb + 1) * TB, :] = h.astype(BF16)

    def proj(sec):
        return jnp.dot(h_s[...], win_ref[:, sec * W:(sec + 1) * W], preferred_element_type=F32)

    gates = jnp.dot(h_s[...], wg_ref[...], preferred_element_type=F32) + gbias_ref[...]

    g8 = gates.T[0:SUB, :]
    tl = lax.broadcasted_iota(jnp.int32, (SUB, R), 1) & (TB - 1)
    ig = g8
    lf = _log_sigmoid(pltpu.roll(g8, SUB - N_HEADS, 0))
    Fc = _seg_scan(lf, tl, TB, jnp.add, 0.0)
    a = ig - Fc
    cmax = _seg_scan(a, tl, TB, jnp.maximum, -jnp.inf)
    m0_cols = [m_ref[bb][:, 0:1] for bb in range(BB)]
    m0 = _lanes_of(m0_cols, TB)
    m = Fc + jnp.maximum(m0, cmax)
    Fl_cols = [Fc[:, (bb + 1) * TB - 1:(bb + 1) * TB] for bb in range(BB)]
    mL_cols = [m[:, (bb + 1) * TB - 1:(bb + 1) * TB] for bb in range(BB)]
    Fl = _lanes_of(Fl_cols, TB)
    mL = _lanes_of(mL_cols, TB)
    per_row = jnp.concatenate(
        [Fc - m, jnp.exp(m0 + Fc - m), jnp.exp(a + Fl - mL), jnp.exp(-m),
         jnp.zeros((LANES - 4 * SUB, R), F32)], axis=0).T
    Fm, decay0 = per_row[:, 0:SUB], per_row[:, SUB:2 * SUB]
    wL, emm = per_row[:, 2 * SUB:3 * SUB], per_row[:, 3 * SUB:4 * SUB]

    xp = proj(0)
    for bb in range(BB):
        ext_s[bb, HIST:HIST + TB, :] = xp[bb * TB:(bb + 1) * TB, :]

    pos_head = start + t * TB + lax.broadcasted_iota(jnp.int32, (HIST, 1), 0)
    for g, w in enumerate(POOL_WINDOWS):
        cols = slice(g * GW, (g + 1) * GW)
        inv_head = 1.0 / jnp.minimum(pos_head + 1, w).astype(F32)
        for bb in range(BB):
            ext = ext_s[bb, :, cols]
            win = ext
            s = 1
            while s < w:
                win = win + pltpu.roll(win, s, 0)
                s *= 2
            r0 = bb * TB
            head = win[HIST:2 * HIST, :] * inv_head - ext[HIST:2 * HIST, :]
            tail = win[2 * HIST:, :] * (1.0 / w) - ext[2 * HIST:, :]
            pooled_s[r0:r0 + HIST, cols] = head.astype(BF16)
            pooled_s[r0 + HIST:r0 + TB, cols] = tail.astype(BF16)

    zp_s[...] = _silu(proj(1))
    for g in range(N_POOL_GROUPS):
        cols = slice(g * GW, (g + 1) * GW)
        mixed = jnp.dot(pooled_s[:, cols], wpool_ref[g], preferred_element_type=F32)
        ycat_s[:, cols] = (mixed * pscale_ref[:, cols] * zp_s[:, cols]).astype(BF16)
    q_s[...] = proj(2).astype(BF16)
    k_s[...] = (proj(3) * (DH ** -0.5)).astype(BF16)
    v_s[...] = proj(4).astype(BF16)
    if o_gate_early:
        og_s[...] = _sigmoid(proj(5))

    pool_out_ref[...] = ext_s[:, TB + 1:TB + HIST, :]

    for bb in range(BB):
        ext_s[bb, 0:HIST, :] = ext_s[bb, TB:TB + HIST, :]

    causal = (lax.broadcasted_iota(jnp.int32, (TB, TB), 0)
              >= lax.broadcasted_iota(jnp.int32, (TB, TB), 1))

    for bb in range(BB):
        rows = slice(bb * TB, (bb + 1) * TB)
        dL = jnp.exp(m0_cols[bb] + Fl_cols[bb] - mL_cols[bb])
        for hd in range(N_HEADS):
            cols = slice(hd * DH, (hd + 1) * DH)
            q = q_s[rows, cols]
            k = k_s[rows, cols]
            v = v_s[rows, cols]
            s = lax.dot_general(q, k, (((1,), (1,)), ((), ())), preferred_element_type=F32)
            logD = Fm[rows, hd:hd + 1] + a[hd:hd + 1, rows]
            S = s * jnp.exp(jnp.where(causal, logD, -jnp.inf))
            C0 = C_ref[bb, hd]
            n0 = n_ref[bb, hd:hd + 1, :]
            d0 = decay0[rows, hd:hd + 1]
            qn = jnp.sum(q.astype(F32) * n0, axis=-1, keepdims=True)
            nq = jnp.sum(S, axis=-1, keepdims=True) + d0 * qn
            num = (jnp.dot(S.astype(BF16), v, preferred_element_type=F32)
                   + d0 * jnp.dot(q, C0.astype(BF16), preferred_element_type=F32))
            den = jnp.maximum(jnp.abs(nq), emm[rows, hd:hd + 1])
            hh = num * (1.0 / den)
            hc = hh - jnp.mean(hh, axis=-1, keepdims=True)
            hn = hc * lax.rsqrt(jnp.mean(hc * hc, axis=-1, keepdims=True) + EPS)
            hm = hn * ghead_ref[:, cols]
            hm_s[rows, cols] = hm * og_s[rows, cols] if o_gate_early else hm
            kw = k.astype(F32) * wL[rows, hd:hd + 1]
            dl = dL[hd:hd + 1, :]
            C_ref[bb, hd] = dl * C0 + lax.dot_general(
                kw.astype(BF16), v, (((0,), (0,)), ((), ())), preferred_element_type=F32)
            n_ref[bb, hd:hd + 1, :] = dl * n0 + jnp.sum(kw, axis=0, keepdims=True)
        m_ref[bb] = jnp.broadcast_to(mL_cols[bb], (SUB, LANES))

    if not o_gate_early:
        hm_s[...] = hm_s[...] * _sigmoid(proj(5))
    ycat_s[:, W:2 * W] = (hm_s[...] * _silu(proj(6))).astype(BF16)

    out_cols = OUT_COLS if R >= MIN_ROWS_FOR_COLUMN_BLOCKS else D
    ssq = [jnp.zeros((TB, 1), F32) for _ in range(BB)]
    for c0 in range(0, D, out_cols):
        cols = slice(c0, c0 + out_cols)
        y = jnp.dot(ycat_s[...], wout_ref[:, cols], preferred_element_type=F32)
        for bb in range(BB):
            gate = mod_ref[bb, :, 2 * D + c0:2 * D + c0 + out_cols]
            xn = x_ref[bb, :, cols] + gate * y[bb * TB:(bb + 1) * TB, :]
            ssq[bb] = ssq[bb] + jnp.sum(xn * xn, axis=-1, keepdims=True)
            y_ref[bb, :, cols] = xn
    for bb in range(BB):
        r = lax.rsqrt(ssq[bb] * (1.0 / D) + EPS)
        y_ref[bb] = (y_ref[bb] * r) * gfinal_ref[...]


def _const_spec(shape):
    nd = len(shape)
    return pl.BlockSpec(shape, lambda b, t: (0,) * nd, pipeline_mode=pl.Buffered(1))


def _block_plan(n_batch, n_tokens, has_state):
    tb = min(n_tokens, MLSTM_CHUNK)
    c_bytes = N_HEADS * HEAD_DIM * HEAD_DIM * 4
    n_windows = 4 if has_state else 2
    bb = min(n_batch, MAX_STEP_ROWS // tb, STATE_WINDOW_BYTES // (n_windows * c_bytes))
    assert n_tokens % tb == 0 and n_batch % bb == 0 and tb >= 2 * HIST
    return bb, tb


def _run_layer(x, mod, mod_row0, state, weights, *, start):
    B, T, D = x.shape
    W, H, DH = W_POOL, N_HEADS, HEAD_DIM
    has_state = state is not None
    BB, TB = _block_plan(B, T, has_state)
    R = BB * TB
    grid = (B // BB, T // TB)
    assert mod_row0 % BB == 0
    mod_blk0 = mod_row0 // BB

    in_specs = [
        pl.BlockSpec((BB, TB, D), lambda b, t: (b, t, 0)),
        pl.BlockSpec((BB, 1, 3 * D), lambda b, t: (b + mod_blk0, 0, 0)),
    ]
    args = [x, mod]
    if has_state:
        pool0, C0, n0, m0 = state
        in_specs += [
            pl.BlockSpec((BB, POOL_BUF, W), lambda b, t: (b, 0, 0)),
            pl.BlockSpec((BB, H, DH, DH), lambda b, t: (b, 0, 0, 0)),
            pl.BlockSpec((BB, H, DH), lambda b, t: (b, 0, 0)),
            pl.BlockSpec((BB, SUB, LANES), lambda b, t: (b, 0, 0)),
        ]
        m0p = jnp.broadcast_to(jnp.pad(m0, ((0, 0), (0, SUB - H)))[:, :, None], (B, SUB, LANES))
        args += [pool0, C0, n0, m0p]
    in_specs += [_const_spec(w.shape) for w in weights]
    args += list(weights)

    out_shape = (
        jax.ShapeDtypeStruct((B, T, D), F32),
        jax.ShapeDtypeStruct((B, POOL_BUF, W), F32),
        jax.ShapeDtypeStruct((B, H, DH, DH), F32),
        jax.ShapeDtypeStruct((B, H, DH), F32),
        jax.ShapeDtypeStruct((B, SUB, LANES), F32),
    )
    out_specs = (
        pl.BlockSpec((BB, TB, D), lambda b, t: (b, t, 0)),
        pl.BlockSpec((BB, POOL_BUF, W), lambda b, t: (b, 0, 0)),
        pl.BlockSpec((BB, H, DH, DH), lambda b, t: (b, 0, 0, 0)),
        pl.BlockSpec((BB, H, DH), lambda b, t: (b, 0, 0)),
        pl.BlockSpec((BB, SUB, LANES), lambda b, t: (b, 0, 0)),
    )
    scratch = [
        pltpu.VMEM((R, D), BF16),
        pltpu.VMEM((BB, HIST + TB, W), F32),
        pltpu.VMEM((R, W), BF16),
        pltpu.VMEM((R, W), F32),
        pltpu.VMEM((R, W), F32),
        pltpu.VMEM((R, W), BF16),
        pltpu.VMEM((R, W), BF16),
        pltpu.VMEM((R, W), BF16),
        pltpu.VMEM((R, 2 * W), BF16),
        pltpu.VMEM((BB, 1, D), F32),
        pltpu.VMEM((R, W) if TB < MLSTM_CHUNK else (SUB, LANES), F32),
    ]
    y, pool, C, n, m = pl.pallas_call(
        functools.partial(_layer_kernel, BB=BB, TB=TB, start=start, has_state=has_state),
        grid=grid,
        in_specs=in_specs,
        out_specs=out_specs,
        out_shape=out_shape,
        scratch_shapes=scratch,
        compiler_params=pltpu.CompilerParams(
            dimension_semantics=("parallel", "arbitrary"),
            vmem_limit_bytes=VMEM_LIMIT_BYTES),
        name="layer_state" if has_state else "layer_fresh",
    )(*args)
    return y, pool, C, n, m[:, :H, 0]


def kernel(x_prompt, x_sample, c_prompt, c_sample, state_pool, state_C, state_n, state_m,
           w_ada, b_ada, g_norm, w_in, b_i, b_f, w_pool, pool_scale, g_head, w_out, g_final):
    depth = w_ada.shape[0]
    assert depth == 1, "single-layer trunk"
    l = 0
    nbp = x_prompt.shape[0]
    n_main = N_MAIN_SECTIONS * W_POOL

    mod = _adaln_mod(jnp.concatenate([c_prompt, c_sample], axis=0), w_ada[l], b_ada[l])

    w_in_t = jnp.swapaxes(w_in[l], 0, 1)
    n_gate = 2 * N_HEADS
    assert n_main % n_gate == 0 and w_in_t.shape[0] == n_main + n_gate
    w_gate_t = _row_block(w_in_t, n_main // n_gate, n_gate)
    w_gate = jnp.pad(w_gate_t.T, ((0, 0), (0, LANES - n_gate))).astype(BF16)
    gate_bias = jnp.pad(jnp.concatenate([b_i[l], b_f[l]]), (0, LANES - 2 * N_HEADS)).reshape(1, LANES)
    weights = (
        g_norm[l].reshape(1, D_MODEL),
        _transposed_rows_bf16(w_in_t, n_main, W_POOL),
        w_gate,
        gate_bias,
        w_pool[l].astype(BF16),
        pool_scale[l].reshape(1, W_POOL),
        g_head[l].reshape(1, W_MLSTM),
        _rows_bf16(w_out[l], 512),
        g_final.reshape(1, D_MODEL),
    )

    yp, pp, pc, pn, pm = _run_layer(x_prompt, mod, 0, None, weights, start=0)
    ys, sp, sc, sn, sm = _run_layer(
        x_sample, mod, nbp, (state_pool[l], state_C[l], state_n[l], state_m[l]), weights,
        start=PAST_LEN)
    return (yp, ys, pp[None], pc[None], pn[None], pm[None], sp[None], sc[None], sn[None], sm[None])
```

```python
import functools

import jax
import jax.numpy as jnp
from jax import lax
from jax.experimental import pallas as pl
from jax.experimental.pallas import tpu as pltpu

F32 = jnp.float32
BF16 = jnp.bfloat16

D_MODEL = 1024
W_POOL = 1024
W_MLSTM = 1024
POOL_WINDOWS = (2, 4, 8, 16)
N_POOL_GROUPS = 4
POOL_GW = W_POOL // N_POOL_GROUPS
POOL_BUF = 15
N_HEADS = 4
HEAD_DIM = W_MLSTM // N_HEADS
EPS = 1e-6
N_MAIN_SECTIONS = 7
LANES = 128
SUB = 8
BF16_SUBLANES = 16
PAST_LEN = 2048
HIST = 16
OUT_COLS = 256
MIN_ROWS_FOR_COLUMN_BLOCKS = 256
VMEM_LIMIT_BYTES = 56 * 1024 * 1024
MLSTM_CHUNK = 256
MAX_STEP_ROWS = 512
STATE_WINDOW_BYTES = 16 * 1024 * 1024


def _sigmoid(z):
    return 0.5 * jnp.tanh(0.5 * z) + 0.5


def _silu(z):
    hz = 0.5 * z
    return hz * jnp.tanh(hz) + hz


def _log_sigmoid(z):
    return jnp.minimum(z, 0.0) - jnp.log1p(jnp.exp(-jnp.abs(z)))


def _mod_kernel(c_ref, w_ref, b_ref, o_ref):
    c = c_ref[...]
    mod = jnp.dot(_silu(c), w_ref[...], preferred_element_type=F32) + b_ref[...]
    for b in range(mod.shape[0]):
        o_ref[b] = mod[b:b + 1, :]


def _adaln_mod(c, w_ada, b_ada):
    nb = c.shape[0]
    n_out = w_ada.shape[1]
    blk = D_MODEL // 2
    return pl.pallas_call(
        _mod_kernel,
        grid=(n_out // blk,),
        in_specs=[
            pl.BlockSpec((nb, D_MODEL), lambda j: (0, 0)),
            pl.BlockSpec((D_MODEL, blk), lambda j: (0, j)),
            pl.BlockSpec((1, blk), lambda j: (0, j)),
        ],
        out_specs=pl.BlockSpec((nb, 1, blk), lambda j: (0, 0, j)),
        out_shape=jax.ShapeDtypeStruct((nb, 1, n_out), F32),
        name="adaln_mod",
    )(c, w_ada, b_ada.reshape(1, n_out))


def _cast_kernel(i_ref, o_ref):
    o_ref[...] = i_ref[...].T.astype(o_ref.dtype)


def _transposed_rows_bf16(wt, n_rows, blk):
    k = wt.shape[1]
    return pl.pallas_call(
        _cast_kernel,
        grid=(n_rows // blk,),
        in_specs=[pl.BlockSpec((blk, k), lambda j: (j, 0))],
        out_specs=pl.BlockSpec((k, blk), lambda j: (0, j)),
        out_shape=jax.ShapeDtypeStruct((k, n_rows), BF16),
        name="cast_bf16",
    )(wt)


def _plain_cast_kernel(i_ref, o_ref):
    o_ref[...] = i_ref[...].astype(o_ref.dtype)


def _rows_bf16(w, blk):
    n, k = w.shape
    return pl.pallas_call(
        _plain_cast_kernel,
        grid=(n // blk,),
        in_specs=[pl.BlockSpec((blk, k), lambda j: (j, 0))],
        out_specs=pl.BlockSpec((blk, k), lambda j: (j, 0)),
        out_shape=jax.ShapeDtypeStruct((n, k), BF16),
        name="cast_rows_bf16",
    )(w)


def _copy_kernel(i_ref, o_ref):
    o_ref[...] = i_ref[...]


def _row_block(wt, blk_index, blk):
    k = wt.shape[1]
    return pl.pallas_call(
        _copy_kernel,
        grid=(1,),
        in_specs=[pl.BlockSpec((blk, k), lambda j: (blk_index, 0))],
        out_specs=pl.BlockSpec((blk, k), lambda j: (0, 0)),
        out_shape=jax.ShapeDtypeStruct((blk, k), wt.dtype),
        name="row_block",
    )(wt)


def _seg_scan(x, tl, seg, combine, fill):
    k = 1
    while k < seg:
        shifted = pltpu.roll(x, k, 1)
        x = combine(x, jnp.where(tl >= k, shifted, fill))
        k *= 2
    return x


def _lanes_of(per_batch_cols, TB):
    parts = [jnp.broadcast_to(c, (SUB, TB)) for c in per_batch_cols]
    return parts[0] if len(parts) == 1 else jnp.concatenate(parts, axis=1)


def _layer_kernel(*refs, BB, TB, start, has_state):
    R = BB * TB
    o_gate_early = TB < MLSTM_CHUNK
    D, W, DH, GW = D_MODEL, W_POOL, HEAD_DIM, POOL_GW
    it = iter(refs)
    x_ref = next(it)
    mod_ref = next(it)
    if has_state:
        pool0_ref, C0_ref, n0_ref, m0_ref = next(it), next(it), next(it), next(it)
    gnorm_ref, win_ref, wg_ref, gbias_ref = next(it), next(it), next(it), next(it)
    wpool_ref, pscale_ref, ghead_ref, wout_ref, gfinal_ref = (next(it), next(it), next(it),
                                                             next(it), next(it))
    y_ref, pool_out_ref, C_ref, n_ref, m_ref = next(it), next(it), next(it), next(it), next(it)
    h_s, ext_s, pooled_s, zp_s, hm_s, q_s, k_s, v_s, ycat_s, gain_s, og_s = it

    t = pl.program_id(1)

    @pl.when(t == 0)
    def _init():
        if has_state:
            ext_s[:, 0:1, :] = jnp.zeros((BB, 1, W), F32)
            ext_s[:, 1:HIST, :] = pool0_ref[...]
            C_ref[...] = C0_ref[...]
            n_ref[...] = n0_ref[...]
            m_ref[...] = m0_ref[...]
        else:
            ext_s[:, 0:HIST, :] = jnp.zeros((BB, HIST, W), F32)
            C_ref[...] = jnp.zeros_like(C_ref)
            n_ref[...] = jnp.zeros_like(n_ref)
            m_ref[...] = jnp.zeros_like(m_ref)
        gain_s[...] = gnorm_ref[...] * (1.0 + mod_ref[:, :, D:2 * D])

    for bb in range(BB):
        x = x_ref[bb]
        r = lax.rsqrt(jnp.mean(x * x, axis=-1, keepdims=True) + EPS)
        shift = mod_ref[bb, :, 0:D]
        h = (x * r) * gain_s[bb] + shift
        h_s[bb * TB:(bb + 1) * TB, :] = h.astype(BF16)

    def proj(sec):
        return jnp.dot(h_s[...], win_ref[:, sec * W:(sec + 1) * W], preferred_element_type=F32)

    xp = proj(0)
    for bb in range(BB):
        ext_s[bb, HIST:HIST + TB, :] = xp[bb * TB:(bb + 1) * TB, :]

    pos_head = start + t * TB + lax.broadcasted_iota(jnp.int32, (HIST, 1), 0)
    for g, w in enumerate(POOL_WINDOWS):
        cols = slice(g * GW, (g + 1) * GW)
        inv_head = 1.0 / jnp.minimum(pos_head + 1, w).astype(F32)
        for bb in range(BB):
            ext = ext_s[bb, :, cols]
            win = ext
            s = 1
            while s < w:
                win = win + pltpu.roll(win, s, 0)
                s *= 2
            r0 = bb * TB
            head = win[HIST:2 * HIST, :] * inv_head - ext[HIST:2 * HIST, :]
            tail = win[2 * HIST:, :] * (1.0 / w) - ext[2 * HIST:, :]
            pooled_s[r0:r0 + HIST, cols] = head.astype(BF16)
            pooled_s[r0 + HIST:r0 + TB, cols] = tail.astype(BF16)

    zp_s[...] = _silu(proj(1))

    gates_t = lax.dot_general(wg_ref[...], h_s[...], (((1,), (1,)), ((), ())),
                              preferred_element_type=F32)
    g8 = gates_t[0:SUB, :] + gbias_ref[:, 0:1]
    tl = lax.broadcasted_iota(jnp.int32, (SUB, R), 1) & (TB - 1)
    ig = g8
    lf = _log_sigmoid(pltpu.roll(g8, SUB - N_HEADS, 0))
    Fc = _seg_scan(lf, tl, TB, jnp.add, 0.0)
    a = ig - Fc
    cmax = _seg_scan(a, tl, TB, jnp.maximum, -jnp.inf)
    m0_cols = [m_ref[bb][:, 0:1] for bb in range(BB)]
    m0 = _lanes_of(m0_cols, TB)
    m = Fc + jnp.maximum(m0, cmax)
    Fl_cols = [Fc[:, (bb + 1) * TB - 1:(bb + 1) * TB] for bb in range(BB)]
    mL_cols = [m[:, (bb + 1) * TB - 1:(bb + 1) * TB] for bb in range(BB)]
    Fl = _lanes_of(Fl_cols, TB)
    mL = _lanes_of(mL_cols, TB)
    per_row = jnp.concatenate(
        [Fc - m, jnp.exp(m0 + Fc - m), jnp.exp(a + Fl - mL), jnp.exp(-m),
         jnp.zeros((LANES - 4 * SUB, R), F32)], axis=0).T
    Fm, decay0 = per_row[:, 0:SUB], per_row[:, SUB:2 * SUB]
    wL, emm = per_row[:, 2 * SUB:3 * SUB], per_row[:, 3 * SUB:4 * SUB]

    for g in range(N_POOL_GROUPS):
        cols = slice(g * GW, (g + 1) * GW)
        mixed = jnp.dot(pooled_s[:, cols], wpool_ref[g], preferred_element_type=F32)
        ycat_s[:, cols] = (mixed * pscale_ref[:, cols] * zp_s[:, cols]).astype(BF16)
    q_s[...] = proj(2).astype(BF16)
    k_s[...] = (proj(3) * (DH ** -0.5)).astype(BF16)
    v_s[...] = proj(4).astype(BF16)
    if o_gate_early:
        og_s[...] = _sigmoid(proj(5))

    pool_out_ref[...] = ext_s[:, TB + 1:TB + HIST, :]

    for bb in range(BB):
        ext_s[bb, 0:HIST, :] = ext_s[bb, TB:TB + HIST, :]

    causal = (lax.broadcasted_iota(jnp.int32, (TB, TB), 0)
              >= lax.broadcasted_iota(jnp.int32, (TB, TB), 1))

    for bb in range(BB):
        rows = slice(bb * TB, (bb + 1) * TB)
        dL = jnp.exp(m0_cols[bb] + Fl_cols[bb] - mL_cols[bb])
        for hd in range(N_HEADS):
            cols = slice(hd * DH, (hd + 1) * DH)
            q = q_s[rows, cols]
            k = k_s[rows, cols]
            v = v_s[rows, cols]
            s = lax.dot_general(q, k, (((1,), (1,)), ((), ())), preferred_element_type=F32)
            logD = Fm[rows, hd:hd + 1] + a[hd:hd + 1, rows]
            S = s * jnp.exp(jnp.where(causal, logD, -jnp.inf))
            C0 = C_ref[bb, hd]
            n0 = n_ref[bb, hd:hd + 1, :]
            d0 = decay0[rows, hd:hd + 1]
            qn = jnp.sum(q.astype(F32) * n0, axis=-1, keepdims=True)
            nq = jnp.sum(S, axis=-1, keepdims=True) + d0 * qn
            num = (jnp.dot(S.astype(BF16), v, preferred_element_type=F32)
                   + d0 * jnp.dot(q, C0.astype(BF16), preferred_element_type=F32))
            den = jnp.maximum(jnp.abs(nq), emm[rows, hd:hd + 1])
            hh = num * (1.0 / den)
            hc = hh - jnp.mean(hh, axis=-1, keepdims=True)
            hn = hc * lax.rsqrt(jnp.mean(hc * hc, axis=-1, keepdims=True) + EPS)
            hm = hn * ghead_ref[:, cols]
            hm_s[rows, cols] = hm * og_s[rows, cols] if o_gate_early else hm
            kw = k.astype(F32) * wL[rows, hd:hd + 1]
            dl = dL[hd:hd + 1, :]
            C_ref[bb, hd] = dl * C0 + lax.dot_general(
                kw.astype(BF16), v, (((0,), (0,)), ((), ())), preferred_element_type=F32)
            n_ref[bb, hd:hd + 1, :] = dl * n0 + jnp.sum(kw, axis=0, keepdims=True)
        m_ref[bb] = jnp.broadcast_to(mL_cols[bb], (SUB, LANES))

    if not o_gate_early:
        hm_s[...] = hm_s[...] * _sigmoid(proj(5))
    ycat_s[:, W:2 * W] = (hm_s[...] * _silu(proj(6))).astype(BF16)

    out_cols = OUT_COLS if R >= MIN_ROWS_FOR_COLUMN_BLOCKS else D
    ssq = [jnp.zeros((TB, 1), F32) for _ in range(BB)]
    for c0 in range(0, D, out_cols):
        cols = slice(c0, c0 + out_cols)
        y = jnp.dot(ycat_s[...], wout_ref[:, cols], preferred_element_type=F32)
        for bb in range(BB):
            gate = mod_ref[bb, :, 2 * D + c0:2 * D + c0 + out_cols]
            xn = x_ref[bb, :, cols] + gate * y[bb * TB:(bb + 1) * TB, :]
            ssq[bb] = ssq[bb] + jnp.sum(xn * xn, axis=-1, keepdims=True)
            y_ref[bb, :, cols] = xn
    for bb in range(BB):
        r = lax.rsqrt(ssq[bb] * (1.0 / D) + EPS)
        y_ref[bb] = (y_ref[bb] * r) * gfinal_ref[...]


def _const_spec(shape):
    nd = len(shape)
    return pl.BlockSpec(shape, lambda b, t: (0,) * nd, pipeline_mode=pl.Buffered(1))


def _block_plan(n_batch, n_tokens, has_state):
    tb = min(n_tokens, MLSTM_CHUNK)
    c_bytes = N_HEADS * HEAD_DIM * HEAD_DIM * 4
    n_windows = 4 if has_state else 2
    bb = min(n_batch, MAX_STEP_ROWS // tb, STATE_WINDOW_BYTES // (n_windows * c_bytes))
    assert n_tokens % tb == 0 and n_batch % bb == 0 and tb >= 2 * HIST
    return bb, tb


def _run_layer(x, mod, mod_row0, state, weights, *, start):
    B, T, D = x.shape
    W, H, DH = W_POOL, N_HEADS, HEAD_DIM
    has_state = state is not None
    BB, TB = _block_plan(B, T, has_state)
    R = BB * TB
    grid = (B // BB, T // TB)
    assert mod_row0 % BB == 0
    mod_blk0 = mod_row0 // BB

    in_specs = [
        pl.BlockSpec((BB, TB, D), lambda b, t: (b, t, 0)),
        pl.BlockSpec((BB, 1, 3 * D), lambda b, t: (b + mod_blk0, 0, 0)),
    ]
    args = [x, mod]
    if has_state:
        pool0, C0, n0, m0 = state
        in_specs += [
            pl.BlockSpec((BB, POOL_BUF, W), lambda b, t: (b, 0, 0)),
            pl.BlockSpec((BB, H, DH, DH), lambda b, t: (b, 0, 0, 0)),
            pl.BlockSpec((BB, H, DH), lambda b, t: (b, 0, 0)),
            pl.BlockSpec((BB, SUB, LANES), lambda b, t: (b, 0, 0)),
        ]
        m0p = jnp.broadcast_to(jnp.pad(m0, ((0, 0), (0, SUB - H)))[:, :, None], (B, SUB, LANES))
        args += [pool0, C0, n0, m0p]
    in_specs += [_const_spec(w.shape) for w in weights]
    args += list(weights)

    out_shape = (
        jax.ShapeDtypeStruct((B, T, D), F32),
        jax.ShapeDtypeStruct((B, POOL_BUF, W), F32),
        jax.ShapeDtypeStruct((B, H, DH, DH), F32),
        jax.ShapeDtypeStruct((B, H, DH), F32),
        jax.ShapeDtypeStruct((B, SUB, LANES), F32),
    )
    out_specs = (
        pl.BlockSpec((BB, TB, D), lambda b, t: (b, t, 0)),
        pl.BlockSpec((BB, POOL_BUF, W), lambda b, t: (b, 0, 0)),
        pl.BlockSpec((BB, H, DH, DH), lambda b, t: (b, 0, 0, 0)),
        pl.BlockSpec((BB, H, DH), lambda b, t: (b, 0, 0)),
        pl.BlockSpec((BB, SUB, LANES), lambda b, t: (b, 0, 0)),
    )
    scratch = [
        pltpu.VMEM((R, D), BF16),
        pltpu.VMEM((BB, HIST + TB, W), F32),
        pltpu.VMEM((R, W), BF16),
        pltpu.VMEM((R, W), F32),
        pltpu.VMEM((R, W), F32),
        pltpu.VMEM((R, W), BF16),
        pltpu.VMEM((R, W), BF16),
        pltpu.VMEM((R, W), BF16),
        pltpu.VMEM((R, 2 * W), BF16),
        pltpu.VMEM((BB, 1, D), F32),
        pltpu.VMEM((R, W) if TB < MLSTM_CHUNK else (SUB, LANES), F32),
    ]
    y, pool, C, n, m = pl.pallas_call(
        functools.partial(_layer_kernel, BB=BB, TB=TB, start=start, has_state=has_state),
        grid=grid,
        in_specs=in_specs,
        out_specs=out_specs,
        out_shape=out_shape,
        scratch_shapes=scratch,
        compiler_params=pltpu.CompilerParams(
            dimension_semantics=("parallel", "arbitrary"),
            vmem_limit_bytes=VMEM_LIMIT_BYTES),
        name="layer_state" if has_state else "layer_fresh",
    )(*args)
    return y, pool, C, n, m[:, :H, 0]


def kernel(x_prompt, x_sample, c_prompt, c_sample, state_pool, state_C, state_n, state_m,
           w_ada, b_ada, g_norm, w_in, b_i, b_f, w_pool, pool_scale, g_head, w_out, g_final):
    depth = w_ada.shape[0]
    assert depth == 1, "single-layer trunk"
    l = 0
    nbp = x_prompt.shape[0]
    n_main = N_MAIN_SECTIONS * W_POOL

    mod = _adaln_mod(jnp.concatenate([c_prompt, c_sample], axis=0), w_ada[l], b_ada[l])

    w_in_t = jnp.swapaxes(w_in[l], 0, 1)
    n_gate = 2 * N_HEADS
    assert n_main % n_gate == 0 and w_in_t.shape[0] == n_main + n_gate
    w_gate_t = _row_block(w_in_t, n_main // n_gate, n_gate)
    assert n_gate == SUB
    w_gate = jnp.pad(w_gate_t, ((0, BF16_SUBLANES - n_gate), (0, 0))).astype(BF16)
    gate_bias = jnp.broadcast_to(jnp.concatenate([b_i[l], b_f[l]])[:, None], (SUB, LANES))
    weights = (
        g_norm[l].reshape(1, D_MODEL),
        _transposed_rows_bf16(w_in_t, n_main, W_POOL),
        w_gate,
        gate_bias,
        w_pool[l].astype(BF16),
        pool_scale[l].reshape(1, W_POOL),
        g_head[l].reshape(1, W_MLSTM),
        _rows_bf16(w_out[l], 512),
        g_final.reshape(1, D_MODEL),
    )

    yp, pp, pc, pn, pm = _run_layer(x_prompt, mod, 0, None, weights, start=0)
    ys, sp, sc, sn, sm = _run_layer(
        x_sample, mod, nbp, (state_pool[l], state_C[l], state_n[l], state_m[l]), weights,
        start=PAST_LEN)
    return (yp, ys, pp[None], pc[None], pn[None], pm[None], sp[None], sc[None], sn[None], sm[None])
```

```python
import functools

import jax
import jax.numpy as jnp
from jax import lax
from jax.experimental import pallas as pl
from jax.experimental.pallas import tpu as pltpu

F32 = jnp.float32
BF16 = jnp.bfloat16

D_MODEL = 1024
W_POOL = 1024
W_MLSTM = 1024
POOL_WINDOWS = (2, 4, 8, 16)
N_POOL_GROUPS = 4
POOL_GW = W_POOL // N_POOL_GROUPS
POOL_BUF = 15
N_HEADS = 4
HEAD_DIM = W_MLSTM // N_HEADS
EPS = 1e-6
N_MAIN_SECTIONS = 7
LANES = 128
SUB = 8
PAST_LEN = 2048
HIST = 16
OUT_COLS = 256
MIN_ROWS_FOR_COLUMN_BLOCKS = 256
VMEM_LIMIT_BYTES = 56 * 1024 * 1024
MLSTM_CHUNK = 256
MAX_STEP_ROWS = 512
STATE_WINDOW_BYTES = 16 * 1024 * 1024


def _sigmoid(z):
    return 0.5 * jnp.tanh(0.5 * z) + 0.5


def _silu(z):
    hz = 0.5 * z
    return hz * jnp.tanh(hz) + hz


def _log_sigmoid(z):
    return jnp.minimum(z, 0.0) - jnp.log1p(jnp.exp(-jnp.abs(z)))


def _mod_kernel(ca_ref, cb_ref, w_ref, b_ref, o_ref):
    c = jnp.concatenate([ca_ref[...], cb_ref[...]], axis=0)
    mod = jnp.dot(_silu(c), w_ref[...], preferred_element_type=F32) + b_ref[...]
    for b in range(mod.shape[0]):
        o_ref[b] = mod[b:b + 1, :]


def _adaln_mod(c_a, c_b, w_ada, b_ada):
    na, nb = c_a.shape[0], c_b.shape[0]
    n_out = w_ada.shape[1]
    blk = D_MODEL
    return pl.pallas_call(
        _mod_kernel,
        grid=(n_out // blk,),
        in_specs=[
            pl.BlockSpec((na, D_MODEL), lambda j: (0, 0)),
            pl.BlockSpec((nb, D_MODEL), lambda j: (0, 0)),
            pl.BlockSpec((D_MODEL, blk), lambda j: (0, j)),
            pl.BlockSpec((1, blk), lambda j: (0, j)),
        ],
        out_specs=pl.BlockSpec((na + nb, 1, blk), lambda j: (0, 0, j)),
        out_shape=jax.ShapeDtypeStruct((na + nb, 1, n_out), F32),
        name="adaln_mod",
    )(c_a, c_b, w_ada, b_ada.reshape(1, n_out))


def _cast_kernel(i_ref, o_ref):
    o_ref[...] = i_ref[...].T.astype(o_ref.dtype)


def _transposed_rows_bf16(wt, n_rows, blk):
    k = wt.shape[1]
    return pl.pallas_call(
        _cast_kernel,
        grid=(n_rows // blk,),
        in_specs=[pl.BlockSpec((blk, k), lambda j: (j, 0))],
        out_specs=pl.BlockSpec((k, blk), lambda j: (0, j)),
        out_shape=jax.ShapeDtypeStruct((k, n_rows), BF16),
        name="cast_bf16",
    )(wt)


def _plain_cast_kernel(i_ref, o_ref):
    o_ref[...] = i_ref[...].astype(o_ref.dtype)


def _rows_bf16(w, blk):
    n, k = w.shape
    return pl.pallas_call(
        _plain_cast_kernel,
        grid=(n // blk,),
        in_specs=[pl.BlockSpec((blk, k), lambda j: (j, 0))],
        out_specs=pl.BlockSpec((blk, k), lambda j: (j, 0)),
        out_shape=jax.ShapeDtypeStruct((n, k), BF16),
        name="cast_rows_bf16",
    )(w)


def _copy_kernel(i_ref, o_ref):
    o_ref[...] = i_ref[...]


def _row_block(wt, blk_index, blk):
    k = wt.shape[1]
    return pl.pallas_call(
        _copy_kernel,
        grid=(1,),
        in_specs=[pl.BlockSpec((blk, k), lambda j: (blk_index, 0))],
        out_specs=pl.BlockSpec((blk, k), lambda j: (0, 0)),
        out_shape=jax.ShapeDtypeStruct((blk, k), wt.dtype),
        name="row_block",
    )(wt)


def _seg_scan(x, tl, seg, combine, fill):
    k = 1
    while k < seg:
        shifted = pltpu.roll(x, k, 1)
        x = combine(x, jnp.where(tl >= k, shifted, fill))
        k *= 2
    return x


def _lanes_of(per_batch_cols, TB):
    parts = [jnp.broadcast_to(c, (SUB, TB)) for c in per_batch_cols]
    return parts[0] if len(parts) == 1 else jnp.concatenate(parts, axis=1)


def _layer_kernel(*refs, BB, TB, start, has_state):
    R = BB * TB
    o_gate_early = TB < MLSTM_CHUNK
    D, W, DH, GW = D_MODEL, W_POOL, HEAD_DIM, POOL_GW
    it = iter(refs)
    x_ref = next(it)
    mod_ref = next(it)
    if has_state:
        pool0_ref, C0_ref, n0_ref, m0_ref = next(it), next(it), next(it), next(it)
    gnorm_ref, win_ref, wg_ref, gbias_ref = next(it), next(it), next(it), next(it)
    wpool_ref, pscale_ref, ghead_ref, wout_ref, gfinal_ref = (next(it), next(it), next(it),
                                                             next(it), next(it))
    y_ref, pool_out_ref, C_ref, n_ref, m_ref = next(it), next(it), next(it), next(it), next(it)
    h_s, ext_s, pooled_s, zp_s, hm_s, q_s, k_s, v_s, ycat_s, gain_s, og_s = it

    t = pl.program_id(1)

    @pl.when(t == 0)
    def _init():
        if has_state:
            ext_s[:, 0:1, :] = jnp.zeros((BB, 1, W), F32)
            ext_s[:, 1:HIST, :] = pool0_ref[...]
            C_ref[...] = C0_ref[...]
            n_ref[...] = n0_ref[...]
            m_ref[...] = m0_ref[...]
        else:
            ext_s[:, 0:HIST, :] = jnp.zeros((BB, HIST, W), F32)
            C_ref[...] = jnp.zeros_like(C_ref)
            n_ref[...] = jnp.zeros_like(n_ref)
            m_ref[...] = jnp.zeros_like(m_ref)
        gain_s[...] = gnorm_ref[...] * (1.0 + mod_ref[:, :, D:2 * D])

    for bb in range(BB):
        x = x_ref[bb]
        r = lax.rsqrt(jnp.mean(x * x, axis=-1, keepdims=True) + EPS)
        shift = mod_ref[bb, :, 0:D]
        h = (x * r) * gain_s[bb] + shift
        h_s[bb * TB:(bb + 1) * TB, :] = h.astype(BF16)

    def proj(sec):
        return jnp.dot(h_s[...], win_ref[:, sec * W:(sec + 1) * W], preferred_element_type=F32)

    gates = jnp.dot(h_s[...], wg_ref[...], preferred_element_type=F32) + gbias_ref[...]

    g8 = gates.T[0:SUB, :]
    tl = lax.broadcasted_iota(jnp.int32, (SUB, R), 1) & (TB - 1)
    ig = g8
    lf = _log_sigmoid(pltpu.roll(g8, SUB - N_HEADS, 0))
    Fc = _seg_scan(lf, tl, TB, jnp.add, 0.0)
    a = ig - Fc
    cmax = _seg_scan(a, tl, TB, jnp.maximum, -jnp.inf)
    m0_cols = [m_ref[bb][:, 0:1] for bb in range(BB)]
    m0 = _lanes_of(m0_cols, TB)
    m = Fc + jnp.maximum(m0, cmax)
    Fl_cols = [Fc[:, (bb + 1) * TB - 1:(bb + 1) * TB] for bb in range(BB)]
    mL_cols = [m[:, (bb + 1) * TB - 1:(bb + 1) * TB] for bb in range(BB)]
    Fl = _lanes_of(Fl_cols, TB)
    mL = _lanes_of(mL_cols, TB)
    per_row = jnp.concatenate(
        [Fc - m, jnp.exp(m0 + Fc - m), jnp.exp(a + Fl - mL), jnp.exp(-m),
         jnp.zeros((LANES - 4 * SUB, R), F32)], axis=0).T
    Fm, decay0 = per_row[:, 0:SUB], per_row[:, SUB:2 * SUB]
    wL, emm = per_row[:, 2 * SUB:3 * SUB], per_row[:, 3 * SUB:4 * SUB]

    xp = proj(0)
    for bb in range(BB):
        ext_s[bb, HIST:HIST + TB, :] = xp[bb * TB:(bb + 1) * TB, :]

    pos_head = start + t * TB + lax.broadcasted_iota(jnp.int32, (HIST, 1), 0)
    for g, w in enumerate(POOL_WINDOWS):
        cols = slice(g * GW, (g + 1) * GW)
        inv_head = 1.0 / jnp.minimum(pos_head + 1, w).astype(F32)
        for bb in range(BB):
            ext = ext_s[bb, :, cols]
            win = ext
            s = 1
            while s < w:
                win = win + pltpu.roll(win, s, 0)
                s *= 2
            r0 = bb * TB
            head = win[HIST:2 * HIST, :] * inv_head - ext[HIST:2 * HIST, :]
            tail = win[2 * HIST:, :] * (1.0 / w) - ext[2 * HIST:, :]
            pooled_s[r0:r0 + HIST, cols] = head.astype(BF16)
            pooled_s[r0 + HIST:r0 + TB, cols] = tail.astype(BF16)

    zp_s[...] = _silu(proj(1))
    for g in range(N_POOL_GROUPS):
        cols = slice(g * GW, (g + 1) * GW)
        mixed = jnp.dot(pooled_s[:, cols], wpool_ref[g], preferred_element_type=F32)
        ycat_s[:, cols] = (mixed * pscale_ref[:, cols] * zp_s[:, cols]).astype(BF16)
    q_s[...] = proj(2).astype(BF16)
    k_s[...] = (proj(3) * (DH ** -0.5)).astype(BF16)
    v_s[...] = proj(4).astype(BF16)
    if o_gate_early:
        og_s[...] = _sigmoid(proj(5))

    pool_out_ref[...] = ext_s[:, TB + 1:TB + HIST, :]

    for bb in range(BB):
        ext_s[bb, 0:HIST, :] = ext_s[bb, TB:TB + HIST, :]

    causal = (lax.broadcasted_iota(jnp.int32, (TB, TB), 0)
              >= lax.broadcasted_iota(jnp.int32, (TB, TB), 1))

    for bb in range(BB):
        rows = slice(bb * TB, (bb + 1) * TB)
        dL = jnp.exp(m0_cols[bb] + Fl_cols[bb] - mL_cols[bb])
        for hd in range(N_HEADS):
            cols = slice(hd * DH, (hd + 1) * DH)
            q = q_s[rows, cols]
            k = k_s[rows, cols]
            v = v_s[rows, cols]
            s = lax.dot_general(q, k, (((1,), (1,)), ((), ())), preferred_element_type=F32)
            logD = Fm[rows, hd:hd + 1] + a[hd:hd + 1, rows]
            S = s * jnp.exp(jnp.where(causal, logD, -jnp.inf))
            C0 = C_ref[bb, hd]
            n0 = n_ref[bb, hd:hd + 1, :]
            d0 = decay0[rows, hd:hd + 1]
            qn = jnp.sum(q.astype(F32) * n0, axis=-1, keepdims=True)
            nq = jnp.sum(S, axis=-1, keepdims=True) + d0 * qn
            num = (jnp.dot(S.astype(BF16), v, preferred_element_type=F32)
                   + d0 * jnp.dot(q, C0.astype(BF16), preferred_element_type=F32))
            den = jnp.maximum(jnp.abs(nq), emm[rows, hd:hd + 1])
            hh = num * (1.0 / den)
            hc = hh - jnp.mean(hh, axis=-1, keepdims=True)
            hn = hc * lax.rsqrt(jnp.mean(hc * hc, axis=-1, keepdims=True) + EPS)
            hm = hn * ghead_ref[:, cols]
            hm_s[rows, cols] = hm * og_s[rows, cols] if o_gate_early else hm
            kw = k.astype(F32) * wL[rows, hd:hd + 1]
            dl = dL[hd:hd + 1, :]
            C_ref[bb, hd] = dl * C0 + lax.dot_general(
                kw.astype(BF16), v, (((0,), (0,)), ((), ())), preferred_element_type=F32)
            n_ref[bb, hd:hd + 1, :] = dl * n0 + jnp.sum(kw, axis=0, keepdims=True)
        m_ref[bb] = jnp.broadcast_to(mL_cols[bb], (SUB, LANES))

    if not o_gate_early:
        hm_s[...] = hm_s[...] * _sigmoid(proj(5))
    ycat_s[:, W:2 * W] = (hm_s[...] * _silu(proj(6))).astype(BF16)

    out_cols = OUT_COLS if R >= MIN_ROWS_FOR_COLUMN_BLOCKS else D
    ssq = [jnp.zeros((TB, 1), F32) for _ in range(BB)]
    for c0 in range(0, D, out_cols):
        cols = slice(c0, c0 + out_cols)
        y = jnp.dot(ycat_s[...], wout_ref[:, cols], preferred_element_type=F32)
        for bb in range(BB):
            gate = mod_ref[bb, :, 2 * D + c0:2 * D + c0 + out_cols]
            xn = x_ref[bb, :, cols] + gate * y[bb * TB:(bb + 1) * TB, :]
            ssq[bb] = ssq[bb] + jnp.sum(xn * xn, axis=-1, keepdims=True)
            y_ref[bb, :, cols] = xn
    for bb in range(BB):
        r = lax.rsqrt(ssq[bb] * (1.0 / D) + EPS)
        y_ref[bb] = (y_ref[bb] * r) * gfinal_ref[...]


def _const_spec(shape):
    nd = len(shape)
    return pl.BlockSpec(shape, lambda b, t: (0,) * nd, pipeline_mode=pl.Buffered(1))


def _block_plan(n_batch, n_tokens, has_state):
    tb = min(n_tokens, MLSTM_CHUNK)
    c_bytes = N_HEADS * HEAD_DIM * HEAD_DIM * 4
    n_windows = 4 if has_state else 2
    bb = min(n_batch, MAX_STEP_ROWS // tb, STATE_WINDOW_BYTES // (n_windows * c_bytes))
    assert n_tokens % tb == 0 and n_batch % bb == 0 and tb >= 2 * HIST
    return bb, tb


def _run_layer(x, mod, mod_row0, state, weights, *, start):
    B, T, D = x.shape
    W, H, DH = W_POOL, N_HEADS, HEAD_DIM
    has_state = state is not None
    BB, TB = _block_plan(B, T, has_state)
    R = BB * TB
    grid = (B // BB, T // TB)
    assert mod_row0 % BB == 0
    mod_blk0 = mod_row0 // BB

    in_specs = [
        pl.BlockSpec((BB, TB, D), lambda b, t: (b, t, 0)),
        pl.BlockSpec((BB, 1, 3 * D), lambda b, t: (b + mod_blk0, 0, 0)),
    ]
    args = [x, mod]
    if has_state:
        pool0, C0, n0, m0 = state
        in_specs += [
            pl.BlockSpec((BB, POOL_BUF, W), lambda b, t: (b, 0, 0)),
            pl.BlockSpec((BB, H, DH, DH), lambda b, t: (b, 0, 0, 0)),
            pl.BlockSpec((BB, H, DH), lambda b, t: (b, 0, 0)),
            pl.BlockSpec((BB, SUB, LANES), lambda b, t: (b, 0, 0)),
        ]
        m0p = jnp.broadcast_to(jnp.pad(m0, ((0, 0), (0, SUB - H)))[:, :, None], (B, SUB, LANES))
        args += [pool0, C0, n0, m0p]
    in_specs += [_const_spec(w.shape) for w in weights]
    args += list(weights)

    out_shape = (
        jax.ShapeDtypeStruct((B, T, D), F32),
        jax.ShapeDtypeStruct((B, POOL_BUF, W), F32),
        jax.ShapeDtypeStruct((B, H, DH, DH), F32),
        jax.ShapeDtypeStruct((B, H, DH), F32),
        jax.ShapeDtypeStruct((B, SUB, LANES), F32),
    )
    out_specs = (
        pl.BlockSpec((BB, TB, D), lambda b, t: (b, t, 0)),
        pl.BlockSpec((BB, POOL_BUF, W), lambda b, t: (b, 0, 0)),
        pl.BlockSpec((BB, H, DH, DH), lambda b, t: (b, 0, 0, 0)),
        pl.BlockSpec((BB, H, DH), lambda b, t: (b, 0, 0)),
        pl.BlockSpec((BB, SUB, LANES), lambda b, t: (b, 0, 0)),
    )
    scratch = [
        pltpu.VMEM((R, D), BF16),
        pltpu.VMEM((BB, HIST + TB, W), F32),
        pltpu.VMEM((R, W), BF16),
        pltpu.VMEM((R, W), F32),
        pltpu.VMEM((R, W), F32),
        pltpu.VMEM((R, W), BF16),
        pltpu.VMEM((R, W), BF16),
        pltpu.VMEM((R, W), BF16),
        pltpu.VMEM((R, 2 * W), BF16),
        pltpu.VMEM((BB, 1, D), F32),
        pltpu.VMEM((R, W) if TB < MLSTM_CHUNK else (SUB, LANES), F32),
    ]
    y, pool, C, n, m = pl.pallas_call(
        functools.partial(_layer_kernel, BB=BB, TB=TB, start=start, has_state=has_state),
        grid=grid,
        in_specs=in_specs,
        out_specs=out_specs,
        out_shape=out_shape,
        scratch_shapes=scratch,
        compiler_params=pltpu.CompilerParams(
            dimension_semantics=("parallel", "arbitrary"),
            vmem_limit_bytes=VMEM_LIMIT_BYTES),
        name="layer_state" if has_state else "layer_fresh",
    )(*args)
    return y, pool, C, n, m[:, :H, 0]


def kernel(x_prompt, x_sample, c_prompt, c_sample, state_pool, state_C, state_n, state_m,
           w_ada, b_ada, g_norm, w_in, b_i, b_f, w_pool, pool_scale, g_head, w_out, g_final):
    depth = w_ada.shape[0]
    assert depth == 1, "single-layer trunk"
    l = 0
    nbp = x_prompt.shape[0]
    n_main = N_MAIN_SECTIONS * W_POOL

    mod = _adaln_mod(c_prompt, c_sample, w_ada[l], b_ada[l])

    w_in_t = jnp.swapaxes(w_in[l], 0, 1)
    n_gate = 2 * N_HEADS
    assert n_main % n_gate == 0 and w_in_t.shape[0] == n_main + n_gate
    w_gate_t = _row_block(w_in_t, n_main // n_gate, n_gate)
    w_gate = jnp.pad(w_gate_t.T, ((0, 0), (0, LANES - n_gate))).astype(BF16)
    gate_bias = jnp.pad(jnp.concatenate([b_i[l], b_f[l]]), (0, LANES - 2 * N_HEADS)).reshape(1, LANES)
    weights = (
        g_norm[l].reshape(1, D_MODEL),
        _transposed_rows_bf16(w_in_t, n_main, W_POOL),
        w_gate,
        gate_bias,
        w_pool[l].astype(BF16),
        pool_scale[l].reshape(1, W_POOL),
        g_head[l].reshape(1, W_MLSTM),
        _rows_bf16(w_out[l], 512),
        g_final.reshape(1, D_MODEL),
    )

    yp, pp, pc, pn, pm = _run_layer(x_prompt, mod, 0, None, weights, start=0)
    ys, sp, sc, sn, sm = _run_layer(
        x_sample, mod, nbp, (state_pool[l], state_C[l], state_n[l], state_m[l]), weights,
        start=PAST_LEN)
    return (yp, ys, pp[None], pc[None], pn[None], pm[None], sp[None], sc[None], sn[None], sm[None])
```

```python
import functools

import jax
import jax.numpy as jnp
from jax import lax
from jax.experimental import pallas as pl
from jax.experimental.pallas import tpu as pltpu

F32 = jnp.float32
BF16 = jnp.bfloat16

D_MODEL = 1024
W_POOL = 1024
W_MLSTM = 1024
POOL_WINDOWS = (2, 4, 8, 16)
N_POOL_GROUPS = 4
POOL_GW = W_POOL // N_POOL_GROUPS
POOL_BUF = 15
N_HEADS = 4
HEAD_DIM = W_MLSTM // N_HEADS
EPS = 1e-6
N_MAIN_SECTIONS = 7
LANES = 128
SUB = 8
PAST_LEN = 2048
HIST = 16
OUT_COLS = 256
MIN_ROWS_FOR_COLUMN_BLOCKS = 256
VMEM_LIMIT_BYTES = 56 * 1024 * 1024
MLSTM_CHUNK = 256
MAX_STEP_ROWS = 512
STATE_WINDOW_BYTES = 16 * 1024 * 1024


def _sigmoid(z):
    return 0.5 * jnp.tanh(0.5 * z) + 0.5


def _silu(z):
    hz = 0.5 * z
    return hz * jnp.tanh(hz) + hz


def _log_sigmoid(z):
    return jnp.minimum(z, 0.0) - jnp.log1p(jnp.exp(-jnp.abs(z)))


def _mod_kernel(ca_ref, cb_ref, w_ref, b_ref, o_ref):
    c = jnp.concatenate([ca_ref[...], cb_ref[...]], axis=0)
    mod = jnp.dot(_silu(c), w_ref[...], preferred_element_type=F32) + b_ref[...]
    for b in range(mod.shape[0]):
        o_ref[b] = mod[b:b + 1, :]


def _adaln_mod(c_a, c_b, w_ada, b_ada):
    na, nb = c_a.shape[0], c_b.shape[0]
    n_out = w_ada.shape[1]
    blk = D_MODEL
    return pl.pallas_call(
        _mod_kernel,
        grid=(n_out // blk,),
        in_specs=[
            pl.BlockSpec((na, D_MODEL), lambda j: (0, 0)),
            pl.BlockSpec((nb, D_MODEL), lambda j: (0, 0)),
            pl.BlockSpec((D_MODEL, blk), lambda j: (0, j)),
            pl.BlockSpec((1, blk), lambda j: (0, j)),
        ],
        out_specs=pl.BlockSpec((na + nb, 1, blk), lambda j: (0, 0, j)),
        out_shape=jax.ShapeDtypeStruct((na + nb, 1, n_out), F32),
        name="adaln_mod",
    )(c_a, c_b, w_ada, b_ada.reshape(1, n_out))


def _cast_kernel(i_ref, o_ref):
    o_ref[...] = i_ref[...].T.astype(o_ref.dtype)


def _transposed_rows_bf16(wt, n_rows, blk):
    k = wt.shape[1]
    return pl.pallas_call(
        _cast_kernel,
        grid=(n_rows // blk,),
        in_specs=[pl.BlockSpec((blk, k), lambda j: (j, 0))],
        out_specs=pl.BlockSpec((k, blk), lambda j: (0, j)),
        out_shape=jax.ShapeDtypeStruct((k, n_rows), BF16),
        name="cast_bf16",
    )(wt)


def _plain_cast_kernel(i_ref, o_ref):
    o_ref[...] = i_ref[...].astype(o_ref.dtype)


def _rows_bf16(w, blk):
    n, k = w.shape
    return pl.pallas_call(
        _plain_cast_kernel,
        grid=(n // blk,),
        in_specs=[pl.BlockSpec((blk, k), lambda j: (j, 0))],
        out_specs=pl.BlockSpec((blk, k), lambda j: (j, 0)),
        out_shape=jax.ShapeDtypeStruct((n, k), BF16),
        name="cast_rows_bf16",
    )(w)


def _copy_kernel(i_ref, o_ref):
    o_ref[...] = i_ref[...]


def _row_block(wt, blk_index, blk):
    k = wt.shape[1]
    return pl.pallas_call(
        _copy_kernel,
        grid=(1,),
        in_specs=[pl.BlockSpec((blk, k), lambda j: (blk_index, 0))],
        out_specs=pl.BlockSpec((blk, k), lambda j: (0, 0)),
        out_shape=jax.ShapeDtypeStruct((blk, k), wt.dtype),
        name="row_block",
    )(wt)


def _seg_scan(x, tl, seg, combine, fill):
    k = 1
    while k < seg:
        shifted = pltpu.roll(x, k, 1)
        x = combine(x, jnp.where(tl >= k, shifted, fill))
        k *= 2
    return x


def _lanes_of(per_batch_cols, TB):
    parts = [jnp.broadcast_to(c, (SUB, TB)) for c in per_batch_cols]
    return parts[0] if len(parts) == 1 else jnp.concatenate(parts, axis=1)


def _layer_kernel(*refs, BB, TB, start, has_state):
    R = BB * TB
    o_gate_early = TB < MLSTM_CHUNK
    D, W, DH, GW = D_MODEL, W_POOL, HEAD_DIM, POOL_GW
    it = iter(refs)
    x_ref = next(it)
    mod_ref = next(it)
    if has_state:
        pool0_ref, C0_ref, n0_ref, m0_ref = next(it), next(it), next(it), next(it)
    gnorm_ref, win_ref, wg_ref, gbias_ref = next(it), next(it), next(it), next(it)
    wpool_ref, pscale_ref, ghead_ref, wout_ref, gfinal_ref = (next(it), next(it), next(it),
                                                             next(it), next(it))
    y_ref, pool_out_ref, C_ref, n_ref, m_ref = next(it), next(it), next(it), next(it), next(it)
    h_s, ext_s, pooled_s, zp_s, hm_s, q_s, k_s, v_s, ycat_s, gain_s, og_s = it

    t = pl.program_id(1)
    b0 = pl.program_id(0) * BB

    @pl.when(t == 0)
    def _init():
        if has_state:
            ext_s[:, 0:1, :] = jnp.zeros((BB, 1, W), F32)
            for bb in range(BB):
                for j in range(POOL_BUF):
                    ext_s[bb, 1 + j:2 + j, :] = pool0_ref[j, pl.ds(b0 + bb, 1), :]
            C_ref[...] = C0_ref[...]
            n_ref[...] = n0_ref[...]
            m_ref[...] = m0_ref[...]
        else:
            ext_s[:, 0:HIST, :] = jnp.zeros((BB, HIST, W), F32)
            C_ref[...] = jnp.zeros_like(C_ref)
            n_ref[...] = jnp.zeros_like(n_ref)
            m_ref[...] = jnp.zeros_like(m_ref)
        gain_s[...] = gnorm_ref[...] * (1.0 + mod_ref[:, :, D:2 * D])

    for bb in range(BB):
        x = x_ref[bb]
        r = lax.rsqrt(jnp.mean(x * x, axis=-1, keepdims=True) + EPS)
        shift = mod_ref[bb, :, 0:D]
        h = (x * r) * gain_s[bb] + shift
        h_s[bb * TB:(bb + 1) * TB, :] = h.astype(BF16)

    def proj(sec):
        return jnp.dot(h_s[...], win_ref[:, sec * W:(sec + 1) * W], preferred_element_type=F32)

    gates = jnp.dot(h_s[...], wg_ref[...], preferred_element_type=F32) + gbias_ref[...]

    g8 = gates.T[0:SUB, :]
    tl = lax.broadcasted_iota(jnp.int32, (SUB, R), 1) & (TB - 1)
    ig = g8
    lf = _log_sigmoid(pltpu.roll(g8, SUB - N_HEADS, 0))
    Fc = _seg_scan(lf, tl, TB, jnp.add, 0.0)
    a = ig - Fc
    cmax = _seg_scan(a, tl, TB, jnp.maximum, -jnp.inf)
    m0_cols = [m_ref[bb][:, 0:1] for bb in range(BB)]
    m0 = _lanes_of(m0_cols, TB)
    m = Fc + jnp.maximum(m0, cmax)
    Fl_cols = [Fc[:, (bb + 1) * TB - 1:(bb + 1) * TB] for bb in range(BB)]
    mL_cols = [m[:, (bb + 1) * TB - 1:(bb + 1) * TB] for bb in range(BB)]
    Fl = _lanes_of(Fl_cols, TB)
    mL = _lanes_of(mL_cols, TB)
    per_row = jnp.concatenate(
        [Fc - m, jnp.exp(m0 + Fc - m), jnp.exp(a + Fl - mL), jnp.exp(-m),
         jnp.zeros((LANES - 4 * SUB, R), F32)], axis=0).T
    Fm, decay0 = per_row[:, 0:SUB], per_row[:, SUB:2 * SUB]
    wL, emm = per_row[:, 2 * SUB:3 * SUB], per_row[:, 3 * SUB:4 * SUB]

    xp = proj(0)
    for bb in range(BB):
        ext_s[bb, HIST:HIST + TB, :] = xp[bb * TB:(bb + 1) * TB, :]

    pos_head = start + t * TB + lax.broadcasted_iota(jnp.int32, (HIST, 1), 0)
    for g, w in enumerate(POOL_WINDOWS):
        cols = slice(g * GW, (g + 1) * GW)
        inv_head = 1.0 / jnp.minimum(pos_head + 1, w).astype(F32)
        for bb in range(BB):
            ext = ext_s[bb, :, cols]
            win = ext
            s = 1
            while s < w:
                win = win + pltpu.roll(win, s, 0)
                s *= 2
            r0 = bb * TB
            head = win[HIST:2 * HIST, :] * inv_head - ext[HIST:2 * HIST, :]
            tail = win[2 * HIST:, :] * (1.0 / w) - ext[2 * HIST:, :]
            pooled_s[r0:r0 + HIST, cols] = head.astype(BF16)
            pooled_s[r0 + HIST:r0 + TB, cols] = tail.astype(BF16)

    zp_s[...] = _silu(proj(1))
    for g in range(N_POOL_GROUPS):
        cols = slice(g * GW, (g + 1) * GW)
        mixed = jnp.dot(pooled_s[:, cols], wpool_ref[g], preferred_element_type=F32)
        ycat_s[:, cols] = (mixed * pscale_ref[:, cols] * zp_s[:, cols]).astype(BF16)
    q_s[...] = proj(2).astype(BF16)
    k_s[...] = (proj(3) * (DH ** -0.5)).astype(BF16)
    v_s[...] = proj(4).astype(BF16)
    if o_gate_early:
        og_s[...] = _sigmoid(proj(5))

    for bb in range(BB):
        for j in range(POOL_BUF):
            pool_out_ref[j, pl.ds(b0 + bb, 1), :] = ext_s[bb, TB + 1 + j:TB + 2 + j, :]

    for bb in range(BB):
        ext_s[bb, 0:HIST, :] = ext_s[bb, TB:TB + HIST, :]

    causal = (lax.broadcasted_iota(jnp.int32, (TB, TB), 0)
              >= lax.broadcasted_iota(jnp.int32, (TB, TB), 1))

    for bb in range(BB):
        rows = slice(bb * TB, (bb + 1) * TB)
        dL = jnp.exp(m0_cols[bb] + Fl_cols[bb] - mL_cols[bb])
        for hd in range(N_HEADS):
            cols = slice(hd * DH, (hd + 1) * DH)
            q = q_s[rows, cols]
            k = k_s[rows, cols]
            v = v_s[rows, cols]
            s = lax.dot_general(q, k, (((1,), (1,)), ((), ())), preferred_element_type=F32)
            logD = Fm[rows, hd:hd + 1] + a[hd:hd + 1, rows]
            S = s * jnp.exp(jnp.where(causal, logD, -jnp.inf))
            C0 = C_ref[bb, hd]
            n0 = n_ref[bb, hd:hd + 1, :]
            d0 = decay0[rows, hd:hd + 1]
            qn = jnp.sum(q.astype(F32) * n0, axis=-1, keepdims=True)
            nq = jnp.sum(S, axis=-1, keepdims=True) + d0 * qn
            num = (jnp.dot(S.astype(BF16), v, preferred_element_type=F32)
                   + d0 * jnp.dot(q, C0.astype(BF16), preferred_element_type=F32))
            den = jnp.maximum(jnp.abs(nq), emm[rows, hd:hd + 1])
            hh = num * (1.0 / den)
            hc = hh - jnp.mean(hh, axis=-1, keepdims=True)
            hn = hc * lax.rsqrt(jnp.mean(hc * hc, axis=-1, keepdims=True) + EPS)
            hm = hn * ghead_ref[:, cols]
            hm_s[rows, cols] = hm * og_s[rows, cols] if o_gate_early else hm
            kw = k.astype(F32) * wL[rows, hd:hd + 1]
            dl = dL[hd:hd + 1, :]
            C_ref[bb, hd] = dl * C0 + lax.dot_general(
                kw.astype(BF16), v, (((0,), (0,)), ((), ())), preferred_element_type=F32)
            n_ref[bb, hd:hd + 1, :] = dl * n0 + jnp.sum(kw, axis=0, keepdims=True)
        m_ref[bb] = jnp.broadcast_to(mL_cols[bb], (SUB, LANES))

    if not o_gate_early:
        hm_s[...] = hm_s[...] * _sigmoid(proj(5))
    ycat_s[:, W:2 * W] = (hm_s[...] * _silu(proj(6))).astype(BF16)

    out_cols = OUT_COLS if R >= MIN_ROWS_FOR_COLUMN_BLOCKS else D
    ssq = [jnp.zeros((TB, 1), F32) for _ in range(BB)]
    for c0 in range(0, D, out_cols):
        cols = slice(c0, c0 + out_cols)
        y = jnp.dot(ycat_s[...], wout_ref[:, cols], preferred_element_type=F32)
        for bb in range(BB):
            gate = mod_ref[bb, :, 2 * D + c0:2 * D + c0 + out_cols]
            xn = x_ref[bb, :, cols] + gate * y[bb * TB:(bb + 1) * TB, :]
            ssq[bb] = ssq[bb] + jnp.sum(xn * xn, axis=-1, keepdims=True)
            y_ref[bb, :, cols] = xn
    for bb in range(BB):
        r = lax.rsqrt(ssq[bb] * (1.0 / D) + EPS)
        y_ref[bb] = (y_ref[bb] * r) * gfinal_ref[...]


def _const_spec(shape):
    nd = len(shape)
    return pl.BlockSpec(shape, lambda b, t: (0,) * nd, pipeline_mode=pl.Buffered(1))


def _block_plan(n_batch, n_tokens, has_state):
    tb = min(n_tokens, MLSTM_CHUNK)
    c_bytes = N_HEADS * HEAD_DIM * HEAD_DIM * 4
    n_windows = 4 if has_state else 2
    bb = min(n_batch, MAX_STEP_ROWS // tb, STATE_WINDOW_BYTES // (n_windows * c_bytes))
    assert n_tokens % tb == 0 and n_batch % bb == 0 and tb >= 2 * HIST
    return bb, tb


def _run_layer(x, mod, mod_row0, state, weights, *, start):
    B, T, D = x.shape
    W, H, DH = W_POOL, N_HEADS, HEAD_DIM
    has_state = state is not None
    BB, TB = _block_plan(B, T, has_state)
    R = BB * TB
    grid = (B // BB, T // TB)
    assert mod_row0 % BB == 0
    mod_blk0 = mod_row0 // BB

    in_specs = [
        pl.BlockSpec((BB, TB, D), lambda b, t: (b, t, 0)),
        pl.BlockSpec((BB, 1, 3 * D), lambda b, t: (b + mod_blk0, 0, 0)),
    ]
    args = [x, mod]
    if has_state:
        pool0, C0, n0, m0 = state
        in_specs += [
            pl.BlockSpec((POOL_BUF, B, W), lambda b, t: (0, 0, 0), pipeline_mode=pl.Buffered(1)),
            pl.BlockSpec((BB, H, DH, DH), lambda b, t: (b, 0, 0, 0)),
            pl.BlockSpec((BB, H, DH), lambda b, t: (b, 0, 0)),
            pl.BlockSpec((BB, SUB, LANES), lambda b, t: (b, 0, 0)),
        ]
        m0p = jnp.broadcast_to(jnp.pad(m0, ((0, 0), (0, SUB - H)))[:, :, None], (B, SUB, LANES))
        args += [jnp.transpose(pool0, (1, 0, 2)), C0, n0, m0p]
    in_specs += [_const_spec(w.shape) for w in weights]
    args += list(weights)

    out_shape = (
        jax.ShapeDtypeStruct((B, T, D), F32),
        jax.ShapeDtypeStruct((POOL_BUF, B, W), F32),
        jax.ShapeDtypeStruct((B, H, DH, DH), F32),
        jax.ShapeDtypeStruct((B, H, DH), F32),
        jax.ShapeDtypeStruct((B, SUB, LANES), F32),
    )
    out_specs = (
        pl.BlockSpec((BB, TB, D), lambda b, t: (b, t, 0)),
        pl.BlockSpec((POOL_BUF, B, W), lambda b, t: (0, 0, 0)),
        pl.BlockSpec((BB, H, DH, DH), lambda b, t: (b, 0, 0, 0)),
        pl.BlockSpec((BB, H, DH), lambda b, t: (b, 0, 0)),
        pl.BlockSpec((BB, SUB, LANES), lambda b, t: (b, 0, 0)),
    )
    scratch = [
        pltpu.VMEM((R, D), BF16),
        pltpu.VMEM((BB, HIST + TB, W), F32),
        pltpu.VMEM((R, W), BF16),
        pltpu.VMEM((R, W), F32),
        pltpu.VMEM((R, W), F32),
        pltpu.VMEM((R, W), BF16),
        pltpu.VMEM((R, W), BF16),
        pltpu.VMEM((R, W), BF16),
        pltpu.VMEM((R, 2 * W), BF16),
        pltpu.VMEM((BB, 1, D), F32),
        pltpu.VMEM((R, W) if TB < MLSTM_CHUNK else (SUB, LANES), F32),
    ]
    y, pool, C, n, m = pl.pallas_call(
        functools.partial(_layer_kernel, BB=BB, TB=TB, start=start, has_state=has_state),
        grid=grid,
        in_specs=in_specs,
        out_specs=out_specs,
        out_shape=out_shape,
        scratch_shapes=scratch,
        compiler_params=pltpu.CompilerParams(
            dimension_semantics=("arbitrary", "arbitrary"),
            vmem_limit_bytes=VMEM_LIMIT_BYTES),
        name="layer_state" if has_state else "layer_fresh",
    )(*args)
    return y, jnp.transpose(pool, (1, 0, 2)), C, n, m[:, :H, 0]


def kernel(x_prompt, x_sample, c_prompt, c_sample, state_pool, state_C, state_n, state_m,
           w_ada, b_ada, g_norm, w_in, b_i, b_f, w_pool, pool_scale, g_head, w_out, g_final):
    depth = w_ada.shape[0]
    assert depth == 1, "single-layer trunk"
    l = 0
    nbp = x_prompt.shape[0]
    n_main = N_MAIN_SECTIONS * W_POOL

    mod = _adaln_mod(c_prompt, c_sample, w_ada[l], b_ada[l])

    w_in_t = jnp.swapaxes(w_in[l], 0, 1)
    n_gate = 2 * N_HEADS
    assert n_main % n_gate == 0 and w_in_t.shape[0] == n_main + n_gate
    w_gate_t = _row_block(w_in_t, n_main // n_gate, n_gate)
    w_gate = jnp.pad(w_gate_t.T, ((0, 0), (0, LANES - n_gate))).astype(BF16)
    gate_bias = jnp.pad(jnp.concatenate([b_i[l], b_f[l]]), (0, LANES - 2 * N_HEADS)).reshape(1, LANES)
    weights = (
        g_norm[l].reshape(1, D_MODEL),
        _transposed_rows_bf16(w_in_t, n_main, W_POOL),
        w_gate,
        gate_bias,
        w_pool[l].astype(BF16),
        pool_scale[l].reshape(1, W_POOL),
        g_head[l].reshape(1, W_MLSTM),
        _rows_bf16(w_out[l], 512),
        g_final.reshape(1, D_MODEL),
    )

    yp, pp, pc, pn, pm = _run_layer(x_prompt, mod, 0, None, weights, start=0)
    ys, sp, sc, sn, sm = _run_layer(
        x_sample, mod, nbp, (state_pool[l], state_C[l], state_n[l], state_m[l]), weights,
        start=PAST_LEN)
    return (yp, ys, pp[None], pc[None], pn[None], pm[None], sp[None], sc[None], sn[None], sm[None])
```

```python
import functools

import jax
import jax.numpy as jnp
from jax import lax
from jax.experimental import pallas as pl
from jax.experimental.pallas import tpu as pltpu

F32 = jnp.float32
BF16 = jnp.bfloat16

D_MODEL = 1024
W_POOL = 1024
W_MLSTM = 1024
POOL_WINDOWS = (2, 4, 8, 16)
N_POOL_GROUPS = 4
POOL_GW = W_POOL // N_POOL_GROUPS
POOL_BUF = 15
N_HEADS = 4
HEAD_DIM = W_MLSTM // N_HEADS
EPS = 1e-6
N_MAIN_SECTIONS = 7
LANES = 128
SUB = 8
PAST_LEN = 2048
HIST = 16
OUT_COLS = 256
MIN_ROWS_FOR_COLUMN_BLOCKS = 256
VMEM_LIMIT_BYTES = 56 * 1024 * 1024
MLSTM_CHUNK = 256
MAX_STEP_ROWS = 512
STATE_WINDOW_BYTES = 16 * 1024 * 1024


def _sigmoid(z):
    return 0.5 * jnp.tanh(0.5 * z) + 0.5


def _silu(z):
    hz = 0.5 * z
    return hz * jnp.tanh(hz) + hz


def _log_sigmoid(z):
    return jnp.minimum(z, 0.0) - jnp.log1p(jnp.exp(-jnp.abs(z)))


def _mod_kernel(ca_ref, cb_ref, w_ref, b_ref, o_ref):
    c = jnp.concatenate([ca_ref[...], cb_ref[...]], axis=0)
    mod = jnp.dot(_silu(c), w_ref[...], preferred_element_type=F32) + b_ref[...]
    for b in range(mod.shape[0]):
        o_ref[b] = mod[b:b + 1, :]


def _adaln_mod(c_a, c_b, w_ada, b_ada):
    na, nb = c_a.shape[0], c_b.shape[0]
    n_out = w_ada.shape[1]
    blk = D_MODEL
    return pl.pallas_call(
        _mod_kernel,
        grid=(n_out // blk,),
        in_specs=[
            pl.BlockSpec((na, D_MODEL), lambda j: (0, 0)),
            pl.BlockSpec((nb, D_MODEL), lambda j: (0, 0)),
            pl.BlockSpec((D_MODEL, blk), lambda j: (0, j)),
            pl.BlockSpec((1, blk), lambda j: (0, j)),
        ],
        out_specs=pl.BlockSpec((na + nb, 1, blk), lambda j: (0, 0, j)),
        out_shape=jax.ShapeDtypeStruct((na + nb, 1, n_out), F32),
        name="adaln_mod",
    )(c_a, c_b, w_ada, b_ada.reshape(1, n_out))


def _cast_kernel(i_ref, o_ref):
    o_ref[...] = i_ref[...].T.astype(o_ref.dtype)


def _transposed_rows_bf16(wt, n_rows, blk):
    k = wt.shape[1]
    return pl.pallas_call(
        _cast_kernel,
        grid=(n_rows // blk,),
        in_specs=[pl.BlockSpec((blk, k), lambda j: (j, 0))],
        out_specs=pl.BlockSpec((k, blk), lambda j: (0, j)),
        out_shape=jax.ShapeDtypeStruct((k, n_rows), BF16),
        name="cast_bf16",
    )(wt)


def _plain_cast_kernel(i_ref, o_ref):
    o_ref[...] = i_ref[...].astype(o_ref.dtype)


def _rows_bf16(w, blk):
    n, k = w.shape
    return pl.pallas_call(
        _plain_cast_kernel,
        grid=(n // blk,),
        in_specs=[pl.BlockSpec((blk, k), lambda j: (j, 0))],
        out_specs=pl.BlockSpec((blk, k), lambda j: (j, 0)),
        out_shape=jax.ShapeDtypeStruct((n, k), BF16),
        name="cast_rows_bf16",
    )(w)


def _copy_kernel(i_ref, o_ref):
    o_ref[...] = i_ref[...]


def _row_block(wt, blk_index, blk):
    k = wt.shape[1]
    return pl.pallas_call(
        _copy_kernel,
        grid=(1,),
        in_specs=[pl.BlockSpec((blk, k), lambda j: (blk_index, 0))],
        out_specs=pl.BlockSpec((blk, k), lambda j: (0, 0)),
        out_shape=jax.ShapeDtypeStruct((blk, k), wt.dtype),
        name="row_block",
    )(wt)


def _seg_scan(x, tl, seg, combine, fill):
    k = 1
    while k < seg:
        shifted = pltpu.roll(x, k, 1)
        x = combine(x, jnp.where(tl >= k, shifted, fill))
        k *= 2
    return x


def _lanes_of(per_batch_cols, TB):
    parts = [jnp.broadcast_to(c, (SUB, TB)) for c in per_batch_cols]
    return parts[0] if len(parts) == 1 else jnp.concatenate(parts, axis=1)


def _layer_kernel(*refs, BB, TB, start, has_state):
    R = BB * TB
    o_gate_early = TB < MLSTM_CHUNK
    D, W, DH, GW = D_MODEL, W_POOL, HEAD_DIM, POOL_GW
    it = iter(refs)
    x_ref = next(it)
    mod_ref = next(it)
    if has_state:
        pool0_ref, C0_ref, n0_ref, m0_ref = next(it), next(it), next(it), next(it)
    gnorm_ref, win_ref, wg_ref, gbias_ref = next(it), next(it), next(it), next(it)
    wpool_ref, pscale_ref, ghead_ref, wout_ref, gfinal_ref = (next(it), next(it), next(it),
                                                             next(it), next(it))
    y_ref, pool_out_ref, C_ref, n_ref, m_ref = next(it), next(it), next(it), next(it), next(it)
    h_s, ext_s, pooled_s, zp_s, hm_s, q_s, k_s, v_s, ycat_s, gain_s, og_s = it

    t = pl.program_id(1)
    b0 = pl.program_id(0) * BB

    @pl.when(t == 0)
    def _init():
        if has_state:
            ext_s[:, 0:1, :] = jnp.zeros((BB, 1, W), F32)
            for bb in range(BB):
                for j in range(POOL_BUF):
                    ext_s[bb, 1 + j:2 + j, :] = pool0_ref[j, pl.ds(b0 + bb, 1), :]
            C_ref[...] = C0_ref[...]
            n_ref[...] = n0_ref[...]
            m_ref[...] = m0_ref[...]
        else:
            ext_s[:, 0:HIST, :] = jnp.zeros((BB, HIST, W), F32)
            C_ref[...] = jnp.zeros_like(C_ref)
            n_ref[...] = jnp.zeros_like(n_ref)
            m_ref[...] = jnp.zeros_like(m_ref)
        gain_s[...] = gnorm_ref[...] * (1.0 + mod_ref[:, :, D:2 * D])

    for bb in range(BB):
        x = x_ref[bb]
        r = lax.rsqrt(jnp.mean(x * x, axis=-1, keepdims=True) + EPS)
        shift = mod_ref[bb, :, 0:D]
        h = (x * r) * gain_s[bb] + shift
        h_s[bb * TB:(bb + 1) * TB, :] = h.astype(BF16)

    def proj(sec):
        return jnp.dot(h_s[...], win_ref[:, sec * W:(sec + 1) * W], preferred_element_type=F32)

    gates = jnp.dot(h_s[...], wg_ref[...], preferred_element_type=F32) + gbias_ref[...]

    g8 = gates.T[0:SUB, :]
    tl = lax.broadcasted_iota(jnp.int32, (SUB, R), 1) & (TB - 1)
    ig = g8
    lf = _log_sigmoid(pltpu.roll(g8, SUB - N_HEADS, 0))
    Fc = _seg_scan(lf, tl, TB, jnp.add, 0.0)
    a = ig - Fc
    cmax = _seg_scan(a, tl, TB, jnp.maximum, -jnp.inf)
    m0_cols = [m_ref[bb][:, 0:1] for bb in range(BB)]
    m0 = _lanes_of(m0_cols, TB)
    m = Fc + jnp.maximum(m0, cmax)
    Fl_cols = [Fc[:, (bb + 1) * TB - 1:(bb + 1) * TB] for bb in range(BB)]
    mL_cols = [m[:, (bb + 1) * TB - 1:(bb + 1) * TB] for bb in range(BB)]
    Fl = _lanes_of(Fl_cols, TB)
    mL = _lanes_of(mL_cols, TB)
    per_row = jnp.concatenate(
        [Fc - m, jnp.exp(m0 + Fc - m), jnp.exp(a + Fl - mL), jnp.exp(-m),
         jnp.zeros((LANES - 4 * SUB, R), F32)], axis=0).T
    Fm, decay0 = per_row[:, 0:SUB], per_row[:, SUB:2 * SUB]
    wL, emm = per_row[:, 2 * SUB:3 * SUB], per_row[:, 3 * SUB:4 * SUB]

    xp = proj(0)
    for bb in range(BB):
        ext_s[bb, HIST:HIST + TB, :] = xp[bb * TB:(bb + 1) * TB, :]

    pos_head = start + t * TB + lax.broadcasted_iota(jnp.int32, (HIST, 1), 0)
    for g, w in enumerate(POOL_WINDOWS):
        cols = slice(g * GW, (g + 1) * GW)
        inv_head = 1.0 / jnp.minimum(pos_head + 1, w).astype(F32)
        for bb in range(BB):
            ext = ext_s[bb, :, cols]
            win = ext
            s = 1
            while s < w:
                win = win + pltpu.roll(win, s, 0)
                s *= 2
            r0 = bb * TB
            head = win[HIST:2 * HIST, :] * inv_head - ext[HIST:2 * HIST, :]
            tail = win[2 * HIST:, :] * (1.0 / w) - ext[2 * HIST:, :]
            pooled_s[r0:r0 + HIST, cols] = head.astype(BF16)
            pooled_s[r0 + HIST:r0 + TB, cols] = tail.astype(BF16)

    zp_s[...] = _silu(proj(1))
    q_s[...] = proj(2).astype(BF16)
    for g in range(N_POOL_GROUPS):
        cols = slice(g * GW, (g + 1) * GW)
        mixed = jnp.dot(pooled_s[:, cols], wpool_ref[g], preferred_element_type=F32)
        ycat_s[:, cols] = (mixed * pscale_ref[:, cols] * zp_s[:, cols]).astype(BF16)
    k_s[...] = (proj(3) * (DH ** -0.5)).astype(BF16)
    v_s[...] = proj(4).astype(BF16)
    if o_gate_early:
        og_s[...] = _sigmoid(proj(5))

    for bb in range(BB):
        ext_s[bb, 0:HIST, :] = ext_s[bb, TB:TB + HIST, :]

    causal = (lax.broadcasted_iota(jnp.int32, (TB, TB), 0)
              >= lax.broadcasted_iota(jnp.int32, (TB, TB), 1))

    for bb in range(BB):
        rows = slice(bb * TB, (bb + 1) * TB)
        dL = jnp.exp(m0_cols[bb] + Fl_cols[bb] - mL_cols[bb])
        for hd in range(N_HEADS):
            cols = slice(hd * DH, (hd + 1) * DH)
            q = q_s[rows, cols]
            k = k_s[rows, cols]
            v = v_s[rows, cols]
            s = lax.dot_general(q, k, (((1,), (1,)), ((), ())), preferred_element_type=F32)
            logD = Fm[rows, hd:hd + 1] + a[hd:hd + 1, rows]
            S = s * jnp.exp(jnp.where(causal, logD, -jnp.inf))
            C0 = C_ref[bb, hd]
            n0 = n_ref[bb, hd:hd + 1, :]
            d0 = decay0[rows, hd:hd + 1]
            qn = jnp.sum(q.astype(F32) * n0, axis=-1, keepdims=True)
            nq = jnp.sum(S, axis=-1, keepdims=True) + d0 * qn
            num = (jnp.dot(S.astype(BF16), v, preferred_element_type=F32)
                   + d0 * jnp.dot(q, C0.astype(BF16), preferred_element_type=F32))
            den = jnp.maximum(jnp.abs(nq), emm[rows, hd:hd + 1])
            hh = num * (1.0 / den)
            hc = hh - jnp.mean(hh, axis=-1, keepdims=True)
            hn = hc * lax.rsqrt(jnp.mean(hc * hc, axis=-1, keepdims=True) + EPS)
            hm = hn * ghead_ref[:, cols]
            hm_s[rows, cols] = hm * og_s[rows, cols] if o_gate_early else hm
            kw = k.astype(F32) * wL[rows, hd:hd + 1]
            dl = dL[hd:hd + 1, :]
            C_ref[bb, hd] = dl * C0 + lax.dot_general(
                kw.astype(BF16), v, (((0,), (0,)), ((), ())), preferred_element_type=F32)
            n_ref[bb, hd:hd + 1, :] = dl * n0 + jnp.sum(kw, axis=0, keepdims=True)
        m_ref[bb] = jnp.broadcast_to(mL_cols[bb], (SUB, LANES))

    if not o_gate_early:
        hm_s[...] = hm_s[...] * _sigmoid(proj(5))
    ycat_s[:, W:2 * W] = (hm_s[...] * _silu(proj(6))).astype(BF16)

    out_cols = OUT_COLS if R >= MIN_ROWS_FOR_COLUMN_BLOCKS else D
    ssq = [jnp.zeros((TB, 1), F32) for _ in range(BB)]
    for c0 in range(0, D, out_cols):
        cols = slice(c0, c0 + out_cols)
        y = jnp.dot(ycat_s[...], wout_ref[:, cols], preferred_element_type=F32)
        for bb in range(BB):
            gate = mod_ref[bb, :, 2 * D + c0:2 * D + c0 + out_cols]
            xn = x_ref[bb, :, cols] + gate * y[bb * TB:(bb + 1) * TB, :]
            ssq[bb] = ssq[bb] + jnp.sum(xn * xn, axis=-1, keepdims=True)
            y_ref[bb, :, cols] = xn
    for bb in range(BB):
        r = lax.rsqrt(ssq[bb] * (1.0 / D) + EPS)
        y_ref[bb] = (y_ref[bb] * r) * gfinal_ref[...]

    @pl.when(t == pl.num_programs(1) - 1)
    def _emit_pool():
        for bb in range(BB):
            for j in range(POOL_BUF):
                pool_out_ref[j, pl.ds(b0 + bb, 1), :] = ext_s[bb, 1 + j:2 + j, :]


def _const_spec(shape):
    nd = len(shape)
    return pl.BlockSpec(shape, lambda b, t: (0,) * nd, pipeline_mode=pl.Buffered(1))


def _block_plan(n_batch, n_tokens, has_state):
    tb = min(n_tokens, MLSTM_CHUNK)
    c_bytes = N_HEADS * HEAD_DIM * HEAD_DIM * 4
    n_windows = 4 if has_state else 2
    bb = min(n_batch, MAX_STEP_ROWS // tb, STATE_WINDOW_BYTES // (n_windows * c_bytes))
    assert n_tokens % tb == 0 and n_batch % bb == 0 and tb >= 2 * HIST
    return bb, tb


def _run_layer(x, mod, mod_row0, state, weights, *, start):
    B, T, D = x.shape
    W, H, DH = W_POOL, N_HEADS, HEAD_DIM
    has_state = state is not None
    BB, TB = _block_plan(B, T, has_state)
    R = BB * TB
    grid = (B // BB, T // TB)
    assert mod_row0 % BB == 0
    mod_blk0 = mod_row0 // BB

    in_specs = [
        pl.BlockSpec((BB, TB, D), lambda b, t: (b, t, 0)),
        pl.BlockSpec((BB, 1, 3 * D), lambda b, t: (b + mod_blk0, 0, 0)),
    ]
    args = [x, mod]
    if has_state:
        pool0, C0, n0, m0 = state
        in_specs += [
            pl.BlockSpec((POOL_BUF, B, W), lambda b, t: (0, 0, 0), pipeline_mode=pl.Buffered(1)),
            pl.BlockSpec((BB, H, DH, DH), lambda b, t: (b, 0, 0, 0)),
            pl.BlockSpec((BB, H, DH), lambda b, t: (b, 0, 0)),
            pl.BlockSpec((BB, SUB, LANES), lambda b, t: (b, 0, 0)),
        ]
        m0p = jnp.broadcast_to(jnp.pad(m0, ((0, 0), (0, SUB - H)))[:, :, None], (B, SUB, LANES))
        args += [jnp.transpose(pool0, (1, 0, 2)), C0, n0, m0p]
    in_specs += [_const_spec(w.shape) for w in weights]
    args += list(weights)

    out_shape = (
        jax.ShapeDtypeStruct((B, T, D), F32),
        jax.ShapeDtypeStruct((POOL_BUF, B, W), F32),
        jax.ShapeDtypeStruct((B, H, DH, DH), F32),
        jax.ShapeDtypeStruct((B, H, DH), F32),
        jax.ShapeDtypeStruct((B, SUB, LANES), F32),
    )
    out_specs = (
        pl.BlockSpec((BB, TB, D), lambda b, t: (b, t, 0)),
        pl.BlockSpec((POOL_BUF, B, W), lambda b, t: (0, 0, 0)),
        pl.BlockSpec((BB, H, DH, DH), lambda b, t: (b, 0, 0, 0)),
        pl.BlockSpec((BB, H, DH), lambda b, t: (b, 0, 0)),
        pl.BlockSpec((BB, SUB, LANES), lambda b, t: (b, 0, 0)),
    )
    scratch = [
        pltpu.VMEM((R, D), BF16),
        pltpu.VMEM((BB, HIST + TB, W), F32),
        pltpu.VMEM((R, W), BF16),
        pltpu.VMEM((R, W), F32),
        pltpu.VMEM((R, W), F32),
        pltpu.VMEM((R, W), BF16),
        pltpu.VMEM((R, W), BF16),
        pltpu.VMEM((R, W), BF16),
        pltpu.VMEM((R, 2 * W), BF16),
        pltpu.VMEM((BB, 1, D), F32),
        pltpu.VMEM((R, W) if TB < MLSTM_CHUNK else (SUB, LANES), F32),
    ]
    y, pool, C, n, m = pl.pallas_call(
        functools.partial(_layer_kernel, BB=BB, TB=TB, start=start, has_state=has_state),
        grid=grid,
        in_specs=in_specs,
        out_specs=out_specs,
        out_shape=out_shape,
        scratch_shapes=scratch,
        compiler_params=pltpu.CompilerParams(
            dimension_semantics=("arbitrary", "arbitrary"),
            vmem_limit_bytes=VMEM_LIMIT_BYTES),
        name="layer_state" if has_state else "layer_fresh",
    )(*args)
    return y, jnp.transpose(pool, (1, 0, 2)), C, n, m[:, :H, 0]


def kernel(x_prompt, x_sample, c_prompt, c_sample, state_pool, state_C, state_n, state_m,
           w_ada, b_ada, g_norm, w_in, b_i, b_f, w_pool, pool_scale, g_head, w_out, g_final):
    depth = w_ada.shape[0]
    assert depth == 1, "single-layer trunk"
    l = 0
    nbp = x_prompt.shape[0]
    n_main = N_MAIN_SECTIONS * W_POOL

    mod = _adaln_mod(c_prompt, c_sample, w_ada[l], b_ada[l])

    w_in_t = jnp.swapaxes(w_in[l], 0, 1)
    n_gate = 2 * N_HEADS
    assert n_main % n_gate == 0 and w_in_t.shape[0] == n_main + n_gate
    w_gate_t = _row_block(w_in_t, n_main // n_gate, n_gate)
    w_gate = jnp.pad(w_gate_t.T, ((0, 0), (0, LANES - n_gate))).astype(BF16)
    gate_bias = jnp.pad(jnp.concatenate([b_i[l], b_f[l]]), (0, LANES - 2 * N_HEADS)).reshape(1, LANES)
    weights = (
        g_norm[l].reshape(1, D_MODEL),
        _transposed_rows_bf16(w_in_t, n_main, W_POOL),
        w_gate,
        gate_bias,
        w_pool[l].astype(BF16),
        pool_scale[l].reshape(1, W_POOL),
        g_head[l].reshape(1, W_MLSTM),
        _rows_bf16(w_out[l], 512),
        g_final.reshape(1, D_MODEL),
    )

    yp, pp, pc, pn, pm = _run_layer(x_prompt, mod, 0, None, weights, start=0)
    ys, sp, sc, sn, sm = _run_layer(
        x_sample, mod, nbp, (state_pool[l], state_C[l], state_n[l], state_m[l]), weights,
        start=PAST_LEN)
    return (yp, ys, pp[None], pc[None], pn[None], pm[None], sp[None], sc[None], sn[None], sm[None])
```

```python
import functools

import jax
import jax.numpy as jnp
from jax import lax
from jax.experimental import pallas as pl
from jax.experimental.pallas import tpu as pltpu

F32 = jnp.float32
BF16 = jnp.bfloat16

D_MODEL = 1024
W_POOL = 1024
W_MLSTM = 1024
POOL_WINDOWS = (2, 4, 8, 16)
N_POOL_GROUPS = 4
POOL_GW = W_POOL // N_POOL_GROUPS
POOL_BUF = 15
N_HEADS = 4
HEAD_DIM = W_MLSTM // N_HEADS
EPS = 1e-6
N_MAIN_SECTIONS = 7
LANES = 128
SUB = 8
PAST_LEN = 2048
HIST = 16
OUT_COLS = 256
MIN_ROWS_FOR_COLUMN_BLOCKS = 256
VMEM_LIMIT_BYTES = 56 * 1024 * 1024
MLSTM_CHUNK = 256
MAX_STEP_ROWS = 512
STATE_WINDOW_BYTES = 16 * 1024 * 1024


def _sigmoid(z):
    return 0.5 * jnp.tanh(0.5 * z) + 0.5


def _silu(z):
    hz = 0.5 * z
    return hz * jnp.tanh(hz) + hz


def _log_sigmoid(z):
    return jnp.minimum(z, 0.0) - jnp.log1p(jnp.exp(-jnp.abs(z)))


def _mod_kernel(ca_ref, cb_ref, w_ref, b_ref, o_ref):
    c = jnp.concatenate([ca_ref[...], cb_ref[...]], axis=0)
    mod = jnp.dot(_silu(c), w_ref[...], preferred_element_type=F32) + b_ref[...]
    for b in range(mod.shape[0]):
        o_ref[b] = mod[b:b + 1, :]


def _adaln_mod(c_a, c_b, w_ada, b_ada):
    na, nb = c_a.shape[0], c_b.shape[0]
    n_out = w_ada.shape[1]
    blk = D_MODEL
    return pl.pallas_call(
        _mod_kernel,
        grid=(n_out // blk,),
        in_specs=[
            pl.BlockSpec((na, D_MODEL), lambda j: (0, 0)),
            pl.BlockSpec((nb, D_MODEL), lambda j: (0, 0)),
            pl.BlockSpec((D_MODEL, blk), lambda j: (0, j)),
            pl.BlockSpec((1, blk), lambda j: (0, j)),
        ],
        out_specs=pl.BlockSpec((na + nb, 1, blk), lambda j: (0, 0, j)),
        out_shape=jax.ShapeDtypeStruct((na + nb, 1, n_out), F32),
        name="adaln_mod",
    )(c_a, c_b, w_ada, b_ada.reshape(1, n_out))


def _cast_kernel(i_ref, o_ref):
    o_ref[...] = i_ref[...].T.astype(o_ref.dtype)


def _transposed_rows_bf16(wt, n_rows, blk):
    k = wt.shape[1]
    return pl.pallas_call(
        _cast_kernel,
        grid=(n_rows // blk,),
        in_specs=[pl.BlockSpec((blk, k), lambda j: (j, 0))],
        out_specs=pl.BlockSpec((k, blk), lambda j: (0, j)),
        out_shape=jax.ShapeDtypeStruct((k, n_rows), BF16),
        name="cast_bf16",
    )(wt)


def _plain_cast_kernel(i_ref, o_ref):
    o_ref[...] = i_ref[...].astype(o_ref.dtype)


def _rows_bf16(w, blk):
    n, k = w.shape
    return pl.pallas_call(
        _plain_cast_kernel,
        grid=(n // blk,),
        in_specs=[pl.BlockSpec((blk, k), lambda j: (j, 0))],
        out_specs=pl.BlockSpec((blk, k), lambda j: (j, 0)),
        out_shape=jax.ShapeDtypeStruct((n, k), BF16),
        name="cast_rows_bf16",
    )(w)


def _copy_kernel(i_ref, o_ref):
    o_ref[...] = i_ref[...]


def _row_block(wt, blk_index, blk):
    k = wt.shape[1]
    return pl.pallas_call(
        _copy_kernel,
        grid=(1,),
        in_specs=[pl.BlockSpec((blk, k), lambda j: (blk_index, 0))],
        out_specs=pl.BlockSpec((blk, k), lambda j: (0, 0)),
        out_shape=jax.ShapeDtypeStruct((blk, k), wt.dtype),
        name="row_block",
    )(wt)


def _seg_scan(x, tl, seg, combine, fill):
    k = 1
    while k < seg:
        shifted = pltpu.roll(x, k, 1)
        x = combine(x, jnp.where(tl >= k, shifted, fill))
        k *= 2
    return x


def _lanes_of(per_batch_cols, TB):
    parts = [jnp.broadcast_to(c, (SUB, TB)) for c in per_batch_cols]
    return parts[0] if len(parts) == 1 else jnp.concatenate(parts, axis=1)


def _layer_kernel(*refs, BB, TB, start, has_state):
    R = BB * TB
    o_gate_early = TB < MLSTM_CHUNK
    D, W, DH, GW = D_MODEL, W_POOL, HEAD_DIM, POOL_GW
    it = iter(refs)
    x_ref = next(it)
    mod_ref = next(it)
    if has_state:
        pool0_ref, C0_ref, n0_ref, m0_ref = next(it), next(it), next(it), next(it)
    gnorm_ref, win_ref, wg_ref, gbias_ref = next(it), next(it), next(it), next(it)
    wpool_ref, pscale_ref, ghead_ref, wout_ref, gfinal_ref = (next(it), next(it), next(it),
                                                             next(it), next(it))
    y_ref, pool_out_ref, C_ref, n_ref, m_ref = next(it), next(it), next(it), next(it), next(it)
    h_s, ext_s, pooled_s, zp_s, hm_s, q_s, k_s, v_s, ycat_s, gain_s, og_s = it

    t = pl.program_id(1)
    b0 = pl.program_id(0) * BB

    @pl.when(t == 0)
    def _init():
        if has_state:
            ext_s[:, 0:1, :] = jnp.zeros((BB, 1, W), F32)
            for bb in range(BB):
                for j in range(POOL_BUF):
                    ext_s[bb, 1 + j:2 + j, :] = pool0_ref[j, pl.ds(b0 + bb, 1), :]
            C_ref[...] = C0_ref[...]
            n_ref[...] = n0_ref[...]
            m_ref[...] = m0_ref[...]
        else:
            ext_s[:, 0:HIST, :] = jnp.zeros((BB, HIST, W), F32)
            C_ref[...] = jnp.zeros_like(C_ref)
            n_ref[...] = jnp.zeros_like(n_ref)
            m_ref[...] = jnp.zeros_like(m_ref)
        gain_s[...] = gnorm_ref[...] * (1.0 + mod_ref[:, :, D:2 * D])

    for bb in range(BB):
        x = x_ref[bb]
        r = lax.rsqrt(jnp.mean(x * x, axis=-1, keepdims=True) + EPS)
        shift = mod_ref[bb, :, 0:D]
        h = (x * r) * gain_s[bb] + shift
        h_s[bb * TB:(bb + 1) * TB, :] = h.astype(BF16)

    def proj(sec):
        return jnp.dot(h_s[...], win_ref[:, sec * W:(sec + 1) * W], preferred_element_type=F32)

    gates = jnp.dot(h_s[...], wg_ref[...], preferred_element_type=F32) + gbias_ref[...]

    g8 = gates.T[0:SUB, :]
    tl = lax.broadcasted_iota(jnp.int32, (SUB, R), 1) & (TB - 1)
    ig = g8
    lf = _log_sigmoid(pltpu.roll(g8, SUB - N_HEADS, 0))
    Fc = _seg_scan(lf, tl, TB, jnp.add, 0.0)
    a = ig - Fc
    cmax = _seg_scan(a, tl, TB, jnp.maximum, -jnp.inf)
    m0_cols = [m_ref[bb][:, 0:1] for bb in range(BB)]
    m0 = _lanes_of(m0_cols, TB)
    m = Fc + jnp.maximum(m0, cmax)
    Fl_cols = [Fc[:, (bb + 1) * TB - 1:(bb + 1) * TB] for bb in range(BB)]
    mL_cols = [m[:, (bb + 1) * TB - 1:(bb + 1) * TB] for bb in range(BB)]
    Fl = _lanes_of(Fl_cols, TB)
    mL = _lanes_of(mL_cols, TB)
    per_row = jnp.concatenate(
        [Fc - m, jnp.exp(m0 + Fc - m), jnp.exp(a + Fl - mL), jnp.exp(-m),
         jnp.zeros((LANES - 4 * SUB, R), F32)], axis=0).T
    Fm, decay0 = per_row[:, 0:SUB], per_row[:, SUB:2 * SUB]
    wL, emm = per_row[:, 2 * SUB:3 * SUB], per_row[:, 3 * SUB:4 * SUB]

    xp = proj(0)
    for bb in range(BB):
        ext_s[bb, HIST:HIST + TB, :] = xp[bb * TB:(bb + 1) * TB, :]

    pos_head = start + t * TB + lax.broadcasted_iota(jnp.int32, (HIST, 1), 0)
    for g, w in enumerate(POOL_WINDOWS):
        cols = slice(g * GW, (g + 1) * GW)
        inv_head = 1.0 / jnp.minimum(pos_head + 1, w).astype(F32)
        for bb in range(BB):
            ext = ext_s[bb, :, cols]
            win = ext
            s = 1
            while s < w:
                win = win + pltpu.roll(win, s, 0)
                s *= 2
            r0 = bb * TB
            head = win[HIST:2 * HIST, :] * inv_head - ext[HIST:2 * HIST, :]
            tail = win[2 * HIST:, :] * (1.0 / w) - ext[2 * HIST:, :]
            pooled_s[r0:r0 + HIST, cols] = head.astype(BF16)
            pooled_s[r0 + HIST:r0 + TB, cols] = tail.astype(BF16)

    zp_s[...] = _silu(proj(1))
    for g in range(N_POOL_GROUPS):
        cols = slice(g * GW, (g + 1) * GW)
        mixed = jnp.dot(pooled_s[:, cols], wpool_ref[g], preferred_element_type=F32)
        ycat_s[:, cols] = (mixed * pscale_ref[:, cols] * zp_s[:, cols]).astype(BF16)
    q_s[...] = proj(2).astype(BF16)
    k_s[...] = (proj(3) * (DH ** -0.5)).astype(BF16)
    v_s[...] = proj(4).astype(BF16)
    if o_gate_early:
        og_s[...] = _sigmoid(proj(5))

    for bb in range(BB):
        ext_s[bb, 0:HIST, :] = ext_s[bb, TB:TB + HIST, :]

    causal = (lax.broadcasted_iota(jnp.int32, (TB, TB), 0)
              >= lax.broadcasted_iota(jnp.int32, (TB, TB), 1))

    dLs = [jnp.exp(m0_cols[bb] + Fl_cols[bb] - mL_cols[bb]) for bb in range(BB)]
    for hd in range(N_HEADS):
        for bb in range(BB):
            rows = slice(bb * TB, (bb + 1) * TB)
            dL = dLs[bb]
            cols = slice(hd * DH, (hd + 1) * DH)
            q = q_s[rows, cols]
            k = k_s[rows, cols]
            v = v_s[rows, cols]
            s = lax.dot_general(q, k, (((1,), (1,)), ((), ())), preferred_element_type=F32)
            logD = Fm[rows, hd:hd + 1] + a[hd:hd + 1, rows]
            S = s * jnp.exp(jnp.where(causal, logD, -jnp.inf))
            C0 = C_ref[bb, hd]
            n0 = n_ref[bb, hd:hd + 1, :]
            d0 = decay0[rows, hd:hd + 1]
            qn = jnp.sum(q.astype(F32) * n0, axis=-1, keepdims=True)
            nq = jnp.sum(S, axis=-1, keepdims=True) + d0 * qn
            num = (jnp.dot(S.astype(BF16), v, preferred_element_type=F32)
                   + d0 * jnp.dot(q, C0.astype(BF16), preferred_element_type=F32))
            den = jnp.maximum(jnp.abs(nq), emm[rows, hd:hd + 1])
            hh = num * (1.0 / den)
            hc = hh - jnp.mean(hh, axis=-1, keepdims=True)
            hn = hc * lax.rsqrt(jnp.mean(hc * hc, axis=-1, keepdims=True) + EPS)
            hm = hn * ghead_ref[:, cols]
            hm_s[rows, cols] = hm * og_s[rows, cols] if o_gate_early else hm
            kw = k.astype(F32) * wL[rows, hd:hd + 1]
            dl = dL[hd:hd + 1, :]
            C_ref[bb, hd] = dl * C0 + lax.dot_general(
                kw.astype(BF16), v, (((0,), (0,)), ((), ())), preferred_element_type=F32)
            n_ref[bb, hd:hd + 1, :] = dl * n0 + jnp.sum(kw, axis=0, keepdims=True)
    for bb in range(BB):
        m_ref[bb] = jnp.broadcast_to(mL_cols[bb], (SUB, LANES))

    if not o_gate_early:
        hm_s[...] = hm_s[...] * _sigmoid(proj(5))
    ycat_s[:, W:2 * W] = (hm_s[...] * _silu(proj(6))).astype(BF16)

    out_cols = OUT_COLS if R >= MIN_ROWS_FOR_COLUMN_BLOCKS else D
    ssq = [jnp.zeros((TB, 1), F32) for _ in range(BB)]
    for c0 in range(0, D, out_cols):
        cols = slice(c0, c0 + out_cols)
        y = jnp.dot(ycat_s[...], wout_ref[:, cols], preferred_element_type=F32)
        for bb in range(BB):
            gate = mod_ref[bb, :, 2 * D + c0:2 * D + c0 + out_cols]
            xn = x_ref[bb, :, cols] + gate * y[bb * TB:(bb + 1) * TB, :]
            ssq[bb] = ssq[bb] + jnp.sum(xn * xn, axis=-1, keepdims=True)
            y_ref[bb, :, cols] = xn
    for bb in range(BB):
        r = lax.rsqrt(ssq[bb] * (1.0 / D) + EPS)
        y_ref[bb] = (y_ref[bb] * r) * gfinal_ref[...]

    @pl.when(t == pl.num_programs(1) - 1)
    def _emit_pool():
        for bb in range(BB):
            for j in range(POOL_BUF):
                pool_out_ref[j, pl.ds(b0 + bb, 1), :] = ext_s[bb, 1 + j:2 + j, :]


def _const_spec(shape):
    nd = len(shape)
    return pl.BlockSpec(shape, lambda b, t: (0,) * nd, pipeline_mode=pl.Buffered(1))


def _block_plan(n_batch, n_tokens, has_state):
    tb = min(n_tokens, MLSTM_CHUNK)
    c_bytes = N_HEADS * HEAD_DIM * HEAD_DIM * 4
    n_windows = 4 if has_state else 2
    bb = min(n_batch, MAX_STEP_ROWS // tb, STATE_WINDOW_BYTES // (n_windows * c_bytes))
    assert n_tokens % tb == 0 and n_batch % bb == 0 and tb >= 2 * HIST
    return bb, tb


def _run_layer(x, mod, mod_row0, state, weights, *, start):
    B, T, D = x.shape
    W, H, DH = W_POOL, N_HEADS, HEAD_DIM
    has_state = state is not None
    BB, TB = _block_plan(B, T, has_state)
    R = BB * TB
    grid = (B // BB, T // TB)
    assert mod_row0 % BB == 0
    mod_blk0 = mod_row0 // BB

    in_specs = [
        pl.BlockSpec((BB, TB, D), lambda b, t: (b, t, 0)),
        pl.BlockSpec((BB, 1, 3 * D), lambda b, t: (b + mod_blk0, 0, 0)),
    ]
    args = [x, mod]
    if has_state:
        pool0, C0, n0, m0 = state
        in_specs += [
            pl.BlockSpec((POOL_BUF, B, W), lambda b, t: (0, 0, 0), pipeline_mode=pl.Buffered(1)),
            pl.BlockSpec((BB, H, DH, DH), lambda b, t: (b, 0, 0, 0)),
            pl.BlockSpec((BB, H, DH), lambda b, t: (b, 0, 0)),
            pl.BlockSpec((BB, SUB, LANES), lambda b, t: (b, 0, 0)),
        ]
        m0p = jnp.broadcast_to(jnp.pad(m0, ((0, 0), (0, SUB - H)))[:, :, None], (B, SUB, LANES))
        args += [jnp.transpose(pool0, (1, 0, 2)), C0, n0, m0p]
    in_specs += [_const_spec(w.shape) for w in weights]
    args += list(weights)

    out_shape = (
        jax.ShapeDtypeStruct((B, T, D), F32),
        jax.ShapeDtypeStruct((POOL_BUF, B, W), F32),
        jax.ShapeDtypeStruct((B, H, DH, DH), F32),
        jax.ShapeDtypeStruct((B, H, DH), F32),
        jax.ShapeDtypeStruct((B, SUB, LANES), F32),
    )
    out_specs = (
        pl.BlockSpec((BB, TB, D), lambda b, t: (b, t, 0)),
        pl.BlockSpec((POOL_BUF, B, W), lambda b, t: (0, 0, 0)),
        pl.BlockSpec((BB, H, DH, DH), lambda b, t: (b, 0, 0, 0)),
        pl.BlockSpec((BB, H, DH), lambda b, t: (b, 0, 0)),
        pl.BlockSpec((BB, SUB, LANES), lambda b, t: (b, 0, 0)),
    )
    scratch = [
        pltpu.VMEM((R, D), BF16),
        pltpu.VMEM((BB, HIST + TB, W), F32),
        pltpu.VMEM((R, W), BF16),
        pltpu.VMEM((R, W), F32),
        pltpu.VMEM((R, W), F32),
        pltpu.VMEM((R, W), BF16),
        pltpu.VMEM((R, W), BF16),
        pltpu.VMEM((R, W), BF16),
        pltpu.VMEM((R, 2 * W), BF16),
        pltpu.VMEM((BB, 1, D), F32),
        pltpu.VMEM((R, W) if TB < MLSTM_CHUNK else (SUB, LANES), F32),
    ]
    y, pool, C, n, m = pl.pallas_call(
        functools.partial(_layer_kernel, BB=BB, TB=TB, start=start, has_state=has_state),
        grid=grid,
        in_specs=in_specs,
        out_specs=out_specs,
        out_shape=out_shape,
        scratch_shapes=scratch,
        compiler_params=pltpu.CompilerParams(
            dimension_semantics=("arbitrary", "arbitrary"),
            vmem_limit_bytes=VMEM_LIMIT_BYTES),
        name="layer_state" if has_state else "layer_fresh",
    )(*args)
    return y, jnp.transpose(pool, (1, 0, 2)), C, n, m[:, :H, 0]


def kernel(x_prompt, x_sample, c_prompt, c_sample, state_pool, state_C, state_n, state_m,
           w_ada, b_ada, g_norm, w_in, b_i, b_f, w_pool, pool_scale, g_head, w_out, g_final):
    depth = w_ada.shape[0]
    assert depth == 1, "single-layer trunk"
    l = 0
    nbp = x_prompt.shape[0]
    n_main = N_MAIN_SECTIONS * W_POOL

    mod = _adaln_mod(c_prompt, c_sample, w_ada[l], b_ada[l])

    w_in_t = jnp.swapaxes(w_in[l], 0, 1)
    n_gate = 2 * N_HEADS
    assert n_main % n_gate == 0 and w_in_t.shape[0] == n_main + n_gate
    w_gate_t = _row_block(w_in_t, n_main // n_gate, n_gate)
    w_gate = jnp.pad(w_gate_t.T, ((0, 0), (0, LANES - n_gate))).astype(BF16)
    gate_bias = jnp.pad(jnp.concatenate([b_i[l], b_f[l]]), (0, LANES - 2 * N_HEADS)).reshape(1, LANES)
    weights = (
        g_norm[l].reshape(1, D_MODEL),
        _transposed_rows_bf16(w_in_t, n_main, W_POOL),
        w_gate,
        gate_bias,
        w_pool[l].astype(BF16),
        pool_scale[l].reshape(1, W_POOL),
        g_head[l].reshape(1, W_MLSTM),
        _rows_bf16(w_out[l], 512),
        g_final.reshape(1, D_MODEL),
    )

    yp, pp, pc, pn, pm = _run_layer(x_prompt, mod, 0, None, weights, start=0)
    ys, sp, sc, sn, sm = _run_layer(
        x_sample, mod, nbp, (state_pool[l], state_C[l], state_n[l], state_m[l]), weights,
        start=PAST_LEN)
    return (yp, ys, pp[None], pc[None], pn[None], pm[None], sp[None], sc[None], sn[None], sm[None])
```

```python
import functools

import jax
import jax.numpy as jnp
from jax import lax
from jax.experimental import pallas as pl
from jax.experimental.pallas import tpu as pltpu

F32 = jnp.float32
BF16 = jnp.bfloat16

D_MODEL = 1024
W_POOL = 1024
W_MLSTM = 1024
POOL_WINDOWS = (2, 4, 8, 16)
N_POOL_GROUPS = 4
POOL_GW = W_POOL // N_POOL_GROUPS
POOL_BUF = 15
N_HEADS = 4
HEAD_DIM = W_MLSTM // N_HEADS
EPS = 1e-6
N_MAIN_SECTIONS = 7
LANES = 128
SUB = 8
PAST_LEN = 2048
HIST = 16
OUT_COLS = 256
MIN_ROWS_FOR_COLUMN_BLOCKS = 256
VMEM_LIMIT_BYTES = 56 * 1024 * 1024
MLSTM_CHUNK = 256
MAX_STEP_ROWS = 512
STATE_WINDOW_BYTES = 16 * 1024 * 1024


def _sigmoid(z):
    return 0.5 * jnp.tanh(0.5 * z) + 0.5


def _silu(z):
    hz = 0.5 * z
    return hz * jnp.tanh(hz) + hz


def _log_sigmoid(z):
    return jnp.minimum(z, 0.0) - jnp.log1p(jnp.exp(-jnp.abs(z)))


def _mod_kernel(ca_ref, cb_ref, w_ref, b_ref, o_ref):
    c = jnp.concatenate([ca_ref[...], cb_ref[...]], axis=0)
    mod = jnp.dot(_silu(c), w_ref[...], preferred_element_type=F32) + b_ref[...]
    for b in range(mod.shape[0]):
        o_ref[b] = mod[b:b + 1, :]


def _adaln_mod(c_a, c_b, w_ada, b_ada):
    na, nb = c_a.shape[0], c_b.shape[0]
    n_out = w_ada.shape[1]
    blk = D_MODEL
    return pl.pallas_call(
        _mod_kernel,
        grid=(n_out // blk,),
        in_specs=[
            pl.BlockSpec((na, D_MODEL), lambda j: (0, 0)),
            pl.BlockSpec((nb, D_MODEL), lambda j: (0, 0)),
            pl.BlockSpec((D_MODEL, blk), lambda j: (0, j)),
            pl.BlockSpec((1, blk), lambda j: (0, j)),
        ],
        out_specs=pl.BlockSpec((na + nb, 1, blk), lambda j: (0, 0, j)),
        out_shape=jax.ShapeDtypeStruct((na + nb, 1, n_out), F32),
        name="adaln_mod",
    )(c_a, c_b, w_ada, b_ada.reshape(1, n_out))


def _cast_kernel(i_ref, o_ref):
    o_ref[...] = i_ref[...].T.astype(o_ref.dtype)


def _transposed_rows_bf16(wt, n_rows, blk):
    k = wt.shape[1]
    return pl.pallas_call(
        _cast_kernel,
        grid=(n_rows // blk,),
        in_specs=[pl.BlockSpec((blk, k), lambda j: (j, 0))],
        out_specs=pl.BlockSpec((k, blk), lambda j: (0, j)),
        out_shape=jax.ShapeDtypeStruct((k, n_rows), BF16),
        name="cast_bf16",
    )(wt)


def _plain_cast_kernel(i_ref, o_ref):
    o_ref[...] = i_ref[...].astype(o_ref.dtype)


def _rows_bf16(w, blk):
    n, k = w.shape
    return pl.pallas_call(
        _plain_cast_kernel,
        grid=(n // blk,),
        in_specs=[pl.BlockSpec((blk, k), lambda j: (j, 0))],
        out_specs=pl.BlockSpec((blk, k), lambda j: (j, 0)),
        out_shape=jax.ShapeDtypeStruct((n, k), BF16),
        name="cast_rows_bf16",
    )(w)


def _copy_kernel(i_ref, o_ref):
    o_ref[...] = i_ref[...]


def _row_block(wt, blk_index, blk):
    k = wt.shape[1]
    return pl.pallas_call(
        _copy_kernel,
        grid=(1,),
        in_specs=[pl.BlockSpec((blk, k), lambda j: (blk_index, 0))],
        out_specs=pl.BlockSpec((blk, k), lambda j: (0, 0)),
        out_shape=jax.ShapeDtypeStruct((blk, k), wt.dtype),
        name="row_block",
    )(wt)


def _seg_scan(x, tl, seg, combine, fill):
    k = 1
    while k < seg:
        shifted = pltpu.roll(x, k, 1)
        x = combine(x, jnp.where(tl >= k, shifted, fill))
        k *= 2
    return x


def _lanes_of(per_batch_cols, TB):
    parts = [jnp.broadcast_to(c, (SUB, TB)) for c in per_batch_cols]
    return parts[0] if len(parts) == 1 else jnp.concatenate(parts, axis=1)


def _layer_kernel(*refs, BB, TB, start, has_state):
    R = BB * TB
    o_gate_early = True
    D, W, DH, GW = D_MODEL, W_POOL, HEAD_DIM, POOL_GW
    it = iter(refs)
    x_ref = next(it)
    mod_ref = next(it)
    if has_state:
        pool0_ref, C0_ref, n0_ref, m0_ref = next(it), next(it), next(it), next(it)
    gnorm_ref, win_ref, wg_ref, gbias_ref = next(it), next(it), next(it), next(it)
    wpool_ref, pscale_ref, ghead_ref, wout_ref, gfinal_ref = (next(it), next(it), next(it),
                                                             next(it), next(it))
    y_ref, pool_out_ref, C_ref, n_ref, m_ref = next(it), next(it), next(it), next(it), next(it)
    h_s, ext_s, pooled_s, zp_s, hm_s, q_s, k_s, v_s, ycat_s, gain_s, og_s = it

    t = pl.program_id(1)
    b0 = pl.program_id(0) * BB

    @pl.when(t == 0)
    def _init():
        if has_state:
            ext_s[:, 0:1, :] = jnp.zeros((BB, 1, W), F32)
            for bb in range(BB):
                for j in range(POOL_BUF):
                    ext_s[bb, 1 + j:2 + j, :] = pool0_ref[j, pl.ds(b0 + bb, 1), :]
            C_ref[...] = C0_ref[...]
            n_ref[...] = n0_ref[...]
            m_ref[...] = m0_ref[...]
        else:
            ext_s[:, 0:HIST, :] = jnp.zeros((BB, HIST, W), F32)
            C_ref[...] = jnp.zeros_like(C_ref)
            n_ref[...] = jnp.zeros_like(n_ref)
            m_ref[...] = jnp.zeros_like(m_ref)
        gain_s[...] = gnorm_ref[...] * (1.0 + mod_ref[:, :, D:2 * D])

    for bb in range(BB):
        x = x_ref[bb]
        r = lax.rsqrt(jnp.mean(x * x, axis=-1, keepdims=True) + EPS)
        shift = mod_ref[bb, :, 0:D]
        h = (x * r) * gain_s[bb] + shift
        h_s[bb * TB:(bb + 1) * TB, :] = h.astype(BF16)

    def proj(sec):
        return jnp.dot(h_s[...], win_ref[:, sec * W:(sec + 1) * W], preferred_element_type=F32)

    gates = jnp.dot(h_s[...], wg_ref[...], preferred_element_type=F32) + gbias_ref[...]

    g8 = gates.T[0:SUB, :]
    tl = lax.broadcasted_iota(jnp.int32, (SUB, R), 1) & (TB - 1)
    ig = g8
    lf = _log_sigmoid(pltpu.roll(g8, SUB - N_HEADS, 0))
    Fc = _seg_scan(lf, tl, TB, jnp.add, 0.0)
    a = ig - Fc
    cmax = _seg_scan(a, tl, TB, jnp.maximum, -jnp.inf)
    m0_cols = [m_ref[bb][:, 0:1] for bb in range(BB)]
    m0 = _lanes_of(m0_cols, TB)
    m = Fc + jnp.maximum(m0, cmax)
    Fl_cols = [Fc[:, (bb + 1) * TB - 1:(bb + 1) * TB] for bb in range(BB)]
    mL_cols = [m[:, (bb + 1) * TB - 1:(bb + 1) * TB] for bb in range(BB)]
    Fl = _lanes_of(Fl_cols, TB)
    mL = _lanes_of(mL_cols, TB)
    per_row = jnp.concatenate(
        [Fc - m, jnp.exp(m0 + Fc - m), jnp.exp(a + Fl - mL), jnp.exp(-m),
         jnp.zeros((LANES - 4 * SUB, R), F32)], axis=0).T
    Fm, decay0 = per_row[:, 0:SUB], per_row[:, SUB:2 * SUB]
    wL, emm = per_row[:, 2 * SUB:3 * SUB], per_row[:, 3 * SUB:4 * SUB]

    xp = proj(0)
    for bb in range(BB):
        ext_s[bb, HIST:HIST + TB, :] = xp[bb * TB:(bb + 1) * TB, :]

    pos_head = start + t * TB + lax.broadcasted_iota(jnp.int32, (HIST, 1), 0)
    for g, w in enumerate(POOL_WINDOWS):
        cols = slice(g * GW, (g + 1) * GW)
        inv_head = 1.0 / jnp.minimum(pos_head + 1, w).astype(F32)
        for bb in range(BB):
            ext = ext_s[bb, :, cols]
            win = ext
            s = 1
            while s < w:
                win = win + pltpu.roll(win, s, 0)
                s *= 2
            r0 = bb * TB
            head = win[HIST:2 * HIST, :] * inv_head - ext[HIST:2 * HIST, :]
            tail = win[2 * HIST:, :] * (1.0 / w) - ext[2 * HIST:, :]
            pooled_s[r0:r0 + HIST, cols] = head.astype(BF16)
            pooled_s[r0 + HIST:r0 + TB, cols] = tail.astype(BF16)

    zp_s[...] = _silu(proj(1))
    for g in range(N_POOL_GROUPS):
        cols = slice(g * GW, (g + 1) * GW)
        mixed = jnp.dot(pooled_s[:, cols], wpool_ref[g], preferred_element_type=F32)
        ycat_s[:, cols] = (mixed * pscale_ref[:, cols] * zp_s[:, cols]).astype(BF16)
    q_s[...] = proj(2).astype(BF16)
    k_s[...] = (proj(3) * (DH ** -0.5)).astype(BF16)
    v_s[...] = proj(4).astype(BF16)
    if o_gate_early:
        og_s[...] = _sigmoid(proj(5))

    for bb in range(BB):
        ext_s[bb, 0:HIST, :] = ext_s[bb, TB:TB + HIST, :]

    causal = (lax.broadcasted_iota(jnp.int32, (TB, TB), 0)
              >= lax.broadcasted_iota(jnp.int32, (TB, TB), 1))

    dLs = [jnp.exp(m0_cols[bb] + Fl_cols[bb] - mL_cols[bb]) for bb in range(BB)]
    for hd in range(N_HEADS):
        for bb in range(BB):
            rows = slice(bb * TB, (bb + 1) * TB)
            dL = dLs[bb]
            cols = slice(hd * DH, (hd + 1) * DH)
            q = q_s[rows, cols]
            k = k_s[rows, cols]
            v = v_s[rows, cols]
            s = lax.dot_general(q, k, (((1,), (1,)), ((), ())), preferred_element_type=F32)
            logD = Fm[rows, hd:hd + 1] + a[hd:hd + 1, rows]
            S = s * jnp.exp(jnp.where(causal, logD, -jnp.inf))
            C0 = C_ref[bb, hd]
            n0 = n_ref[bb, hd:hd + 1, :]
            d0 = decay0[rows, hd:hd + 1]
            qn = jnp.sum(q.astype(F32) * n0, axis=-1, keepdims=True)
            nq = jnp.sum(S, axis=-1, keepdims=True) + d0 * qn
            num = (jnp.dot(S.astype(BF16), v, preferred_element_type=F32)
                   + d0 * jnp.dot(q, C0.astype(BF16), preferred_element_type=F32))
            den = jnp.maximum(jnp.abs(nq), emm[rows, hd:hd + 1])
            hh = num * (1.0 / den)
            hc = hh - jnp.mean(hh, axis=-1, keepdims=True)
            hn = hc * lax.rsqrt(jnp.mean(hc * hc, axis=-1, keepdims=True) + EPS)
            hm = hn * ghead_ref[:, cols]
            hm_s[rows, cols] = hm * og_s[rows, cols] if o_gate_early else hm
            kw = k.astype(F32) * wL[rows, hd:hd + 1]
            dl = dL[hd:hd + 1, :]
            C_ref[bb, hd] = dl * C0 + lax.dot_general(
                kw.astype(BF16), v, (((0,), (0,)), ((), ())), preferred_element_type=F32)
            n_ref[bb, hd:hd + 1, :] = dl * n0 + jnp.sum(kw, axis=0, keepdims=True)
    for bb in range(BB):
        m_ref[bb] = jnp.broadcast_to(mL_cols[bb], (SUB, LANES))

    if not o_gate_early:
        hm_s[...] = hm_s[...] * _sigmoid(proj(5))
    ycat_s[:, W:2 * W] = (hm_s[...] * _silu(proj(6))).astype(BF16)

    out_cols = OUT_COLS if R >= MIN_ROWS_FOR_COLUMN_BLOCKS else D
    ssq = [jnp.zeros((TB, 1), F32) for _ in range(BB)]
    for c0 in range(0, D, out_cols):
        cols = slice(c0, c0 + out_cols)
        y = jnp.dot(ycat_s[...], wout_ref[:, cols], preferred_element_type=F32)
        for bb in range(BB):
            gate = mod_ref[bb, :, 2 * D + c0:2 * D + c0 + out_cols]
            xn = x_ref[bb, :, cols] + gate * y[bb * TB:(bb + 1) * TB, :]
            ssq[bb] = ssq[bb] + jnp.sum(xn * xn, axis=-1, keepdims=True)
            y_ref[bb, :, cols] = xn
    for bb in range(BB):
        r = lax.rsqrt(ssq[bb] * (1.0 / D) + EPS)
        y_ref[bb] = (y_ref[bb] * r) * gfinal_ref[...]

    @pl.when(t == pl.num_programs(1) - 1)
    def _emit_pool():
        for bb in range(BB):
            for j in range(POOL_BUF):
                pool_out_ref[j, pl.ds(b0 + bb, 1), :] = ext_s[bb, 1 + j:2 + j, :]


def _const_spec(shape):
    nd = len(shape)
    return pl.BlockSpec(shape, lambda b, t: (0,) * nd, pipeline_mode=pl.Buffered(1))


def _block_plan(n_batch, n_tokens, has_state):
    tb = min(n_tokens, MLSTM_CHUNK)
    c_bytes = N_HEADS * HEAD_DIM * HEAD_DIM * 4
    n_windows = 4 if has_state else 2
    bb = min(n_batch, MAX_STEP_ROWS // tb, STATE_WINDOW_BYTES // (n_windows * c_bytes))
    assert n_tokens % tb == 0 and n_batch % bb == 0 and tb >= 2 * HIST
    return bb, tb


def _run_layer(x, mod, mod_row0, state, weights, *, start):
    B, T, D = x.shape
    W, H, DH = W_POOL, N_HEADS, HEAD_DIM
    has_state = state is not None
    BB, TB = _block_plan(B, T, has_state)
    R = BB * TB
    grid = (B // BB, T // TB)
    assert mod_row0 % BB == 0
    mod_blk0 = mod_row0 // BB

    in_specs = [
        pl.BlockSpec((BB, TB, D), lambda b, t: (b, t, 0)),
        pl.BlockSpec((BB, 1, 3 * D), lambda b, t: (b + mod_blk0, 0, 0)),
    ]
    args = [x, mod]
    if has_state:
        pool0, C0, n0, m0 = state
        in_specs += [
            pl.BlockSpec((POOL_BUF, B, W), lambda b, t: (0, 0, 0), pipeline_mode=pl.Buffered(1)),
            pl.BlockSpec((BB, H, DH, DH), lambda b, t: (b, 0, 0, 0)),
            pl.BlockSpec((BB, H, DH), lambda b, t: (b, 0, 0)),
            pl.BlockSpec((BB, SUB, LANES), lambda b, t: (b, 0, 0)),
        ]
        m0p = jnp.broadcast_to(jnp.pad(m0, ((0, 0), (0, SUB - H)))[:, :, None], (B, SUB, LANES))
        args += [jnp.transpose(pool0, (1, 0, 2)), C0, n0, m0p]
    in_specs += [_const_spec(w.shape) for w in weights]
    args += list(weights)

    out_shape = (
        jax.ShapeDtypeStruct((B, T, D), F32),
        jax.ShapeDtypeStruct((POOL_BUF, B, W), F32),
        jax.ShapeDtypeStruct((B, H, DH, DH), F32),
        jax.ShapeDtypeStruct((B, H, DH), F32),
        jax.ShapeDtypeStruct((B, SUB, LANES), F32),
    )
    out_specs = (
        pl.BlockSpec((BB, TB, D), lambda b, t: (b, t, 0)),
        pl.BlockSpec((POOL_BUF, B, W), lambda b, t: (0, 0, 0)),
        pl.BlockSpec((BB, H, DH, DH), lambda b, t: (b, 0, 0, 0)),
        pl.BlockSpec((BB, H, DH), lambda b, t: (b, 0, 0)),
        pl.BlockSpec((BB, SUB, LANES), lambda b, t: (b, 0, 0)),
    )
    scratch = [
        pltpu.VMEM((R, D), BF16),
        pltpu.VMEM((BB, HIST + TB, W), F32),
        pltpu.VMEM((R, W), BF16),
        pltpu.VMEM((R, W), F32),
        pltpu.VMEM((R, W), F32),
        pltpu.VMEM((R, W), BF16),
        pltpu.VMEM((R, W), BF16),
        pltpu.VMEM((R, W), BF16),
        pltpu.VMEM((R, 2 * W), BF16),
        pltpu.VMEM((BB, 1, D), F32),
        pltpu.VMEM((R, W), F32),
    ]
    y, pool, C, n, m = pl.pallas_call(
        functools.partial(_layer_kernel, BB=BB, TB=TB, start=start, has_state=has_state),
        grid=grid,
        in_specs=in_specs,
        out_specs=out_specs,
        out_shape=out_shape,
        scratch_shapes=scratch,
        compiler_params=pltpu.CompilerParams(
            dimension_semantics=("arbitrary", "arbitrary"),
            vmem_limit_bytes=VMEM_LIMIT_BYTES),
        name="layer_state" if has_state else "layer_fresh",
    )(*args)
    return y, jnp.transpose(pool, (1, 0, 2)), C, n, m[:, :H, 0]


def kernel(x_prompt, x_sample, c_prompt, c_sample, state_pool, state_C, state_n, state_m,
           w_ada, b_ada, g_norm, w_in, b_i, b_f, w_pool, pool_scale, g_head, w_out, g_final):
    depth = w_ada.shape[0]
    assert depth == 1, "single-layer trunk"
    l = 0
    nbp = x_prompt.shape[0]
    n_main = N_MAIN_SECTIONS * W_POOL

    mod = _adaln_mod(c_prompt, c_sample, w_ada[l], b_ada[l])

    w_in_t = jnp.swapaxes(w_in[l], 0, 1)
    n_gate = 2 * N_HEADS
    assert n_main % n_gate == 0 and w_in_t.shape[0] == n_main + n_gate
    w_gate_t = _row_block(w_in_t, n_main // n_gate, n_gate)
    w_gate = jnp.pad(w_gate_t.T, ((0, 0), (0, LANES - n_gate))).astype(BF16)
    gate_bias = jnp.pad(jnp.concatenate([b_i[l], b_f[l]]), (0, LANES - 2 * N_HEADS)).reshape(1, LANES)
    weights = (
        g_norm[l].reshape(1, D_MODEL),
        _transposed_rows_bf16(w_in_t, n_main, W_POOL),
        w_gate,
        gate_bias,
        w_pool[l].astype(BF16),
        pool_scale[l].reshape(1, W_POOL),
        g_head[l].reshape(1, W_MLSTM),
        _rows_bf16(w_out[l], 512),
        g_final.reshape(1, D_MODEL),
    )

    yp, pp, pc, pn, pm = _run_layer(x_prompt, mod, 0, None, weights, start=0)
    ys, sp, sc, sn, sm = _run_layer(
        x_sample, mod, nbp, (state_pool[l], state_C[l], state_n[l], state_m[l]), weights,
        start=PAST_LEN)
    return (yp, ys, pp[None], pc[None], pn[None], pm[None], sp[None], sc[None], sn[None], sm[None])
```
